```python
import math
import jax
import jax.numpy as jnp
from jax import lax
import numpy as np


D_MODEL = 1024
BATCH = 4
SEQ = 8192
DEPTH = 1

GRID_W = 64
CTX_LEN = 256
S5_WIDTH = 512
S5_GROUP = 16
S5_GROUPS = S5_WIDTH // S5_GROUP
S5_STATE = 64
SGU_WIDTH = 512
SGU_GROUPS = 8
SGU_GROUP_DIM = SGU_WIDTH // SGU_GROUPS
CHUNK = 128
FFN_HIDDEN = 2816
CONV_K = 3
N_BRANCH = 2
IN_WIDTH = S5_WIDTH + 2 * SGU_WIDTH + N_BRANCH * D_MODEL
N_MOD = 6
EPS = 1e-6
DT_MIN = 1e-3
DT_MAX = 1e-1

kernel_name = 'hybrid_s5_sgu_convffn_block'


def rms_norm(x, g):
    x32 = x.astype(jnp.float32)
    y = x32 * lax.rsqrt(jnp.mean(x32 * x32, axis=-1, keepdims=True) + EPS)
    return y.astype(x.dtype) * g


def layer_norm(x, g, b):
    x32 = x.astype(jnp.float32)
    xc = x32 - jnp.mean(x32, axis=-1, keepdims=True)
    y = xc * lax.rsqrt(jnp.mean(xc * xc, axis=-1, keepdims=True) + EPS)
    return y.astype(x.dtype) * g + b


def modulate(x, g, shift, scale):
    return rms_norm(x, g) * (1 + scale) + shift


def s5_discretize(a_re, a_im, log_step, b_re, b_im):
    f32 = jnp.float32
    a_re = a_re.astype(f32)
    a_im = a_im.astype(f32)
    dt = jnp.exp(log_step.astype(f32))[:, None]
    mag = jnp.exp(a_re * dt)
    ab_re = mag * jnp.cos(a_im * dt)
    ab_im = mag * jnp.sin(a_im * dt)
    p = ab_re - 1.0
    q = ab_im
    den = a_re * a_re + a_im * a_im
    k_re = ((p * a_re + q * a_im) / den)[..., None]
    k_im = ((q * a_re - p * a_im) / den)[..., None]
    b_re = b_re.astype(f32)
    b_im = b_im.astype(f32)
    bb_re = k_re * b_re - k_im * b_im
    bb_im = k_re * b_im + k_im * b_re
    return ab_re, ab_im, bb_re, bb_im


def _linear_recurrence_combine(e1, e2):
    a1r, a1i, b1r, b1i = e1
    a2r, a2i, b2r, b2i = e2
    return (a2r * a1r - a2i * a1i,
            a2r * a1i + a2i * a1r,
            a2r * b1r - a2i * b1i + b2r,
            a2r * b1i + a2i * b1r + b2i)


def s5_scan(u, ab_re, ab_im, bb_re, bb_im, s0=None):
    bsz, length, _ = u.shape
    ug = u.astype(jnp.float32).reshape(bsz, length, S5_GROUPS, S5_GROUP)
    bu_re = jnp.einsum('blgh,gnh->blgn', ug, bb_re)
    bu_im = jnp.einsum('blgh,gnh->blgn', ug, bb_im)
    if s0 is not None:
        s0_re, s0_im = s0
        bu_re = bu_re.at[:, 0].add(ab_re * s0_re - ab_im * s0_im)
        bu_im = bu_im.at[:, 0].add(ab_re * s0_im + ab_im * s0_re)
    a_re = jnp.broadcast_to(ab_re, (1, length) + ab_re.shape)
    a_im = jnp.broadcast_to(ab_im, (1, length) + ab_im.shape)
    _, _, h_re, h_im = lax.associative_scan(
        _linear_recurrence_combine, (a_re, a_im, bu_re, bu_im), axis=1)
    return h_re, h_im


def s5_readout(h_re, h_im, c_re, c_im):
    y = (jnp.einsum('blgn,ghn->blgh', h_re, c_re.astype(jnp.float32))
         - jnp.einsum('blgn,ghn->blgh', h_im, c_im.astype(jnp.float32)))
    return y.reshape(y.shape[0], y.shape[1], S5_WIDTH)


def s5_glu(y, w, b):
    y = jax.nn.gelu(y)
    return y * jax.nn.sigmoid(y @ w + b)


def s5_mixer(u, u_ctx, a_re, a_im, log_step, b_re, b_im, c_re, c_im, d_skip, w_glu, b_glu, with_ctx_out):
    f32 = jnp.float32
    y = u.astype(f32) * d_skip.astype(f32)
    y_ctx = u_ctx.astype(f32) * d_skip.astype(f32) if with_ctx_out else None
    for direction in range(2):
        ab_re, ab_im, bb_re, bb_im = s5_discretize(
            a_re[direction], a_im[direction], log_step[direction], b_re[direction], b_im[direction])
        orient = (lambda t: jnp.flip(t, axis=1)) if direction == 1 else (lambda t: t)
        hc_re, hc_im = s5_scan(orient(u_ctx), ab_re, ab_im, bb_re, bb_im)
        h_re, h_im = s5_scan(orient(u), ab_re, ab_im, bb_re, bb_im,
                             s0=(hc_re[:, -1], hc_im[:, -1]))
        y = y + orient(s5_readout(h_re, h_im, c_re[direction], c_im[direction]))
        if with_ctx_out:
            y_ctx = y_ctx + orient(s5_readout(hc_re, hc_im, c_re[direction], c_im[direction]))
    y = s5_glu(y.astype(u.dtype), w_glu, b_glu)
    if with_ctx_out:
        y_ctx = s5_glu(y_ctx.astype(u.dtype), w_glu, b_glu)
    return y, y_ctx


def sgu_mixer(z, ln_g, ln_b, w_sp, b_sp):
    u, v = jnp.split(z, 2, axis=-1)
    v = layer_norm(v, ln_g, ln_b)
    bsz, length, _ = v.shape
    v = v.reshape(bsz, length // CHUNK, CHUNK, SGU_GROUPS, SGU_GROUP_DIM)
    s = jnp.einsum('bcpgd,gqp->bcqgd', v, w_sp) + jnp.transpose(b_sp)[None, None, :, :, None]
    return u * s.reshape(bsz, length, SGU_WIDTH)


def merge_branches(y_a, y_b, gate_logits, w_proj_a, w_proj_b, b_gate, w_out):
    g_a, g_b = jnp.split(jax.nn.sigmoid(gate_logits + b_gate), N_BRANCH, axis=-1)
    return (g_a * (y_a @ w_proj_a) + g_b * (y_b @ w_proj_b)) @ w_out


def depthwise_conv_grid(u, w, b, rows, cols):
    bsz, _, ch = u.shape
    y = lax.conv_general_dilated(
        u.reshape(bsz, rows, cols, ch), w[:, :, None, :], (1, 1), 'SAME',
        dimension_numbers=('NHWC', 'HWIO', 'NHWC'), feature_group_count=ch)
    return y.reshape(bsz, rows * cols, ch) + b


def conv_ffn(h, w_up, conv_w, conv_b, w_down, rows, cols):
    up = depthwise_conv_grid(h @ w_up, conv_w, conv_b, rows, cols)
    gate, val = jnp.split(up, 2, axis=-1)
    return (jax.nn.silu(gate) * val) @ w_down


def trunk_layer(x, xc, c_silu, cc_silu, rows, w_ada, b_ada, g_mix, w_in,
                s5_a_re, s5_a_im, s5_log_step, s5_b_re, s5_b_im, s5_c_re, s5_c_im,
                s5_d, s5_w_glu, s5_b_glu, sgu_ln_g, sgu_ln_b, sgu_w, sgu_b,
                w_proj_a, w_proj_b, b_gate, w_out, g_ffn, w_up, conv_w, conv_b, w_down,
                update_ctx):
    mod = c_silu @ w_ada + b_ada
    sh1, sc1, ga1, sh2, sc2, ga2 = jnp.split(mod[:, None, :], N_MOD, axis=-1)
    mod_c = cc_silu @ w_ada + b_ada
    sh1c, sc1c, ga1c, sh2c, sc2c, ga2c = jnp.split(mod_c, N_MOD, axis=-1)

    h = modulate(x, g_mix, sh1, sc1)
    hc = modulate(xc, g_mix, sh1c, sc1c)
    proj = h @ w_in
    u_a = proj[..., :S5_WIDTH]
    z_b = jax.nn.gelu(proj[..., S5_WIDTH:S5_WIDTH + 2 * SGU_WIDTH])
    gate_logits = proj[..., S5_WIDTH + 2 * SGU_WIDTH:]
    proj_c = hc @ (w_in if update_ctx else w_in[:, :S5_WIDTH])
    u_ac = proj_c[..., :S5_WIDTH]
    y_a, y_ac = s5_mixer(u_a, u_ac, s5_a_re, s5_a_im, s5_log_step, s5_b_re, s5_b_im,
                         s5_c_re, s5_c_im, s5_d, s5_w_glu, s5_b_glu, update_ctx)
    y_b = sgu_mixer(z_b, sgu_ln_g, sgu_ln_b, sgu_w, sgu_b)
    x = x + ga1 * merge_branches(y_a, y_b, gate_logits, w_proj_a, w_proj_b, b_gate, w_out)

    h2 = modulate(x, g_ffn, sh2, sc2)
    x = x + ga2 * conv_ffn(h2, w_up, conv_w, conv_b, w_down, rows, GRID_W)

    if update_ctx:
        z_bc = jax.nn.gelu(proj_c[..., S5_WIDTH:S5_WIDTH + 2 * SGU_WIDTH])
        gate_c = proj_c[..., S5_WIDTH + 2 * SGU_WIDTH:]
        y_bc = sgu_mixer(z_bc, sgu_ln_g, sgu_ln_b, sgu_w, sgu_b)
        xc = xc + ga1c * merge_branches(y_ac, y_bc, gate_c, w_proj_a, w_proj_b, b_gate, w_out)
        h2c = modulate(xc, g_ffn, sh2c, sc2c)
        mid = CONV_K // 2
        xc = xc + ga2c * conv_ffn(h2c, w_up, conv_w[mid:mid + 1], conv_b, w_down, 1, xc.shape[1])
    else:
        xc = None
    return x, xc


def setup_inputs(seed: int = 0) -> dict:
    key = jax.random.key(seed)
    ks = jax.random.split(key, 40)
    f32 = jnp.float32
    D, L = D_MODEL, DEPTH
    G, N, H = S5_GROUPS, S5_STATE, S5_GROUP
    F2 = 2 * FFN_HIDDEN

    def nrm(k, shape, scale):
        return jax.random.normal(k, shape, f32) * scale

    n_idx = jnp.arange(N, dtype=f32)
    return {
        'x': nrm(ks[0], (BATCH, SEQ, D), 1.0),
        'c': nrm(ks[1], (BATCH, D), 1.0),
        'ctx': nrm(ks[2], (BATCH, CTX_LEN, D), 1.0),
        'c_ctx': nrm(ks[3], (D,), 1.0),
        'w_ada': nrm(ks[4], (L, D, N_MOD * D), 0.5 * D ** -0.5),
        'b_ada': nrm(ks[5], (L, N_MOD * D), 0.02),
        'g_mix': 1.0 + nrm(ks[6], (L, D), 0.02),
        'w_in': nrm(ks[7], (L, D, IN_WIDTH), D ** -0.5),
        's5_a_re': -0.5 + nrm(ks[8], (L, 2, G, N), 0.01),
        's5_a_im': math.pi * n_idx + nrm(ks[9], (L, 2, G, N), 0.01),
        's5_log_step': jax.random.uniform(ks[10], (L, 2, G), f32, math.log(DT_MIN), math.log(DT_MAX)),
        's5_b_re': nrm(ks[11], (L, 2, G, N, H), (2 * H) ** -0.5),
        's5_b_im': nrm(ks[12], (L, 2, G, N, H), (2 * H) ** -0.5),
        's5_c_re': nrm(ks[13], (L, 2, G, H, N), N ** -0.5),
        's5_c_im': nrm(ks[14], (L, 2, G, H, N), N ** -0.5),
        's5_d': nrm(ks[15], (L, S5_WIDTH), 1.0),
        's5_w_glu': nrm(ks[16], (L, S5_WIDTH, S5_WIDTH), S5_WIDTH ** -0.5),
        's5_b_glu': nrm(ks[17], (L, S5_WIDTH), 0.02),
        'sgu_ln_g': 1.0 + nrm(ks[18], (L, SGU_WIDTH), 0.02),
        'sgu_ln_b': nrm(ks[19], (L, SGU_WIDTH), 0.02),
        'sgu_w': nrm(ks[20], (L, SGU_GROUPS, CHUNK, CHUNK), CHUNK ** -0.5),
        'sgu_b': 1.0 + nrm(ks[21], (L, SGU_GROUPS, CHUNK), 0.1),
        'w_proj_a': nrm(ks[22], (L, S5_WIDTH, D), S5_WIDTH ** -0.5),
        'w_proj_b': nrm(ks[23], (L, SGU_WIDTH, D), SGU_WIDTH ** -0.5),
        'b_gate': nrm(ks[24], (L, N_BRANCH * D), 0.02),
        'w_out': nrm(ks[25], (L, D, D), D ** -0.5),
        'g_ffn': 1.0 + nrm(ks[26], (L, D), 0.02),
        'w_up': nrm(ks[27], (L, D, F2), D ** -0.5),
        'conv_w': nrm(ks[28], (L, CONV_K, CONV_K, F2), 1.0 / CONV_K),
        'conv_b': nrm(ks[29], (L, F2), 0.02),
        'w_down': nrm(ks[30], (L, FFN_HIDDEN, D), FFN_HIDDEN ** -0.5),
        'g_final': 1.0 + nrm(ks[31], (D,), 0.02),
    }


def reference(x, c, ctx, c_ctx, w_ada, b_ada, g_mix, w_in, s5_a_re, s5_a_im, s5_log_step,
              s5_b_re, s5_b_im, s5_c_re, s5_c_im, s5_d, s5_w_glu, s5_b_glu,
              sgu_ln_g, sgu_ln_b, sgu_w, sgu_b, w_proj_a, w_proj_b, b_gate, w_out,
              g_ffn, w_up, conv_w, conv_b, w_down, g_final):
    rows = x.shape[1] // GRID_W
    c_silu = jax.nn.silu(c)
    cc_silu = jax.nn.silu(c_ctx)
    xc = ctx
    for i in range(DEPTH):
        x, xc = trunk_layer(
            x, xc, c_silu, cc_silu, rows, w_ada[i], b_ada[i], g_mix[i], w_in[i],
            s5_a_re[i], s5_a_im[i], s5_log_step[i], s5_b_re[i], s5_b_im[i], s5_c_re[i], s5_c_im[i],
            s5_d[i], s5_w_glu[i], s5_b_glu[i], sgu_ln_g[i], sgu_ln_b[i], sgu_w[i], sgu_b[i],
            w_proj_a[i], w_proj_b[i], b_gate[i], w_out[i], g_ffn[i], w_up[i], conv_w[i], conv_b[i],
            w_down[i], i + 1 < DEPTH)
    return rms_norm(x, g_final)
```

```python
import functools
import math

import jax
import jax.numpy as jnp
from jax import lax
from jax.experimental import pallas as pl
from jax.experimental.pallas import tpu as pltpu

F32 = jnp.float32
BF16 = jnp.bfloat16

D_MODEL = 1024
GRID_W = 64
S5_WIDTH = 512
S5_GROUP = 16
S5_GROUPS = S5_WIDTH // S5_GROUP
S5_STATE = 64
SGU_WIDTH = 512
SGU_GROUPS = 8
SGU_GROUP_DIM = SGU_WIDTH // SGU_GROUPS
CHUNK = 128
FFN_HIDDEN = 2816
N_MOD = 6
EPS = 1e-6

S5_CHUNK = 16
S5_PAIRS = S5_GROUPS // 2
PAIR_W = 2 * S5_CHUNK * S5_GROUP
STATE_W = S5_GROUPS * S5_STATE

V7X_VMEM_LIMIT = 56 * 1024 * 1024

TOK_TILE = 512
FFN_CHUNK = 256
SCAN_LANES = 256


def _cparams(sem):
    return pltpu.CompilerParams(dimension_semantics=sem, vmem_limit_bytes=V7X_VMEM_LIMIT)


def _const_spec(shape):
    zeros = (0,) * len(shape)
    return pl.BlockSpec(shape, lambda *_: zeros, pipeline_mode=pl.Buffered(1))


def _gelu_tanh(x):
    c = math.sqrt(2.0 / math.pi)
    return 0.5 * x * (1.0 + jnp.tanh(c * (x + 0.044715 * (x * x * x))))


def _modulated_rms(x, g, shift, scale):
    ms = jnp.mean(x * x, axis=-1, keepdims=True)
    return (x * lax.rsqrt(ms + EPS)) * (g * (1.0 + scale)) + shift


def _bdot(a, b):
    return jnp.dot(a, b, preferred_element_type=F32)


def _ada_kernel(c_ref, w_ref, b_ref, o_ref):
    c = c_ref[...]
    cs = c * jax.nn.sigmoid(c)
    o_ref[...] = jnp.dot(cs, w_ref[...], preferred_element_type=F32,
                         precision=lax.Precision.HIGHEST) + b_ref[...]


def _ada_rows(c_rows, w_ada, b_ada):
    n = w_ada.shape[1]
    bn = 1536
    return pl.pallas_call(
        _ada_kernel,
        grid=(n // bn,),
        in_specs=[_const_spec(c_rows.shape),
                  pl.BlockSpec((D_MODEL, bn), lambda j: (0, j)),
                  pl.BlockSpec((1, bn), lambda j: (0, j))],
        out_specs=pl.BlockSpec((c_rows.shape[0], bn), lambda j: (0, j)),
        out_shape=jax.ShapeDtypeStruct((c_rows.shape[0], n), F32),
        compiler_params=_cparams(("arbitrary",)),
        name="ada",
    )(c_rows, w_ada, b_ada.reshape(1, n))


def _inproj_kernel(x_ref, mod_ref, g_ref, w_ref, bgate_ref, lng_ref, lnb_ref, wsp_ref, bsp_ref,
                   wpb_ref, u_ref, ga_ref, mb_ref):
    m = mod_ref[0]
    h = _modulated_rms(x_ref[0], g_ref[...], m[0:1], m[1:2]).astype(BF16)
    tt = h.shape[0]

    u_ref[0] = _bdot(h, w_ref[:, 0:S5_WIDTH]).astype(BF16)

    z0 = S5_WIDTH
    zu = _gelu_tanh(_bdot(h, w_ref[:, z0:z0 + SGU_WIDTH]))
    zv = _gelu_tanh(_bdot(h, w_ref[:, z0 + SGU_WIDTH:z0 + 2 * SGU_WIDTH]))
    mu = jnp.mean(zv, axis=-1, keepdims=True)
    zc = zv - mu
    var = jnp.mean(zc * zc, axis=-1, keepdims=True)
    v = ((zc * lax.rsqrt(var + EPS)) * lng_ref[...] + lnb_ref[...]).astype(BF16)

    lane = lax.broadcasted_iota(jnp.int32, (CHUNK, 128), 1)
    zero = jnp.zeros((CHUNK, 128), BF16)
    rows = []
    for c in range(tt // CHUNK):
        tiles = []
        for j in range(SGU_WIDTH // 128):
            vj = v[c * CHUNK:(c + 1) * CHUNK, j * 128:(j + 1) * 128]
            rhs = jnp.concatenate([jnp.where(lane < SGU_GROUP_DIM, vj, zero),
                                   jnp.where(lane >= SGU_GROUP_DIM, vj, zero)], axis=0)
            tiles.append(_bdot(wsp_ref[j], rhs))
        rows.append(jnp.concatenate(tiles, axis=1) + bsp_ref[...])
    s = jnp.concatenate(rows, axis=0)
    yb = (zu * s).astype(BF16)
    pb = _bdot(yb, wpb_ref[...])

    g0 = S5_WIDTH + 2 * SGU_WIDTH
    ga = jax.nn.sigmoid(_bdot(h, w_ref[:, g0:g0 + D_MODEL]) + bgate_ref[:, 0:D_MODEL])
    gb = jax.nn.sigmoid(_bdot(h, w_ref[:, g0 + D_MODEL:g0 + 2 * D_MODEL])
                        + bgate_ref[:, D_MODEL:2 * D_MODEL])
    ga_ref[0] = ga
    mb_ref[0] = gb * pb


def _inproj(x, mod3, g_mix, w_in, b_gate, ln_g, ln_b, wsp_pair, bsp_rows, w_proj_b):
    bsz, length, _ = x.shape
    tt = TOK_TILE
    tok = lambda w: pl.BlockSpec((1, tt, w), lambda b, i: (b, i, 0))
    return pl.pallas_call(
        _inproj_kernel,
        grid=(bsz, length // tt),
        in_specs=[tok(D_MODEL),
                  pl.BlockSpec((1, N_MOD, D_MODEL), lambda b, i: (b, 0, 0)),
                  _const_spec(g_mix.shape), _const_spec(w_in.shape), _const_spec(b_gate.shape),
                  _const_spec(ln_g.shape), _const_spec(ln_b.shape), _const_spec(wsp_pair.shape),
                  _const_spec(bsp_rows.shape), _const_spec(w_proj_b.shape)],
        out_specs=[tok(S5_WIDTH), tok(D_MODEL), tok(D_MODEL)],
        out_shape=[jax.ShapeDtypeStruct((bsz, length, S5_WIDTH), BF16),
                   jax.ShapeDtypeStruct((bsz, length, D_MODEL), F32),
                   jax.ShapeDtypeStruct((bsz, length, D_MODEL), F32)],
        compiler_params=_cparams(("parallel", "parallel")),
        name="inproj",
    )(x, mod3, g_mix, w_in, b_gate, ln_g, ln_b, wsp_pair, bsp_rows, w_proj_b)


def _ctxproj_kernel(x_ref, mod_ref, g_ref, w_ref, u_ref):
    m = mod_ref[0]
    h = _modulated_rms(x_ref[0], g_ref[...], m[0:1], m[1:2]).astype(BF16)
    u_ref[0] = _bdot(h, w_ref[...]).astype(BF16)


def _ctxproj(ctx, mod3, ctx_row, g_mix, w_in_u):
    bsz, length, _ = ctx.shape
    return pl.pallas_call(
        _ctxproj_kernel,
        grid=(bsz,),
        in_specs=[pl.BlockSpec((1, length, D_MODEL), lambda b: (b, 0, 0)),
                  pl.BlockSpec((1, N_MOD, D_MODEL), lambda b: (ctx_row, 0, 0)),
                  _const_spec(g_mix.shape), _const_spec(w_in_u.shape)],
        out_specs=pl.BlockSpec((1, length, S5_WIDTH), lambda b: (b, 0, 0)),
        out_shape=jax.ShapeDtypeStruct((bsz, length, S5_WIDTH), BF16),
        compiler_params=_cparams(("parallel",)),
        name="ctxproj",
    )(ctx, mod3, g_mix, w_in_u)


def _s5_operators(a_re, a_im, log_step, b_re, b_im, c_re, c_im, d_skip):
    hp = lax.Precision.HIGHEST
    n_tok = S5_CHUNK
    dt = jnp.exp(log_step)[..., None]
    mag = jnp.exp(a_re * dt)
    lam_re = mag * jnp.cos(a_im * dt)
    lam_im = mag * jnp.sin(a_im * dt)
    p = lam_re - 1.0
    q = lam_im
    den = a_re * a_re + a_im * a_im
    k_re = ((p * a_re + q * a_im) / den)[..., None]
    k_im = ((q * a_re - p * a_im) / den)[..., None]
    bb_re = k_re * b_re - k_im * b_im
    bb_im = k_re * b_im + k_im * b_re

    pw_re = [jnp.ones_like(lam_re)]
    pw_im = [jnp.zeros_like(lam_im)]
    for _ in range(n_tok):
        r, i = pw_re[-1], pw_im[-1]
        pw_re.append(r * lam_re - i * lam_im)
        pw_im.append(r * lam_im + i * lam_re)
    pw_re = jnp.stack(pw_re)
    pw_im = jnp.stack(pw_im)

    cb_re = jnp.einsum('dgon,dgnh->dgnho', c_re, bb_re, precision=hp) \
        - jnp.einsum('dgon,dgnh->dgnho', c_im, bb_im, precision=hp)
    cb_im = jnp.einsum('dgon,dgnh->dgnho', c_re, bb_im, precision=hp) \
        + jnp.einsum('dgon,dgnh->dgnho', c_im, bb_re, precision=hp)
    kt = jnp.einsum('tdgn,dgnho->tdgho', pw_re[:n_tok], cb_re, precision=hp) \
        - jnp.einsum('tdgn,dgnho->tdgho', pw_im[:n_tok], cb_im, precision=hp)

    ii = jnp.arange(n_tok)[:, None]
    jj = jnp.arange(n_tok)[None, :]
    zeros_k = jnp.zeros_like(kt[:1, 0])
    kf = jnp.concatenate([jnp.broadcast_to(zeros_k, (n_tok - 1,) + zeros_k.shape[1:]), kt[:, 0]], 0)
    kb = jnp.concatenate([jnp.broadcast_to(zeros_k, (n_tok - 1,) + zeros_k.shape[1:]), kt[:, 1]], 0)
    t_f = kf[(jj - ii) + n_tok - 1]
    t_b = kb[(ii - jj) + n_tok - 1]
    eye_tok = jnp.eye(n_tok, dtype=F32)[:, :, None, None, None]
    eye_ch = jnp.eye(S5_GROUP, dtype=F32)[None, None, None]
    skip = eye_tok * eye_ch * d_skip.reshape(S5_GROUPS, S5_GROUP)[None, None, :, :, None]
    w_intra = (t_f + t_b + skip).transpose(2, 0, 3, 1, 4).reshape(
        S5_GROUPS, n_tok * S5_GROUP, n_tok * S5_GROUP)

    def state_cols(pr, pi, d):
        re = pr[:, :, :, None] * bb_re[d][None] - pi[:, :, :, None] * bb_im[d][None]
        im = pr[:, :, :, None] * bb_im[d][None] + pi[:, :, :, None] * bb_re[d][None]
        to = lambda t: t.transpose(1, 0, 3, 2).reshape(S5_GROUPS, n_tok * S5_GROUP, S5_STATE)
        return to(re), to(im)
    f_re, f_im = state_cols(pw_re[n_tok - 1::-1, 0][:n_tok], pw_im[n_tok - 1::-1, 0][:n_tok], 0)
    b_re_, b_im_ = state_cols(pw_re[:n_tok, 1], pw_im[:n_tok, 1], 1)
    w_st = jnp.stack([f_re, f_im, b_re_, b_im_], axis=2)

    def read_rows(pr, pi, d):
        re = c_re[d][None] * pr[:, :, None, :] - c_im[d][None] * pi[:, :, None, :]
        im = c_re[d][None] * pi[:, :, None, :] + c_im[d][None] * pr[:, :, None, :]
        to = lambda t: t.transpose(1, 3, 0, 2).reshape(S5_GROUPS, S5_STATE, n_tok * S5_GROUP)
        return to(re), to(-im)
    rf_re, rf_im = read_rows(pw_re[1:n_tok + 1, 0], pw_im[1:n_tok + 1, 0], 0)
    rb_re, rb_im = read_rows(pw_re[n_tok:0:-1, 1], pw_im[n_tok:0:-1, 1], 1)
    w_rd = jnp.stack([rf_re, rf_im, rb_re, rb_im], axis=1)

    eye2 = jnp.eye(2, dtype=F32)
    blk = n_tok * S5_GROUP
    w_intra_p = jnp.einsum('pgrc,gk->pgrkc', w_intra.reshape(S5_PAIRS, 2, blk, blk), eye2)
    w_intra_p = w_intra_p.reshape(S5_PAIRS, PAIR_W, PAIR_W)
    w_state_p = jnp.einsum('pgrcn,gk->pgrckn', w_st.reshape(S5_PAIRS, 2, blk, 4, S5_STATE), eye2)
    w_state_p = w_state_p.reshape(S5_PAIRS, PAIR_W, 4 * 2 * S5_STATE)
    w_read_p = jnp.einsum('pgcnr,gk->pcgnkr', w_rd.reshape(S5_PAIRS, 2, 4, S5_STATE, blk), eye2)
    w_read_p = w_read_p.reshape(S5_PAIRS, 4 * 2 * S5_STATE, PAIR_W)
    w_y = jnp.concatenate([w_intra_p, w_read_p], axis=1)

    lam16_re = pw_re[n_tok].reshape(2, 1, STATE_W)
    lam16_im = pw_im[n_tok].reshape(2, 1, STATE_W)
    p_re = jnp.broadcast_to(lam16_re, (2, 4, STATE_W)).reshape(8, STATE_W)
    p_im = jnp.broadcast_to(lam16_im, (2, 4, STATE_W)).reshape(8, STATE_W)
    return w_state_p.astype(BF16), w_y.astype(BF16), p_re, p_im


def _s5_state_kernel(u_ref, w_ref, s_ref):
    s_ref[0] = _bdot(u_ref[0], w_ref[0])


def _s5_state(u_pairs, w_state):
    npair, rows, _ = u_pairs.shape
    blk = lambda: pl.BlockSpec((1, rows, PAIR_W), lambda p: (p, 0, 0))
    return pl.pallas_call(
        _s5_state_kernel,
        grid=(npair,),
        in_specs=[blk(), pl.BlockSpec((1, PAIR_W, PAIR_W), lambda p: (p, 0, 0))],
        out_specs=blk(),
        out_shape=jax.ShapeDtypeStruct((npair, rows, PAIR_W), F32),
        compiler_params=_cparams(("parallel",)),
        name="s5_state",
    )(u_pairs, w_state)


def _s5_scan_kernel(zre_ref, zim_ref, pre_ref, pim_ref, hre_ref, him_ref):
    pr = pre_ref[...]
    pi = pim_ref[...]
    steps = zre_ref.shape[0]

    def body(k, carry):
        hr, hi = carry
        hre_ref[k] = hr
        him_ref[k] = hi
        return (pr * hr - pi * hi + zre_ref[k], pr * hi + pi * hr + zim_ref[k])

    zero = jnp.zeros(pr.shape, F32)
    lax.fori_loop(0, steps, body, (zero, zero), unroll=4)


def _s5_scan(z_re, z_im, p_re, p_im):
    steps, rows, width = z_re.shape
    slab = lambda: pl.BlockSpec((steps, rows, SCAN_LANES), lambda j: (0, 0, j))
    coef = lambda: pl.BlockSpec((rows, SCAN_LANES), lambda j: (0, j))
    return pl.pallas_call(
        _s5_scan_kernel,
        grid=(width // SCAN_LANES,),
        in_specs=[slab(), slab(), coef(), coef()],
        out_specs=[slab(), slab()],
        out_shape=[jax.ShapeDtypeStruct(z_re.shape, F32)] * 2,
        compiler_params=_cparams(("parallel",)),
        name="s5_scan",
    )(z_re, z_im, p_re, p_im)


def _s5_out_kernel(u_ref, s_ref, w_ref, y_ref):
    y_ref[0] = _bdot(u_ref[0], w_ref[0, 0:PAIR_W]) + _bdot(s_ref[0], w_ref[0, PAIR_W:2 * PAIR_W])


def _s5_out(u_pairs, s_in, w_y):
    npair, rows, _ = u_pairs.shape
    blk = lambda: pl.BlockSpec((1, rows, PAIR_W), lambda p: (p, 0, 0))
    return pl.pallas_call(
        _s5_out_kernel,
        grid=(npair,),
        in_specs=[blk(), blk(), pl.BlockSpec((1, 2 * PAIR_W, PAIR_W), lambda p: (p, 0, 0))],
        out_specs=blk(),
        out_shape=jax.ShapeDtypeStruct((npair, rows, PAIR_W), F32),
        compiler_params=_cparams(("parallel",)),
        name="s5_out",
    )(u_pairs, s_in, w_y)


def _to_pairs(u):
    bsz, length, _ = u.shape
    t = u.reshape(bsz, length // S5_CHUNK, S5_CHUNK, S5_PAIRS, 2, S5_GROUP)
    t = t.transpose(3, 0, 1, 4, 2, 5)
    return t.reshape(S5_PAIRS, bsz * (length // S5_CHUNK), PAIR_W)


def _from_pairs(y, bsz, length):
    t = y.reshape(S5_PAIRS, bsz, length // S5_CHUNK, 2, S5_CHUNK, S5_GROUP)
    t = t.transpose(1, 2, 4, 0, 3, 5)
    return t.reshape(bsz, length, S5_WIDTH)


def _state_components(s, bsz):
    chunks = s.shape[1] // bsz
    t = s.reshape(S5_PAIRS, bsz, chunks, 4, 2 * S5_STATE)
    return t.transpose(3, 2, 1, 0, 4).reshape(4, chunks, bsz, STATE_W)


def _mix_kernel(y_ref, x_ref, ga_ref, mb_ref, mod_ref, wglu_ref, bglu_ref, wpa_ref, wout_ref, o_ref):
    m = mod_ref[0]
    ya = _gelu_tanh(y_ref[0])
    glu = ya * jax.nn.sigmoid(_bdot(ya.astype(BF16), wglu_ref[...]) + bglu_ref[...])
    pa = _bdot(glu.astype(BF16), wpa_ref[...])
    merged = (ga_ref[0] * pa + mb_ref[0]).astype(BF16)
    o_ref[0] = x_ref[0] + m[2:3] * _bdot(merged, wout_ref[...])


def _mix(y, x, ga, mb, mod3, w_glu, b_glu, w_proj_a, w_out):
    bsz, length, _ = x.shape
    tt = TOK_TILE
    tok = lambda w: pl.BlockSpec((1, tt, w), lambda b, i: (b, i, 0))
    return pl.pallas_call(
        _mix_kernel,
        grid=(bsz, length // tt),
        in_specs=[tok(S5_WIDTH), tok(D_MODEL), tok(D_MODEL), tok(D_MODEL),
                  pl.BlockSpec((1, N_MOD, D_MODEL), lambda b, i: (b, 0, 0)),
                  _const_spec(w_glu.shape), _const_spec(b_glu.shape),
                  _const_spec(w_proj_a.shape), _const_spec(w_out.shape)],
        out_specs=tok(D_MODEL),
        out_shape=jax.ShapeDtypeStruct(x.shape, F32),
        compiler_params=_cparams(("parallel", "parallel")),
        name="mix",
    )(y, x, ga, mb, mod3, w_glu, b_glu, w_proj_a, w_out)


def _grid_conv(u, w9, bias, tt):
    n = u.shape[0]
    col = lax.broadcasted_iota(jnp.int32, (n, 1), 0) % GRID_W
    left = jnp.where(col == 0, 0.0, pltpu.roll(u, 1, 0))
    right = jnp.where(col == GRID_W - 1, 0.0, pltpu.roll(u, n - 1, 0))
    acc = bias
    for dr in range(3):
        lo = dr * GRID_W
        acc = acc + (left[lo:lo + tt] * w9[3 * dr:3 * dr + 1]
                     + u[lo:lo + tt] * w9[3 * dr + 1:3 * dr + 2]
                     + right[lo:lo + tt] * w9[3 * dr + 2:3 * dr + 3])
    return acc


def _ffn_kernel(top_ref, x_ref, bot_ref, mod_ref, g_ref, wup_ref, cw_ref, cb_ref, wdn_ref, gf_ref,
                o_ref, h_scr, act_scr):
    i = pl.program_id(1)
    n_i = pl.num_programs(1)
    m = mod_ref[0]
    g = g_ref[...]
    tt = x_ref.shape[1]
    x = x_ref[0]

    def norm(t):
        return _modulated_rms(t, g, m[3:4], m[4:5])

    h_scr[0:GRID_W] = jnp.where(i > 0, norm(top_ref[0]), 0.0).astype(BF16)
    h_scr[GRID_W:GRID_W + tt] = norm(x).astype(BF16)
    h_scr[GRID_W + tt:] = jnp.where(i < n_i - 1, norm(bot_ref[0]), 0.0).astype(BF16)
    h = h_scr[...]

    fc = FFN_CHUNK
    for j in range(FFN_HIDDEN // fc):
        gs = slice(j * fc, (j + 1) * fc)
        vs = slice(FFN_HIDDEN + j * fc, FFN_HIDDEN + (j + 1) * fc)
        cg = _grid_conv(_bdot(h, wup_ref[:, gs]), cw_ref[:, gs], cb_ref[:, gs], tt)
        cv = _grid_conv(_bdot(h, wup_ref[:, vs]), cw_ref[:, vs], cb_ref[:, vs], tt)
        act_scr[:, gs] = ((cg * jax.nn.sigmoid(cg)) * cv).astype(BF16)

    x2 = x + m[5:6] * _bdot(act_scr[...], wdn_ref[...])
    ms = jnp.mean(x2 * x2, axis=-1, keepdims=True)
    o_ref[0] = (x2 * lax.rsqrt(ms + EPS)) * gf_ref[...]


def _ffn(x1, mod3, g_ffn, w_up, conv_w9, conv_b, w_down, g_final):
    bsz, length, _ = x1.shape
    tt = TOK_TILE
    rows_per_tile = tt // GRID_W
    n_rows = length // GRID_W
    halo = lambda fn: pl.BlockSpec((1, GRID_W, D_MODEL), fn)
    return pl.pallas_call(
        _ffn_kernel,
        grid=(bsz, length // tt),
        in_specs=[halo(lambda b, i: (b, jnp.maximum(i * rows_per_tile - 1, 0), 0)),
                  pl.BlockSpec((1, tt, D_MODEL), lambda b, i: (b, i, 0)),
                  halo(lambda b, i: (b, jnp.minimum((i + 1) * rows_per_tile, n_rows - 1), 0)),
                  pl.BlockSpec((1, N_MOD, D_MODEL), lambda b, i: (b, 0, 0)),
                  _const_spec(g_ffn.shape), _const_spec(w_up.shape), _const_spec(conv_w9.shape),
                  _const_spec(conv_b.shape), _const_spec(w_down.shape), _const_spec(g_final.shape)],
        out_specs=pl.BlockSpec((1, tt, D_MODEL), lambda b, i: (b, i, 0)),
        out_shape=jax.ShapeDtypeStruct(x1.shape, F32),
        scratch_shapes=[pltpu.VMEM((tt + 2 * GRID_W, D_MODEL), BF16),
                        pltpu.VMEM((tt, FFN_HIDDEN), BF16)],
        compiler_params=_cparams(("parallel", "parallel")),
        name="ffn",
    )(x1, x1, x1, mod3, g_ffn, w_up, conv_w9, conv_b, w_down, g_final)


def kernel(x, c, ctx, c_ctx, w_ada, b_ada, g_mix, w_in, s5_a_re, s5_a_im, s5_log_step, s5_b_re, s5_b_im, s5_c_re, s5_c_im, s5_d, s5_w_glu, s5_b_glu, sgu_ln_g, sgu_ln_b, sgu_w, sgu_b, w_proj_a, w_proj_b, b_gate, w_out, g_ffn, w_up, conv_w, conv_b, w_down, g_final):
    bsz, length, _ = x.shape
    ctx_len = ctx.shape[1]
    row = lambda t: t.reshape(1, -1)

    ctx_row = bsz
    c_rows = jnp.zeros((8, D_MODEL), F32).at[:bsz].set(c).at[ctx_row].set(c_ctx)
    mod3 = _ada_rows(c_rows, w_ada[0], b_ada[0]).reshape(8, N_MOD, D_MODEL)

    w_in_b = w_in[0].astype(BF16)
    wsp = sgu_w[0].astype(BF16)
    wsp_pair = jnp.concatenate([wsp[0::2], wsp[1::2]], axis=-1)
    bsp_rows = jnp.repeat(sgu_b[0].T, SGU_GROUP_DIM, axis=1)

    u, ga, mb = _inproj(x, mod3, row(g_mix[0]), w_in_b, row(b_gate[0]), row(sgu_ln_g[0]),
                        row(sgu_ln_b[0]), wsp_pair, bsp_rows, w_proj_b[0].astype(BF16))
    u_ctx = _ctxproj(ctx, mod3, ctx_row, row(g_mix[0]), w_in_b[:, :S5_WIDTH])

    w_state, w_y, p_re, p_im = _s5_operators(
        s5_a_re[0], s5_a_im[0], s5_log_step[0], s5_b_re[0], s5_b_im[0], s5_c_re[0], s5_c_im[0],
        s5_d[0])

    u_pairs = _to_pairs(u)
    s_main = _state_components(_s5_state(u_pairs, w_state), bsz)
    s_ctx = _state_components(_s5_state(_to_pairs(u_ctx), w_state), bsz)

    fwd = jnp.concatenate([s_ctx[0:2], s_main[0:2]], axis=1)
    bwd = jnp.concatenate([s_ctx[2:4, ::-1], s_main[2:4, ::-1]], axis=1)
    z = jnp.concatenate([fwd, bwd], axis=2)
    h_re, h_im = _s5_scan(z[0], z[1], p_re, p_im)

    n_ctx = ctx_len // S5_CHUNK
    s_in = jnp.stack([h_re[n_ctx:, :bsz], h_im[n_ctx:, :bsz],
                      h_re[n_ctx:, bsz:][::-1], h_im[n_ctx:, bsz:][::-1]])
    chunks = length // S5_CHUNK
    s_in = s_in.reshape(4, chunks, bsz, S5_PAIRS, 2 * S5_STATE).transpose(3, 2, 1, 0, 4)
    s_in = s_in.reshape(S5_PAIRS, bsz * chunks, PAIR_W).astype(BF16)

    y = _from_pairs(_s5_out(u_pairs, s_in, w_y), bsz, length)

    x1 = _mix(y, x, ga, mb, mod3, s5_w_glu[0].astype(BF16), row(s5_b_glu[0]),
              w_proj_a[0].astype(BF16), w_out[0].astype(BF16))

    return _ffn(x1, mod3, row(g_ffn[0]), w_up[0].astype(BF16), conv_w[0].reshape(9, -1),
                row(conv_b[0]), w_down[0].astype(BF16), row(g_final))
```

```python
import math

import jax
import jax.numpy as jnp
from jax import lax
from jax.experimental import pallas as pl
from jax.experimental.pallas import tpu as pltpu

F32 = jnp.float32
BF16 = jnp.bfloat16

D_MODEL = 1024
GRID_W = 64
S5_WIDTH = 512
S5_GROUP = 16
S5_GROUPS = S5_WIDTH // S5_GROUP
S5_STATE = 64
SGU_WIDTH = 512
SGU_GROUPS = 8
SGU_GROUP_DIM = SGU_WIDTH // SGU_GROUPS
CHUNK = 128
FFN_HIDDEN = 2816
N_MOD = 6
EPS = 1e-6

LANES = 128
SUBLANES = 8
V7X_VMEM_LIMIT = 56 * 1024 * 1024

S5_CHUNK = 16
S5_PAIRS = S5_GROUPS // 2
PAIR_W = 2 * S5_CHUNK * S5_GROUP
S5_SLABS = S5_WIDTH // LANES
PAIRS_PER_SLAB = S5_PAIRS // S5_SLABS

TOK_TILE = 512
FFN_CHUNK = 256


def _cparams(sem):
    return pltpu.CompilerParams(dimension_semantics=sem, vmem_limit_bytes=V7X_VMEM_LIMIT)


def _const_spec(shape):
    zeros = (0,) * len(shape)
    return pl.BlockSpec(shape, lambda *_: zeros, pipeline_mode=pl.Buffered(1))


def _gelu_tanh(x):
    c = math.sqrt(2.0 / math.pi)
    return 0.5 * x * (1.0 + jnp.tanh(c * (x + 0.044715 * (x * x * x))))


def _modulated_rms(x, g, shift, scale):
    ms = jnp.mean(x * x, axis=-1, keepdims=True)
    return (x * lax.rsqrt(ms + EPS)) * (g * (1.0 + scale)) + shift


def _bdot(a, b):
    return jnp.dot(a, b, preferred_element_type=F32)


def _ada_kernel(c_ref, w_ref, b_ref, o_ref):
    c = c_ref[...]
    cs = c * jax.nn.sigmoid(c)
    o_ref[...] = jnp.dot(cs, w_ref[...], preferred_element_type=F32,
                         precision=lax.Precision.HIGHEST) + b_ref[...]


def _ada_rows(c_rows, w_ada, b_ada):
    n = w_ada.shape[1]
    bn = 1536
    return pl.pallas_call(
        _ada_kernel,
        grid=(n // bn,),
        in_specs=[_const_spec(c_rows.shape),
                  pl.BlockSpec((D_MODEL, bn), lambda j: (0, j)),
                  pl.BlockSpec((1, bn), lambda j: (0, j))],
        out_specs=pl.BlockSpec((c_rows.shape[0], bn), lambda j: (0, j)),
        out_shape=jax.ShapeDtypeStruct((c_rows.shape[0], n), F32),
        compiler_params=_cparams(("arbitrary",)),
        name="ada",
    )(c_rows, w_ada, b_ada.reshape(1, n))


def _swap_slab_and_piece(slabs):
    lane = lax.broadcasted_iota(jnp.int32, (1, LANES), 1)
    out = list(slabs)
    for k in range(3):
        width = S5_GROUP << k
        keep = ((lane >> (4 + k)) & 1) == 0
        nxt = list(out)
        for a0 in range(len(out)):
            if a0 & (1 << k):
                continue
            a1 = a0 | (1 << k)
            nxt[a0] = jnp.where(keep, out[a0], pltpu.roll(out[a1], width, 1))
            nxt[a1] = jnp.where(keep, pltpu.roll(out[a0], LANES - width, 1), out[a1])
        out = nxt
    return out


def _pair_lane_block(slab_idx):
    t_hi, g = divmod(slab_idx, SUBLANES)
    pp, gg = divmod(g, 2)
    return pp, gg * 2 + t_hi


def _store_pair_tiles(u, u_scr, up_ref):
    cpt = u.shape[0] // S5_CHUNK
    for j in range(S5_SLABS):
        u_scr[j] = u[:, j * LANES:(j + 1) * LANES]
    for j in range(S5_SLABS):
        slabs = _swap_slab_and_piece(
            [u_scr[j, pl.ds(t, cpt, stride=S5_CHUNK), :] for t in range(S5_CHUNK)])
        for a in range(S5_CHUNK):
            pp, blk = _pair_lane_block(a)
            up_ref[j * PAIRS_PER_SLAB + pp, :, blk * LANES:(blk + 1) * LANES] = slabs[a].astype(BF16)


def _inproj_kernel(x_ref, mod_ref, g_ref, w_ref, bgate_ref, lng_ref, lnb_ref, wsp_ref, bsp_ref,
                   wpb_ref, u_ref, ga_ref, mb_ref, u_scr):
    m = mod_ref[0]
    h = _modulated_rms(x_ref[0], g_ref[...], m[0:1], m[1:2]).astype(BF16)
    tt = h.shape[0]

    _store_pair_tiles(_bdot(h, w_ref[:, 0:S5_WIDTH]), u_scr, u_ref)

    z0 = S5_WIDTH
    zu = _gelu_tanh(_bdot(h, w_ref[:, z0:z0 + SGU_WIDTH]))
    zv = _gelu_tanh(_bdot(h, w_ref[:, z0 + SGU_WIDTH:z0 + 2 * SGU_WIDTH]))
    mu = jnp.mean(zv, axis=-1, keepdims=True)
    zc = zv - mu
    var = jnp.mean(zc * zc, axis=-1, keepdims=True)
    v = ((zc * lax.rsqrt(var + EPS)) * lng_ref[...] + lnb_ref[...]).astype(BF16)

    lane = lax.broadcasted_iota(jnp.int32, (CHUNK, LANES), 1)
    zero = jnp.zeros((CHUNK, LANES), BF16)
    rows = []
    for c in range(tt // CHUNK):
        tiles = []
        for j in range(SGU_WIDTH // LANES):
            vj = v[c * CHUNK:(c + 1) * CHUNK, j * LANES:(j + 1) * LANES]
            rhs = jnp.concatenate([jnp.where(lane < SGU_GROUP_DIM, vj, zero),
                                   jnp.where(lane >= SGU_GROUP_DIM, vj, zero)], axis=0)
            tiles.append(_bdot(wsp_ref[j], rhs))
        rows.append(jnp.concatenate(tiles, axis=1) + bsp_ref[...])
    s = jnp.concatenate(rows, axis=0)
    yb = (zu * s).astype(BF16)
    pb = _bdot(yb, wpb_ref[...])

    g0 = S5_WIDTH + 2 * SGU_WIDTH
    ga = jax.nn.sigmoid(_bdot(h, w_ref[:, g0:g0 + D_MODEL]) + bgate_ref[:, 0:D_MODEL])
    gb = jax.nn.sigmoid(_bdot(h, w_ref[:, g0 + D_MODEL:g0 + 2 * D_MODEL])
                        + bgate_ref[:, D_MODEL:2 * D_MODEL])
    ga_ref[0] = ga.astype(BF16)
    mb_ref[0] = (gb * pb).astype(BF16)


def _inproj(x, mod3, g_mix, w_in, b_gate, ln_g, ln_b, wsp_pair, bsp_rows, w_proj_b):
    bsz, length, _ = x.shape
    tt = TOK_TILE
    nt = length // tt
    tok = lambda w: pl.BlockSpec((1, tt, w), lambda b, i: (b, i, 0))
    return pl.pallas_call(
        _inproj_kernel,
        grid=(bsz, nt),
        in_specs=[tok(D_MODEL),
                  pl.BlockSpec((1, N_MOD, D_MODEL), lambda b, i: (b, 0, 0)),
                  _const_spec(g_mix.shape), _const_spec(w_in.shape), _const_spec(b_gate.shape),
                  _const_spec(ln_g.shape), _const_spec(ln_b.shape), _const_spec(wsp_pair.shape),
                  _const_spec(bsp_rows.shape), _const_spec(w_proj_b.shape)],
        out_specs=[pl.BlockSpec((S5_PAIRS, tt // S5_CHUNK, PAIR_W), lambda b, i: (0, b * nt + i, 0)),
                   tok(D_MODEL), tok(D_MODEL)],
        out_shape=[jax.ShapeDtypeStruct((S5_PAIRS, bsz * length // S5_CHUNK, PAIR_W), BF16),
                   jax.ShapeDtypeStruct((bsz, length, D_MODEL), BF16),
                   jax.ShapeDtypeStruct((bsz, length, D_MODEL), BF16)],
        scratch_shapes=[pltpu.VMEM((S5_SLABS, tt, LANES), F32)],
        compiler_params=_cparams(("parallel", "parallel")),
        name="inproj",
    )(x, mod3, g_mix, w_in, b_gate, ln_g, ln_b, wsp_pair, bsp_rows, w_proj_b)


def _ctxproj_kernel(x_ref, mod_ref, g_ref, w_ref, u_ref, u_scr):
    m = mod_ref[0]
    h = _modulated_rms(x_ref[0], g_ref[...], m[0:1], m[1:2]).astype(BF16)
    _store_pair_tiles(_bdot(h, w_ref[...]), u_scr, u_ref)


def _ctxproj(ctx, mod3, ctx_row, g_mix, w_in_u):
    bsz, length, _ = ctx.shape
    return pl.pallas_call(
        _ctxproj_kernel,
        grid=(bsz,),
        in_specs=[pl.BlockSpec((1, length, D_MODEL), lambda b: (b, 0, 0)),
                  pl.BlockSpec((1, N_MOD, D_MODEL), lambda b: (ctx_row, 0, 0)),
                  _const_spec(g_mix.shape), _const_spec(w_in_u.shape)],
        out_specs=pl.BlockSpec((S5_PAIRS, length // S5_CHUNK, PAIR_W), lambda b: (0, b, 0)),
        out_shape=jax.ShapeDtypeStruct((S5_PAIRS, bsz * length // S5_CHUNK, PAIR_W), BF16),
        scratch_shapes=[pltpu.VMEM((S5_SLABS, length, LANES), F32)],
        compiler_params=_cparams(("parallel",)),
        name="ctxproj",
    )(ctx, mod3, g_mix, w_in_u)


def _s5_operators(a_re, a_im, log_step, b_re, b_im, c_re, c_im, d_skip):
    hp = lax.Precision.HIGHEST
    n_tok = S5_CHUNK
    dt = jnp.exp(log_step)[..., None]
    mag = jnp.exp(a_re * dt)
    lam_re = mag * jnp.cos(a_im * dt)
    lam_im = mag * jnp.sin(a_im * dt)
    p = lam_re - 1.0
    q = lam_im
    den = a_re * a_re + a_im * a_im
    k_re = ((p * a_re + q * a_im) / den)[..., None]
    k_im = ((q * a_re - p * a_im) / den)[..., None]
    bb_re = k_re * b_re - k_im * b_im
    bb_im = k_re * b_im + k_im * b_re

    pw_re = [jnp.ones_like(lam_re)]
    pw_im = [jnp.zeros_like(lam_im)]
    for _ in range(n_tok):
        r, i = pw_re[-1], pw_im[-1]
        pw_re.append(r * lam_re - i * lam_im)
        pw_im.append(r * lam_im + i * lam_re)
    pw_re = jnp.stack(pw_re)
    pw_im = jnp.stack(pw_im)

    cb_re = jnp.einsum('dgon,dgnh->dgnho', c_re, bb_re, precision=hp) \
        - jnp.einsum('dgon,dgnh->dgnho', c_im, bb_im, precision=hp)
    cb_im = jnp.einsum('dgon,dgnh->dgnho', c_re, bb_im, precision=hp) \
        + jnp.einsum('dgon,dgnh->dgnho', c_im, bb_re, precision=hp)
    kt = jnp.einsum('tdgn,dgnho->tdgho', pw_re[:n_tok], cb_re, precision=hp) \
        - jnp.einsum('tdgn,dgnho->tdgho', pw_im[:n_tok], cb_im, precision=hp)

    ii = jnp.arange(n_tok)[:, None]
    jj = jnp.arange(n_tok)[None, :]
    zeros_k = jnp.zeros_like(kt[:1, 0])
    kf = jnp.concatenate([jnp.broadcast_to(zeros_k, (n_tok - 1,) + zeros_k.shape[1:]), kt[:, 0]], 0)
    kb = jnp.concatenate([jnp.broadcast_to(zeros_k, (n_tok - 1,) + zeros_k.shape[1:]), kt[:, 1]], 0)
    t_f = kf[(jj - ii) + n_tok - 1]
    t_b = kb[(ii - jj) + n_tok - 1]
    eye_tok = jnp.eye(n_tok, dtype=F32)[:, :, None, None, None]
    eye_ch = jnp.eye(S5_GROUP, dtype=F32)[None, None, None]
    skip = eye_tok * eye_ch * d_skip.reshape(S5_GROUPS, S5_GROUP)[None, None, :, :, None]
    w_intra = (t_f + t_b + skip).transpose(2, 0, 3, 1, 4).reshape(
        S5_GROUPS, n_tok * S5_GROUP, n_tok * S5_GROUP)

    def state_cols(pr, pi, d):
        re = pr[:, :, :, None] * bb_re[d][None] - pi[:, :, :, None] * bb_im[d][None]
        im = pr[:, :, :, None] * bb_im[d][None] + pi[:, :, :, None] * bb_re[d][None]
        to = lambda t: t.transpose(1, 0, 3, 2).reshape(S5_GROUPS, n_tok * S5_GROUP, S5_STATE)
        return to(re), to(im)
    f_re, f_im = state_cols(pw_re[n_tok - 1::-1, 0], pw_im[n_tok - 1::-1, 0], 0)
    b_re_, b_im_ = state_cols(pw_re[:n_tok, 1], pw_im[:n_tok, 1], 1)
    w_st = jnp.stack([f_re, f_im, b_re_, b_im_], axis=2)

    def read_rows(pr, pi, d):
        re = c_re[d][None] * pr[:, :, None, :] - c_im[d][None] * pi[:, :, None, :]
        im = c_re[d][None] * pi[:, :, None, :] + c_im[d][None] * pr[:, :, None, :]
        to = lambda t: t.transpose(1, 3, 0, 2).reshape(S5_GROUPS, S5_STATE, n_tok * S5_GROUP)
        return to(re), to(-im)
    rf_re, rf_im = read_rows(pw_re[1:n_tok + 1, 0], pw_im[1:n_tok + 1, 0], 0)
    rb_re, rb_im = read_rows(pw_re[n_tok:0:-1, 1], pw_im[n_tok:0:-1, 1], 1)
    w_rd = jnp.stack([rf_re, rf_im, rb_re, rb_im], axis=1)

    eye2 = jnp.eye(2, dtype=F32)
    blk = n_tok * S5_GROUP
    w_intra_p = jnp.einsum('pgrc,gk->pgrkc', w_intra.reshape(S5_PAIRS, 2, blk, blk), eye2)
    w_intra_p = w_intra_p.reshape(S5_PAIRS, PAIR_W, PAIR_W)
    w_state_p = jnp.einsum('pgrcn,gk->pgrckn', w_st.reshape(S5_PAIRS, 2, blk, 4, S5_STATE), eye2)
    w_state_p = w_state_p.reshape(S5_PAIRS, PAIR_W, 4 * 2 * S5_STATE)
    w_read_p = jnp.einsum('pgcnr,gk->pcgnkr', w_rd.reshape(S5_PAIRS, 2, 4, S5_STATE, blk), eye2)
    w_read_p = w_read_p.reshape(S5_PAIRS, 4 * 2 * S5_STATE, PAIR_W)
    w_y = jnp.concatenate([w_intra_p, w_read_p], axis=1)

    lp_re = [jnp.ones_like(pw_re[0]), pw_re[n_tok]]
    lp_im = [jnp.zeros_like(pw_im[0]), pw_im[n_tok]]
    for _ in range(SUBLANES - 1):
        r, i = lp_re[-1], lp_im[-1]
        lp_re.append(r * pw_re[n_tok] - i * pw_im[n_tok])
        lp_im.append(r * pw_im[n_tok] + i * pw_re[n_tok])
    lanes = lambda t: t.reshape(SUBLANES + 1, 2, S5_PAIRS, 2 * S5_STATE)
    lp_re, lp_im = lanes(jnp.stack(lp_re)), lanes(jnp.stack(lp_im))
    r8 = jnp.arange(SUBLANES)

    def step_rows(lp, s):
        f = jnp.where((r8 >= s)[None, :, None], lp[s, 0][:, None, :], 0.0)
        b = jnp.where((r8 + s < SUBLANES)[None, :, None], lp[s, 1][:, None, :], 0.0)
        return jnp.concatenate([f, b], axis=-1)

    def carry_rows(lp):
        f = lp[r8 + 1, 0].transpose(1, 0, 2)
        b = lp[SUBLANES - r8, 1].transpose(1, 0, 2)
        return jnp.concatenate([f, b], axis=-1)

    kinds = []
    for s in (1, 2, 4):
        kinds += [step_rows(lp_re, s), step_rows(lp_im, s)]
    kinds += [carry_rows(lp_re), carry_rows(lp_im)]
    coef = jnp.stack(kinds, axis=1)
    return w_state_p.astype(BF16), w_y.astype(BF16), coef


def _s5_kernel(u_ref, uc_ref, wst_ref, wy_ref, coef_ref, y_ref, s_scr, sin_scr):
    n_main = u_ref.shape[1]
    n_ctx = uc_ref.shape[1]
    n_rows = n_ctx + n_main + n_ctx

    for pp in range(PAIRS_PER_SLAB):
        s_ctx = _bdot(uc_ref[pp], wst_ref[pp])
        s_scr[pp, 0:n_ctx] = s_ctx
        s_scr[pp, n_ctx:n_ctx + n_main] = _bdot(u_ref[pp], wst_ref[pp])
        s_scr[pp, n_ctx + n_main:] = s_ctx

    row = lax.broadcasted_iota(jnp.int32, (SUBLANES, LANES), 0)
    n_groups = n_rows // SUBLANES
    n_iter = (n_ctx + n_main) // SUBLANES

    def group(pp, base, c_re, c_im, lane0, forward):
        rows = pl.ds(base, SUBLANES)
        re_l = slice(lane0, lane0 + LANES)
        im_l = slice(lane0 + LANES, lane0 + 2 * LANES)
        cf = slice(0, LANES) if forward else slice(LANES, 2 * LANES)
        xr = s_scr[pp, rows, re_l]
        xi = s_scr[pp, rows, im_l]
        for k, sh in enumerate((1, 2, 4)):
            ar = coef_ref[pp, 2 * k, :, cf]
            ai = coef_ref[pp, 2 * k + 1, :, cf]
            amt = sh if forward else SUBLANES - sh
            sr = pltpu.roll(xr, amt, 0)
            si = pltpu.roll(xi, amt, 0)
            xr, xi = xr + (ar * sr - ai * si), xi + (ar * si + ai * sr)
        qr = coef_ref[pp, 6, :, cf]
        qi = coef_ref[pp, 7, :, cf]
        hr = xr + (qr * c_re - qi * c_im)
        hi = xi + (qr * c_im + qi * c_re)
        edge = 0 if forward else SUBLANES - 1
        amt = 1 if forward else SUBLANES - 1
        sin_scr[pp, rows, re_l] = jnp.where(row == edge, c_re, pltpu.roll(hr, amt, 0))
        sin_scr[pp, rows, im_l] = jnp.where(row == edge, c_im, pltpu.roll(hi, amt, 0))
        last = SUBLANES - 1 if forward else 0
        return (jnp.broadcast_to(hr[last:last + 1], (SUBLANES, LANES)),
                jnp.broadcast_to(hi[last:last + 1], (SUBLANES, LANES)))

    def body(g, carry):
        new = []
        for pp in range(PAIRS_PER_SLAB):
            f_re, f_im, b_re, b_im = carry[4 * pp:4 * pp + 4]
            f_base = pl.multiple_of(g * SUBLANES, SUBLANES)
            b_base = pl.multiple_of((n_groups - 1 - g) * SUBLANES, SUBLANES)
            f_re, f_im = group(pp, f_base, f_re, f_im, 0, True)
            b_re, b_im = group(pp, b_base, b_re, b_im, 2 * LANES, False)
            new += [f_re, f_im, b_re, b_im]
        return tuple(new)

    zero = jnp.zeros((SUBLANES, LANES), F32)
    lax.fori_loop(0, n_iter, body, (zero,) * (4 * PAIRS_PER_SLAB))

    main_rows = slice(n_ctx, n_ctx + n_main)
    for pp in range(PAIRS_PER_SLAB):
        y_ref[pp] = (_bdot(u_ref[pp], wy_ref[pp, 0:PAIR_W])
                     + _bdot(sin_scr[pp, main_rows, :].astype(BF16), wy_ref[pp, PAIR_W:2 * PAIR_W]))


def _s5(u_pairs, uc_pairs, bsz, w_state, w_y, coef):
    n_main = u_pairs.shape[1] // bsz
    n_ctx = uc_pairs.shape[1] // bsz
    n_rows = n_main + 2 * n_ctx
    pps = PAIRS_PER_SLAB
    rows = lambda n: pl.BlockSpec((pps, n, PAIR_W), lambda j, b: (j, b, 0))
    return pl.pallas_call(
        _s5_kernel,
        grid=(S5_SLABS, bsz),
        in_specs=[rows(n_main), rows(n_ctx),
                  pl.BlockSpec((pps, PAIR_W, PAIR_W), lambda j, b: (j, 0, 0)),
                  pl.BlockSpec((pps, 2 * PAIR_W, PAIR_W), lambda j, b: (j, 0, 0)),
                  pl.BlockSpec((pps, 8, SUBLANES, 2 * LANES), lambda j, b: (j, 0, 0, 0))],
        out_specs=rows(n_main),
        out_shape=jax.ShapeDtypeStruct(u_pairs.shape, F32),
        scratch_shapes=[pltpu.VMEM((pps, n_rows, PAIR_W), F32),
                        pltpu.VMEM((pps, n_rows, PAIR_W), F32)],
        compiler_params=_cparams(("arbitrary", "arbitrary")),
        name="s5",
    )(u_pairs, uc_pairs, w_state, w_y, coef)


def _mix_kernel(y_ref, x_ref, ga_ref, mb_ref, mod_ref, wglu_ref, bglu_ref, wpa_ref, wout_ref, o_ref,
                y_scr):
    m = mod_ref[0]
    cpt = y_ref.shape[1]
    for j in range(S5_SLABS):
        slabs = []
        for a in range(S5_CHUNK):
            pp, blk = _pair_lane_block(a)
            slabs.append(y_ref[j * PAIRS_PER_SLAB + pp, :, blk * LANES:(blk + 1) * LANES])
        slabs = _swap_slab_and_piece(slabs)
        for t in range(S5_CHUNK):
            y_scr[j, pl.ds(t, cpt, stride=S5_CHUNK), :] = slabs[t]
    ya = _gelu_tanh(jnp.concatenate([y_scr[j] for j in range(S5_SLABS)], axis=1))
    glu = ya * jax.nn.sigmoid(_bdot(ya.astype(BF16), wglu_ref[...]) + bglu_ref[...])
    pa = _bdot(glu.astype(BF16), wpa_ref[...])
    merged = (ga_ref[0].astype(F32) * pa + mb_ref[0].astype(F32)).astype(BF16)
    o_ref[0] = x_ref[0] + m[2:3] * _bdot(merged, wout_ref[...])


def _mix(y_pairs, x, ga, mb, mod3, w_glu, b_glu, w_proj_a, w_out):
    bsz, length, _ = x.shape
    tt = TOK_TILE
    nt = length // tt
    tok = lambda w: pl.BlockSpec((1, tt, w), lambda b, i: (b, i, 0))
    return pl.pallas_call(
        _mix_kernel,
        grid=(bsz, nt),
        in_specs=[pl.BlockSpec((S5_PAIRS, tt // S5_CHUNK, PAIR_W), lambda b, i: (0, b * nt + i, 0)),
                  tok(D_MODEL), tok(D_MODEL), tok(D_MODEL),
                  pl.BlockSpec((1, N_MOD, D_MODEL), lambda b, i: (b, 0, 0)),
                  _const_spec(w_glu.shape), _const_spec(b_glu.shape),
                  _const_spec(w_proj_a.shape), _const_spec(w_out.shape)],
        out_specs=tok(D_MODEL),
        out_shape=jax.ShapeDtypeStruct(x.shape, F32),
        scratch_shapes=[pltpu.VMEM((S5_SLABS, tt, LANES), F32)],
        compiler_params=_cparams(("parallel", "parallel")),
        name="mix",
    )(y_pairs, x, ga, mb, mod3, w_glu, b_glu, w_proj_a, w_out)


def _grid_conv(u, w9, bias, tt):
    n = u.shape[0]
    col = lax.broadcasted_iota(jnp.int32, (n, 1), 0) % GRID_W
    left = jnp.where(col == 0, 0.0, pltpu.roll(u, 1, 0))
    right = jnp.where(col == GRID_W - 1, 0.0, pltpu.roll(u, n - 1, 0))
    acc = bias
    for dr in range(3):
        lo = dr * GRID_W
        acc = acc + (left[lo:lo + tt] * w9[3 * dr:3 * dr + 1]
                     + u[lo:lo + tt] * w9[3 * dr + 1:3 * dr + 2]
                     + right[lo:lo + tt] * w9[3 * dr + 2:3 * dr + 3])
    return acc


def _ffn_kernel(top_ref, x_ref, bot_ref, mod_ref, g_ref, wup_ref, cw_ref, cb_ref, wdn_ref, gf_ref,
                o_ref, h_scr, act_scr):
    i = pl.program_id(1)
    n_i = pl.num_programs(1)
    m = mod_ref[0]
    g = g_ref[...]
    tt = x_ref.shape[1]
    x = x_ref[0]

    def norm(t):
        return _modulated_rms(t, g, m[3:4], m[4:5])

    h_scr[0:GRID_W] = jnp.where(i > 0, norm(top_ref[0]), 0.0).astype(BF16)
    h_scr[GRID_W:GRID_W + tt] = norm(x).astype(BF16)
    h_scr[GRID_W + tt:] = jnp.where(i < n_i - 1, norm(bot_ref[0]), 0.0).astype(BF16)
    h = h_scr[...]

    fc = FFN_CHUNK
    for j in range(FFN_HIDDEN // fc):
        gs = slice(j * fc, (j + 1) * fc)
        vs = slice(FFN_HIDDEN + j * fc, FFN_HIDDEN + (j + 1) * fc)
        cg = _grid_conv(_bdot(h, wup_ref[:, gs]), cw_ref[:, gs], cb_ref[:, gs], tt)
        cv = _grid_conv(_bdot(h, wup_ref[:, vs]), cw_ref[:, vs], cb_ref[:, vs], tt)
        act_scr[:, gs] = ((cg * jax.nn.sigmoid(cg)) * cv).astype(BF16)

    x2 = x + m[5:6] * _bdot(act_scr[...], wdn_ref[...])
    ms = jnp.mean(x2 * x2, axis=-1, keepdims=True)
    o_ref[0] = (x2 * lax.rsqrt(ms + EPS)) * gf_ref[...]


def _ffn(x1, mod3, g_ffn, w_up, conv_w9, conv_b, w_down, g_final):
    bsz, length, _ = x1.shape
    tt = TOK_TILE
    rows_per_tile = tt // GRID_W
    n_rows = length // GRID_W
    halo = lambda fn: pl.BlockSpec((1, GRID_W, D_MODEL), fn)
    return pl.pallas_call(
        _ffn_kernel,
        grid=(bsz, length // tt),
        in_specs=[halo(lambda b, i: (b, jnp.maximum(i * rows_per_tile - 1, 0), 0)),
                  pl.BlockSpec((1, tt, D_MODEL), lambda b, i: (b, i, 0)),
                  halo(lambda b, i: (b, jnp.minimum((i + 1) * rows_per_tile, n_rows - 1), 0)),
                  pl.BlockSpec((1, N_MOD, D_MODEL), lambda b, i: (b, 0, 0)),
                  _const_spec(g_ffn.shape), _const_spec(w_up.shape), _const_spec(conv_w9.shape),
                  _const_spec(conv_b.shape), _const_spec(w_down.shape), _const_spec(g_final.shape)],
        out_specs=pl.BlockSpec((1, tt, D_MODEL), lambda b, i: (b, i, 0)),
        out_shape=jax.ShapeDtypeStruct(x1.shape, F32),
        scratch_shapes=[pltpu.VMEM((tt + 2 * GRID_W, D_MODEL), BF16),
                        pltpu.VMEM((tt, FFN_HIDDEN), BF16)],
        compiler_params=_cparams(("parallel", "parallel")),
        name="ffn",
    )(x1, x1, x1, mod3, g_ffn, w_up, conv_w9, conv_b, w_down, g_final)


def kernel(x, c, ctx, c_ctx, w_ada, b_ada, g_mix, w_in, s5_a_re, s5_a_im, s5_log_step, s5_b_re, s5_b_im, s5_c_re, s5_c_im, s5_d, s5_w_glu, s5_b_glu, sgu_ln_g, sgu_ln_b, sgu_w, sgu_b, w_proj_a, w_proj_b, b_gate, w_out, g_ffn, w_up, conv_w, conv_b, w_down, g_final):
    bsz = x.shape[0]
    row = lambda t: t.reshape(1, -1)

    ctx_row = bsz
    c_rows = jnp.zeros((SUBLANES, D_MODEL), F32).at[:bsz].set(c).at[ctx_row].set(c_ctx)
    mod3 = _ada_rows(c_rows, w_ada[0], b_ada[0]).reshape(SUBLANES, N_MOD, D_MODEL)

    w_in_b = w_in[0].astype(BF16)
    wsp = sgu_w[0].astype(BF16)
    wsp_pair = jnp.concatenate([wsp[0::2], wsp[1::2]], axis=-1)
    bsp_rows = jnp.repeat(sgu_b[0].T, SGU_GROUP_DIM, axis=1)

    u_pairs, ga, mb = _inproj(x, mod3, row(g_mix[0]), w_in_b, row(b_gate[0]), row(sgu_ln_g[0]),
                              row(sgu_ln_b[0]), wsp_pair, bsp_rows, w_proj_b[0].astype(BF16))
    uc_pairs = _ctxproj(ctx, mod3, ctx_row, row(g_mix[0]), w_in_b[:, :S5_WIDTH])

    w_state, w_y, coef = _s5_operators(
        s5_a_re[0], s5_a_im[0], s5_log_step[0], s5_b_re[0], s5_b_im[0], s5_c_re[0], s5_c_im[0],
        s5_d[0])
    y_pairs = _s5(u_pairs, uc_pairs, bsz, w_state, w_y, coef)

    x1 = _mix(y_pairs, x, ga, mb, mod3, s5_w_glu[0].astype(BF16), row(s5_b_glu[0]),
              w_proj_a[0].astype(BF16), w_out[0].astype(BF16))

    return _ffn(x1, mod3, row(g_ffn[0]), w_up[0].astype(BF16), conv_w[0].reshape(9, -1),
                row(conv_b[0]), w_down[0].astype(BF16), row(g_final))
```

```python
import math

import jax
import jax.numpy as jnp
from jax import lax
from jax.experimental import pallas as pl
from jax.experimental.pallas import tpu as pltpu

F32 = jnp.float32
BF16 = jnp.bfloat16

D_MODEL = 1024
GRID_W = 64
S5_WIDTH = 512
S5_GROUP = 16
S5_GROUPS = S5_WIDTH // S5_GROUP
S5_STATE = 64
SGU_WIDTH = 512
SGU_GROUPS = 8
SGU_GROUP_DIM = SGU_WIDTH // SGU_GROUPS
CHUNK = 128
FFN_HIDDEN = 2816
N_MOD = 6
EPS = 1e-6

LANES = 128
SUBLANES = 8
V7X_VMEM_LIMIT = 56 * 1024 * 1024

S5_CHUNK = 16
S5_PAIRS = S5_GROUPS // 2
PAIR_W = 2 * S5_CHUNK * S5_GROUP
S5_SLABS = S5_WIDTH // LANES
PAIRS_PER_SLAB = S5_PAIRS // S5_SLABS

TOK_TILE = 512
FFN_CHUNK = 256
FFN_DOWN_GROUP = 11


def _cparams(sem):
    return pltpu.CompilerParams(dimension_semantics=sem, vmem_limit_bytes=V7X_VMEM_LIMIT)


def _const_spec(shape):
    zeros = (0,) * len(shape)
    return pl.BlockSpec(shape, lambda *_: zeros, pipeline_mode=pl.Buffered(1))


def _gelu_tanh(x):
    c = math.sqrt(2.0 / math.pi)
    return 0.5 * x * (1.0 + jnp.tanh(c * (x + 0.044715 * (x * x * x))))


def _modulated_rms(x, g, shift, scale):
    ms = jnp.mean(x * x, axis=-1, keepdims=True)
    return (x * lax.rsqrt(ms + EPS)) * (g * (1.0 + scale)) + shift


def _bdot(a, b):
    return jnp.dot(a, b, preferred_element_type=F32)


def _ada_kernel(c_ref, w_ref, b_ref, o_ref):
    c = c_ref[...]
    cs = c * jax.nn.sigmoid(c)
    o_ref[...] = jnp.dot(cs, w_ref[...], preferred_element_type=F32,
                         precision=lax.Precision.HIGHEST) + b_ref[...]


def _ada_rows(c_rows, w_ada, b_ada):
    n = w_ada.shape[1]
    bn = 1536
    return pl.pallas_call(
        _ada_kernel,
        grid=(n // bn,),
        in_specs=[_const_spec(c_rows.shape),
                  pl.BlockSpec((D_MODEL, bn), lambda j: (0, j)),
                  pl.BlockSpec((1, bn), lambda j: (0, j))],
        out_specs=pl.BlockSpec((c_rows.shape[0], bn), lambda j: (0, j)),
        out_shape=jax.ShapeDtypeStruct((c_rows.shape[0], n), F32),
        compiler_params=_cparams(("arbitrary",)),
        name="ada",
    )(c_rows, w_ada, b_ada.reshape(1, n))


def _swap_slab_and_piece(slabs):
    lane = lax.broadcasted_iota(jnp.int32, (1, LANES), 1)
    out = list(slabs)
    for k in range(3):
        width = S5_GROUP << k
        keep = ((lane >> (4 + k)) & 1) == 0
        nxt = list(out)
        for a0 in range(len(out)):
            if a0 & (1 << k):
                continue
            a1 = a0 | (1 << k)
            nxt[a0] = jnp.where(keep, out[a0], pltpu.roll(out[a1], width, 1))
            nxt[a1] = jnp.where(keep, pltpu.roll(out[a0], LANES - width, 1), out[a1])
        out = nxt
    return out


def _pair_lane_block(slab_idx):
    t_hi, g = divmod(slab_idx, SUBLANES)
    pp, gg = divmod(g, 2)
    return pp, gg * 2 + t_hi


def _store_pair_tiles(u, u_scr, up_ref):
    cpt = u.shape[0] // S5_CHUNK
    for j in range(S5_SLABS):
        u_scr[j] = u[:, j * LANES:(j + 1) * LANES]
    for j in range(S5_SLABS):
        slabs = _swap_slab_and_piece(
            [u_scr[j, pl.ds(t, cpt, stride=S5_CHUNK), :] for t in range(S5_CHUNK)])
        for a in range(S5_CHUNK):
            pp, blk = _pair_lane_block(a)
            up_ref[j * PAIRS_PER_SLAB + pp, :, blk * LANES:(blk + 1) * LANES] = slabs[a].astype(BF16)


def _inproj_kernel(x_ref, mod_ref, g_ref, w_ref, bgate_ref, lng_ref, lnb_ref, wsp_ref, bsp_ref,
                   wpb_ref, u_ref, ga_ref, mb_ref, u_scr):
    m = mod_ref[0]
    h = _modulated_rms(x_ref[0], g_ref[...], m[0:1], m[1:2]).astype(BF16)
    tt = h.shape[0]

    _store_pair_tiles(_bdot(h, w_ref[:, 0:S5_WIDTH]), u_scr, u_ref)

    z0 = S5_WIDTH
    zu = _gelu_tanh(_bdot(h, w_ref[:, z0:z0 + SGU_WIDTH]))
    zv = _gelu_tanh(_bdot(h, w_ref[:, z0 + SGU_WIDTH:z0 + 2 * SGU_WIDTH]))
    mu = jnp.mean(zv, axis=-1, keepdims=True)
    zc = zv - mu
    var = jnp.mean(zc * zc, axis=-1, keepdims=True)
    v = ((zc * lax.rsqrt(var + EPS)) * lng_ref[...] + lnb_ref[...]).astype(BF16)

    lane = lax.broadcasted_iota(jnp.int32, (CHUNK, LANES), 1)
    zero = jnp.zeros((CHUNK, LANES), BF16)
    rows = []
    for c in range(tt // CHUNK):
        tiles = []
        for j in range(SGU_WIDTH // LANES):
            vj = v[c * CHUNK:(c + 1) * CHUNK, j * LANES:(j + 1) * LANES]
            rhs = jnp.concatenate([jnp.where(lane < SGU_GROUP_DIM, vj, zero),
                                   jnp.where(lane >= SGU_GROUP_DIM, vj, zero)], axis=0)
            tiles.append(_bdot(wsp_ref[j], rhs))
        rows.append(jnp.concatenate(tiles, axis=1) + bsp_ref[...])
    s = jnp.concatenate(rows, axis=0)
    yb = (zu * s).astype(BF16)
    pb = _bdot(yb, wpb_ref[...])

    g0 = S5_WIDTH + 2 * SGU_WIDTH
    ga = jax.nn.sigmoid(_bdot(h, w_ref[:, g0:g0 + D_MODEL]) + bgate_ref[:, 0:D_MODEL])
    gb = jax.nn.sigmoid(_bdot(h, w_ref[:, g0 + D_MODEL:g0 + 2 * D_MODEL])
                        + bgate_ref[:, D_MODEL:2 * D_MODEL])
    ga_ref[0] = ga.astype(BF16)
    mb_ref[0] = (gb * pb).astype(BF16)


def _inproj(x, mod3, g_mix, w_in, b_gate, ln_g, ln_b, wsp_pair, bsp_rows, w_proj_b):
    bsz, length, _ = x.shape
    tt = TOK_TILE
    nt = length // tt
    tok = lambda w: pl.BlockSpec((1, tt, w), lambda b, i: (b, i, 0))
    return pl.pallas_call(
        _inproj_kernel,
        grid=(bsz, nt),
        in_specs=[tok(D_MODEL),
                  pl.BlockSpec((1, N_MOD, D_MODEL), lambda b, i: (b, 0, 0)),
                  _const_spec(g_mix.shape), _const_spec(w_in.shape), _const_spec(b_gate.shape),
                  _const_spec(ln_g.shape), _const_spec(ln_b.shape), _const_spec(wsp_pair.shape),
                  _const_spec(bsp_rows.shape), _const_spec(w_proj_b.shape)],
        out_specs=[pl.BlockSpec((S5_PAIRS, tt // S5_CHUNK, PAIR_W), lambda b, i: (0, b * nt + i, 0)),
                   tok(D_MODEL), tok(D_MODEL)],
        out_shape=[jax.ShapeDtypeStruct((S5_PAIRS, bsz * length // S5_CHUNK, PAIR_W), BF16),
                   jax.ShapeDtypeStruct((bsz, length, D_MODEL), BF16),
                   jax.ShapeDtypeStruct((bsz, length, D_MODEL), BF16)],
        scratch_shapes=[pltpu.VMEM((S5_SLABS, tt, LANES), F32)],
        compiler_params=_cparams(("parallel", "parallel")),
        name="inproj",
    )(x, mod3, g_mix, w_in, b_gate, ln_g, ln_b, wsp_pair, bsp_rows, w_proj_b)


def _ctxproj_kernel(x_ref, mod_ref, g_ref, w_ref, u_ref, u_scr):
    m = mod_ref[0]
    h = _modulated_rms(x_ref[0], g_ref[...], m[0:1], m[1:2]).astype(BF16)
    _store_pair_tiles(_bdot(h, w_ref[...]), u_scr, u_ref)


def _ctxproj(ctx, mod3, ctx_row, g_mix, w_in_u):
    bsz, length, _ = ctx.shape
    return pl.pallas_call(
        _ctxproj_kernel,
        grid=(bsz,),
        in_specs=[pl.BlockSpec((1, length, D_MODEL), lambda b: (b, 0, 0)),
                  pl.BlockSpec((1, N_MOD, D_MODEL), lambda b: (ctx_row, 0, 0)),
                  _const_spec(g_mix.shape), _const_spec(w_in_u.shape)],
        out_specs=pl.BlockSpec((S5_PAIRS, length // S5_CHUNK, PAIR_W), lambda b: (0, b, 0)),
        out_shape=jax.ShapeDtypeStruct((S5_PAIRS, bsz * length // S5_CHUNK, PAIR_W), BF16),
        scratch_shapes=[pltpu.VMEM((S5_SLABS, length, LANES), F32)],
        compiler_params=_cparams(("parallel",)),
        name="ctxproj",
    )(ctx, mod3, g_mix, w_in_u)


def _cmul(ar, ai, br, bi):
    return ar * br - ai * bi, ar * bi + ai * br


def _s5_ops_kernel(a_re_ref, a_im_ref, ls_ref, bt_re_ref, bt_im_ref, c_re_ref, c_im_ref, d_ref,
                   wst_ref, wy_ref, coef_ref):
    hp = lax.Precision.HIGHEST
    n_tok = S5_CHUNK
    blk = n_tok * S5_GROUP
    tok_of_lane = lax.broadcasted_iota(jnp.int32, (1, blk), 1) // S5_GROUP
    row8 = lax.broadcasted_iota(jnp.int32, (SUBLANES, 1), 0)
    diag = (lax.broadcasted_iota(jnp.int32, (blk, blk), 0)
            == lax.broadcasted_iota(jnp.int32, (blk, blk), 1))
    contract_n = (((1,), (1,)), ((), ()))

    for gg in range(2):
        rows_g = slice(gg * blk, (gg + 1) * blk)
        cols_g = slice(gg * blk, (gg + 1) * blk)
        cols_other = slice((1 - gg) * blk, (2 - gg) * blk)
        w_intra = jnp.where(diag, d_ref[0, gg], 0.0)
        for d in range(2):
            forward = d == 0
            a_re = a_re_ref[d, 0, gg]
            a_im = a_im_ref[d, 0, gg]
            dt = jnp.exp(ls_ref[d, 0, gg])
            mag = jnp.exp(a_re * dt)
            lam = (mag * jnp.cos(a_im * dt), mag * jnp.sin(a_im * dt))
            den = a_re * a_re + a_im * a_im
            p, q = lam[0] - 1.0, lam[1]
            k = ((p * a_re + q * a_im) / den, (q * a_re - p * a_im) / den)
            bbar = _cmul(*k, bt_re_ref[d, 0, gg], bt_im_ref[d, 0, gg])
            c = (c_re_ref[d, 0, gg], c_im_ref[d, 0, gg])
            pw = [(jnp.ones_like(p), jnp.zeros_like(p))]
            for _ in range(n_tok):
                pw.append(_cmul(*pw[-1], *lam))

            expo = [n_tok - 1 - i if forward else i for i in range(n_tok)]
            cols = [_cmul(*pw[e], *bbar) for e in expo]
            m_re = jnp.concatenate([t[0] for t in cols], axis=0)
            m_im = jnp.concatenate([t[1] for t in cols], axis=0)
            for comp, m in ((2 * d, m_re), (2 * d + 1, m_im)):
                lo = comp * LANES + gg * S5_STATE
                other = comp * LANES + (1 - gg) * S5_STATE
                wst_ref[0, rows_g, lo:lo + S5_STATE] = m.astype(BF16)
                wst_ref[0, rows_g, other:other + S5_STATE] = jnp.zeros((blk, S5_STATE), BF16)

            c_rep = [jnp.concatenate([t] * n_tok, axis=0) for t in c]
            kr = (lax.dot_general(m_re, c_rep[0], contract_n, precision=hp,
                                  preferred_element_type=F32)
                  - lax.dot_general(m_im, c_rep[1], contract_n, precision=hp,
                                    preferred_element_type=F32))
            toep = jnp.zeros((blk, blk), F32)
            for j in range(n_tok):
                s = S5_GROUP * (n_tok - 1 - j if forward else j)
                if s == 0:
                    moved = kr
                elif forward:
                    moved = jnp.concatenate([kr[s:], jnp.zeros((s, blk), F32)], axis=0)
                else:
                    moved = jnp.concatenate([jnp.zeros((s, blk), F32), kr[:blk - s]], axis=0)
                toep = jnp.where(tok_of_lane == j, moved, toep)
            w_intra = w_intra + toep

            expo = [j + 1 if forward else n_tok - j for j in range(n_tok)]
            rows = [_cmul(*pw[e], *c) for e in expo]
            r_re = jnp.concatenate([t[0] for t in rows], axis=0).T
            r_im = -jnp.concatenate([t[1] for t in rows], axis=0).T
            for comp, r in ((2 * d, r_re), (2 * d + 1, r_im)):
                r0 = PAIR_W + comp * LANES + gg * S5_STATE
                wy_ref[0, r0:r0 + S5_STATE, cols_g] = r.astype(BF16)
                wy_ref[0, r0:r0 + S5_STATE, cols_other] = jnp.zeros((S5_STATE, blk), BF16)

            lp = [(jnp.ones_like(p), jnp.zeros_like(p)), pw[n_tok]]
            for _ in range(SUBLANES - 1):
                lp.append(_cmul(*lp[-1], *pw[n_tok]))
            lanes = slice(d * LANES + gg * S5_STATE, d * LANES + (gg + 1) * S5_STATE)
            for kk, s in enumerate((1, 2, 4)):
                valid = (row8 >= s) if forward else (row8 + s < SUBLANES)
                coef_ref[0, 2 * kk, :, lanes] = jnp.where(valid, lp[s][0], 0.0)
                coef_ref[0, 2 * kk + 1, :, lanes] = jnp.where(valid, lp[s][1], 0.0)
            q_re = jnp.zeros((SUBLANES, S5_STATE), F32)
            q_im = jnp.zeros((SUBLANES, S5_STATE), F32)
            for r in range(SUBLANES):
                e = r + 1 if forward else SUBLANES - r
                q_re = jnp.where(row8 == r, lp[e][0], q_re)
                q_im = jnp.where(row8 == r, lp[e][1], q_im)
            coef_ref[0, 6, :, lanes] = q_re
            coef_ref[0, 7, :, lanes] = q_im

        wy_ref[0, rows_g, cols_g] = w_intra.astype(BF16)
        wy_ref[0, rows_g, cols_other] = jnp.zeros((blk, blk), BF16)


def _s5_operators(a_re, a_im, log_step, b_re, b_im, c_re, c_im, d_skip):
    grp = lambda t, *tail: t.reshape(2, S5_PAIRS, 2, *tail)
    row = lambda t: grp(t, 1, t.shape[-1])
    bt_re = grp(jnp.swapaxes(b_re, -1, -2), S5_GROUP, S5_STATE)
    bt_im = grp(jnp.swapaxes(b_im, -1, -2), S5_GROUP, S5_STATE)
    d_tiled = jnp.tile(d_skip.reshape(S5_PAIRS, 2, 1, S5_GROUP), (1, 1, 1, S5_CHUNK))
    per_pair = lambda *tail: pl.BlockSpec((2, 1, 2) + tail, lambda p: (0, p, 0) + (0,) * len(tail))
    out = lambda *tail: pl.BlockSpec((1,) + tail, lambda p: (p,) + (0,) * len(tail))
    return pl.pallas_call(
        _s5_ops_kernel,
        grid=(S5_PAIRS,),
        in_specs=[per_pair(1, S5_STATE), per_pair(1, S5_STATE), per_pair(1, 1),
                  per_pair(S5_GROUP, S5_STATE), per_pair(S5_GROUP, S5_STATE),
                  per_pair(S5_GROUP, S5_STATE), per_pair(S5_GROUP, S5_STATE),
                  pl.BlockSpec((1, 2, 1, S5_CHUNK * S5_GROUP), lambda p: (p, 0, 0, 0))],
        out_specs=[out(PAIR_W, PAIR_W), out(2 * PAIR_W, PAIR_W), out(8, SUBLANES, 2 * LANES)],
        out_shape=[jax.ShapeDtypeStruct((S5_PAIRS, PAIR_W, PAIR_W), BF16),
                   jax.ShapeDtypeStruct((S5_PAIRS, 2 * PAIR_W, PAIR_W), BF16),
                   jax.ShapeDtypeStruct((S5_PAIRS, 8, SUBLANES, 2 * LANES), F32)],
        compiler_params=_cparams(("parallel",)),
        name="s5_ops",
    )(row(a_re), row(a_im), grp(log_step, 1, 1), bt_re, bt_im,
      grp(c_re, S5_GROUP, S5_STATE), grp(c_im, S5_GROUP, S5_STATE), d_tiled)


def _s5_kernel(u_ref, uc_ref, wst_ref, wy_ref, coef_ref, y_ref, s_scr, sin_scr):
    n_main = u_ref.shape[1]
    n_ctx = uc_ref.shape[1]
    n_rows = n_ctx + n_main + n_ctx

    for pp in range(PAIRS_PER_SLAB):
        s_ctx = _bdot(uc_ref[pp], wst_ref[pp])
        s_scr[pp, 0:n_ctx] = s_ctx
        s_scr[pp, n_ctx:n_ctx + n_main] = _bdot(u_ref[pp], wst_ref[pp])
        s_scr[pp, n_ctx + n_main:] = s_ctx

    row = lax.broadcasted_iota(jnp.int32, (SUBLANES, LANES), 0)
    n_groups = n_rows // SUBLANES
    n_iter = (n_ctx + n_main) // SUBLANES

    def group(pp, base, c_re, c_im, lane0, forward):
        rows = pl.ds(base, SUBLANES)
        re_l = slice(lane0, lane0 + LANES)
        im_l = slice(lane0 + LANES, lane0 + 2 * LANES)
        cf = slice(0, LANES) if forward else slice(LANES, 2 * LANES)
        xr = s_scr[pp, rows, re_l]
        xi = s_scr[pp, rows, im_l]
        for k, sh in enumerate((1, 2, 4)):
            ar = coef_ref[pp, 2 * k, :, cf]
            ai = coef_ref[pp, 2 * k + 1, :, cf]
            amt = sh if forward else SUBLANES - sh
            sr = pltpu.roll(xr, amt, 0)
            si = pltpu.roll(xi, amt, 0)
            xr, xi = xr + (ar * sr - ai * si), xi + (ar * si + ai * sr)
        qr = coef_ref[pp, 6, :, cf]
        qi = coef_ref[pp, 7, :, cf]
        hr = xr + (qr * c_re - qi * c_im)
        hi = xi + (qr * c_im + qi * c_re)
        edge = 0 if forward else SUBLANES - 1
        amt = 1 if forward else SUBLANES - 1
        sin_scr[pp, rows, re_l] = jnp.where(row == edge, c_re, pltpu.roll(hr, amt, 0))
        sin_scr[pp, rows, im_l] = jnp.where(row == edge, c_im, pltpu.roll(hi, amt, 0))
        last = SUBLANES - 1 if forward else 0
        return (jnp.broadcast_to(hr[last:last + 1], (SUBLANES, LANES)),
                jnp.broadcast_to(hi[last:last + 1], (SUBLANES, LANES)))

    def body(g, carry):
        new = []
        for pp in range(PAIRS_PER_SLAB):
            f_re, f_im, b_re, b_im = carry[4 * pp:4 * pp + 4]
            f_base = pl.multiple_of(g * SUBLANES, SUBLANES)
            b_base = pl.multiple_of((n_groups - 1 - g) * SUBLANES, SUBLANES)
            f_re, f_im = group(pp, f_base, f_re, f_im, 0, True)
            b_re, b_im = group(pp, b_base, b_re, b_im, 2 * LANES, False)
            new += [f_re, f_im, b_re, b_im]
        return tuple(new)

    zero = jnp.zeros((SUBLANES, LANES), F32)
    lax.fori_loop(0, n_iter, body, (zero,) * (4 * PAIRS_PER_SLAB))

    main_rows = slice(n_ctx, n_ctx + n_main)
    for pp in range(PAIRS_PER_SLAB):
        y_ref[pp] = (_bdot(u_ref[pp], wy_ref[pp, 0:PAIR_W])
                     + _bdot(sin_scr[pp, main_rows, :].astype(BF16), wy_ref[pp, PAIR_W:2 * PAIR_W]))


def _s5(u_pairs, uc_pairs, bsz, w_state, w_y, coef):
    n_main = u_pairs.shape[1] // bsz
    n_ctx = uc_pairs.shape[1] // bsz
    n_rows = n_main + 2 * n_ctx
    pps = PAIRS_PER_SLAB
    rows = lambda n: pl.BlockSpec((pps, n, PAIR_W), lambda j, b: (j, b, 0))
    return pl.pallas_call(
        _s5_kernel,
        grid=(S5_SLABS, bsz),
        in_specs=[rows(n_main), rows(n_ctx),
                  pl.BlockSpec((pps, PAIR_W, PAIR_W), lambda j, b: (j, 0, 0)),
                  pl.BlockSpec((pps, 2 * PAIR_W, PAIR_W), lambda j, b: (j, 0, 0)),
                  pl.BlockSpec((pps, 8, SUBLANES, 2 * LANES), lambda j, b: (j, 0, 0, 0))],
        out_specs=rows(n_main),
        out_shape=jax.ShapeDtypeStruct(u_pairs.shape, F32),
        scratch_shapes=[pltpu.VMEM((pps, n_rows, PAIR_W), F32),
                        pltpu.VMEM((pps, n_rows, PAIR_W), F32)],
        compiler_params=_cparams(("arbitrary", "arbitrary")),
        name="s5",
    )(u_pairs, uc_pairs, w_state, w_y, coef)


def _mix_kernel(y_ref, x_ref, ga_ref, mb_ref, mod_ref, wglu_ref, bglu_ref, wpa_ref, wout_ref, o_ref,
                y_scr):
    m = mod_ref[0]
    cpt = y_ref.shape[1]
    for j in range(S5_SLABS):
        slabs = []
        for a in range(S5_CHUNK):
            pp, blk = _pair_lane_block(a)
            slabs.append(y_ref[j * PAIRS_PER_SLAB + pp, :, blk * LANES:(blk + 1) * LANES])
        slabs = _swap_slab_and_piece(slabs)
        for t in range(S5_CHUNK):
            y_scr[j, pl.ds(t, cpt, stride=S5_CHUNK), :] = slabs[t]
    ya = _gelu_tanh(jnp.concatenate([y_scr[j] for j in range(S5_SLABS)], axis=1))
    glu = ya * jax.nn.sigmoid(_bdot(ya.astype(BF16), wglu_ref[...]) + bglu_ref[...])
    pa = _bdot(glu.astype(BF16), wpa_ref[...])
    merged = (ga_ref[0].astype(F32) * pa + mb_ref[0].astype(F32)).astype(BF16)
    o_ref[0] = x_ref[0] + m[2:3] * _bdot(merged, wout_ref[...])


def _mix(y_pairs, x, ga, mb, mod3, w_glu, b_glu, w_proj_a, w_out):
    bsz, length, _ = x.shape
    tt = TOK_TILE
    nt = length // tt
    tok = lambda w: pl.BlockSpec((1, tt, w), lambda b, i: (b, i, 0))
    return pl.pallas_call(
        _mix_kernel,
        grid=(bsz, nt),
        in_specs=[pl.BlockSpec((S5_PAIRS, tt // S5_CHUNK, PAIR_W), lambda b, i: (0, b * nt + i, 0)),
                  tok(D_MODEL), tok(D_MODEL), tok(D_MODEL),
                  pl.BlockSpec((1, N_MOD, D_MODEL), lambda b, i: (b, 0, 0)),
                  _const_spec(w_glu.shape), _const_spec(b_glu.shape),
                  _const_spec(w_proj_a.shape), _const_spec(w_out.shape)],
        out_specs=tok(D_MODEL),
        out_shape=jax.ShapeDtypeStruct(x.shape, F32),
        scratch_shapes=[pltpu.VMEM((S5_SLABS, tt, LANES), F32)],
        compiler_params=_cparams(("parallel", "parallel")),
        name="mix",
    )(y_pairs, x, ga, mb, mod3, w_glu, b_glu, w_proj_a, w_out)


def _grid_conv(u, w9, bias, tt):
    ch = u.shape[1]
    v = [u[0:tt] * w9[c:c + 1] + u[GRID_W:GRID_W + tt] * w9[3 + c:4 + c]
         + u[2 * GRID_W:2 * GRID_W + tt] * w9[6 + c:7 + c] for c in range(3)]
    shape4 = (tt // GRID_W, GRID_W // SUBLANES, SUBLANES, ch)
    sub = lax.broadcasted_iota(jnp.int32, (1, 1, SUBLANES, 1), 2)
    zero = jnp.zeros((shape4[0], 1, SUBLANES, ch), F32)
    down = pltpu.roll(v[0].reshape(shape4), 1, 2)
    left = jnp.where(sub == 0, jnp.concatenate([zero, down[:, :-1]], axis=1), down)
    up = pltpu.roll(v[2].reshape(shape4), SUBLANES - 1, 2)
    right = jnp.where(sub == SUBLANES - 1, jnp.concatenate([up[:, 1:], zero], axis=1), up)
    return v[1] + bias + (left + right).reshape(tt, ch)


def _ffn_kernel(top_ref, x_ref, bot_ref, mod_ref, g_ref, wup_ref, cw_ref, cb_ref, wdn_ref, gf_ref,
                o_ref, h_scr):
    i = pl.program_id(1)
    n_i = pl.num_programs(1)
    m = mod_ref[0]
    g = g_ref[...]
    tt = x_ref.shape[1]
    x = x_ref[0]

    def norm(t):
        return _modulated_rms(t, g, m[3:4], m[4:5])

    h_scr[0:GRID_W] = jnp.where(i > 0, norm(top_ref[0]), 0.0).astype(BF16)
    h_scr[GRID_W:GRID_W + tt] = norm(x).astype(BF16)
    h_scr[GRID_W + tt:] = jnp.where(i < n_i - 1, norm(bot_ref[0]), 0.0).astype(BF16)
    h = h_scr[...]

    fc = FFN_CHUNK
    n_chunks = FFN_HIDDEN // fc
    down = None
    k0 = 0
    acts = []
    for j in range(n_chunks):
        gs = slice(j * fc, (j + 1) * fc)
        vs = slice(FFN_HIDDEN + j * fc, FFN_HIDDEN + (j + 1) * fc)
        cg = _grid_conv(_bdot(h, wup_ref[:, gs]), cw_ref[:, gs], cb_ref[:, gs], tt)
        cv = _grid_conv(_bdot(h, wup_ref[:, vs]), cw_ref[:, vs], cb_ref[:, vs], tt)
        acts.append(((cg * jax.nn.sigmoid(cg)) * cv).astype(BF16))
        if (j + 1) % FFN_DOWN_GROUP == 0 or j == n_chunks - 1:
            part = _bdot(jnp.concatenate(acts, axis=1), wdn_ref[k0:(j + 1) * fc, :])
            down = part if down is None else down + part
            k0 = (j + 1) * fc
            acts = []

    x2 = x + m[5:6] * down
    ms = jnp.mean(x2 * x2, axis=-1, keepdims=True)
    o_ref[0] = (x2 * lax.rsqrt(ms + EPS)) * gf_ref[...]


def _ffn(x1, mod3, g_ffn, w_up, conv_w9, conv_b, w_down, g_final):
    bsz, length, _ = x1.shape
    tt = TOK_TILE
    rows_per_tile = tt // GRID_W
    n_rows = length // GRID_W
    halo = lambda fn: pl.BlockSpec((1, GRID_W, D_MODEL), fn)
    return pl.pallas_call(
        _ffn_kernel,
        grid=(bsz, length // tt),
        in_specs=[halo(lambda b, i: (b, jnp.maximum(i * rows_per_tile - 1, 0), 0)),
                  pl.BlockSpec((1, tt, D_MODEL), lambda b, i: (b, i, 0)),
                  halo(lambda b, i: (b, jnp.minimum((i + 1) * rows_per_tile, n_rows - 1), 0)),
                  pl.BlockSpec((1, N_MOD, D_MODEL), lambda b, i: (b, 0, 0)),
                  _const_spec(g_ffn.shape), _const_spec(w_up.shape), _const_spec(conv_w9.shape),
                  _const_spec(conv_b.shape), _const_spec(w_down.shape), _const_spec(g_final.shape)],
        out_specs=pl.BlockSpec((1, tt, D_MODEL), lambda b, i: (b, i, 0)),
        out_shape=jax.ShapeDtypeStruct(x1.shape, F32),
        scratch_shapes=[pltpu.VMEM((tt + 2 * GRID_W, D_MODEL), BF16)],
        compiler_params=_cparams(("parallel", "parallel")),
        name="ffn",
    )(x1, x1, x1, mod3, g_ffn, w_up, conv_w9, conv_b, w_down, g_final)


def kernel(x, c, ctx, c_ctx, w_ada, b_ada, g_mix, w_in, s5_a_re, s5_a_im, s5_log_step, s5_b_re, s5_b_im, s5_c_re, s5_c_im, s5_d, s5_w_glu, s5_b_glu, sgu_ln_g, sgu_ln_b, sgu_w, sgu_b, w_proj_a, w_proj_b, b_gate, w_out, g_ffn, w_up, conv_w, conv_b, w_down, g_final):
    bsz = x.shape[0]
    row = lambda t: t.reshape(1, -1)

    ctx_row = bsz
    c_rows = jnp.zeros((SUBLANES, D_MODEL), F32).at[:bsz].set(c).at[ctx_row].set(c_ctx)
    mod3 = _ada_rows(c_rows, w_ada[0], b_ada[0]).reshape(SUBLANES, N_MOD, D_MODEL)

    w_in_b = w_in[0].astype(BF16)
    wsp = sgu_w[0].astype(BF16)
    wsp_pair = jnp.concatenate([wsp[0::2], wsp[1::2]], axis=-1)
    bsp_rows = jnp.repeat(sgu_b[0].T, SGU_GROUP_DIM, axis=1)

    u_pairs, ga, mb = _inproj(x, mod3, row(g_mix[0]), w_in_b, row(b_gate[0]), row(sgu_ln_g[0]),
                              row(sgu_ln_b[0]), wsp_pair, bsp_rows, w_proj_b[0].astype(BF16))
    uc_pairs = _ctxproj(ctx, mod3, ctx_row, row(g_mix[0]), w_in_b[:, :S5_WIDTH])

    w_state, w_y, coef = _s5_operators(
        s5_a_re[0], s5_a_im[0], s5_log_step[0], s5_b_re[0], s5_b_im[0], s5_c_re[0], s5_c_im[0],
        s5_d[0])
    y_pairs = _s5(u_pairs, uc_pairs, bsz, w_state, w_y, coef)

    x1 = _mix(y_pairs, x, ga, mb, mod3, s5_w_glu[0].astype(BF16), row(s5_b_glu[0]),
              w_proj_a[0].astype(BF16), w_out[0].astype(BF16))

    return _ffn(x1, mod3, row(g_ffn[0]), w_up[0].astype(BF16), conv_w[0].reshape(9, -1),
                row(conv_b[0]), w_down[0].astype(BF16), row(g_final))
```

```python
import math

import jax
import jax.numpy as jnp
from jax import lax
from jax.experimental import pallas as pl
from jax.experimental.pallas import tpu as pltpu

F32 = jnp.float32
BF16 = jnp.bfloat16

D_MODEL = 1024
GRID_W = 64
S5_WIDTH = 512
S5_GROUP = 16
S5_GROUPS = S5_WIDTH // S5_GROUP
S5_STATE = 64
SGU_WIDTH = 512
SGU_GROUPS = 8
SGU_GROUP_DIM = SGU_WIDTH // SGU_GROUPS
CHUNK = 128
FFN_HIDDEN = 2816
N_MOD = 6
EPS = 1e-6

LANES = 128
SUBLANES = 8
V7X_VMEM_LIMIT = 56 * 1024 * 1024

S5_CHUNK = 16
S5_PAIRS = S5_GROUPS // 2
PAIR_W = 2 * S5_CHUNK * S5_GROUP
S5_SLABS = S5_WIDTH // LANES
PAIRS_PER_SLAB = S5_PAIRS // S5_SLABS

TOK_TILE = 512
FFN_TILE = 512
FFN_CHUNK = 256


def _cparams(sem):
    return pltpu.CompilerParams(dimension_semantics=sem, vmem_limit_bytes=V7X_VMEM_LIMIT)


def _const_spec(shape):
    zeros = (0,) * len(shape)
    return pl.BlockSpec(shape, lambda *_: zeros, pipeline_mode=pl.Buffered(1))


def _gelu_tanh(x):
    c = math.sqrt(2.0 / math.pi)
    return 0.5 * x * (1.0 + jnp.tanh(c * (x + 0.044715 * (x * x * x))))


def _modulated_rms(x, g, shift, scale):
    ms = jnp.mean(x * x, axis=-1, keepdims=True)
    return (x * lax.rsqrt(ms + EPS)) * (g * (1.0 + scale)) + shift


def _bdot(a, b):
    return jnp.dot(a, b, preferred_element_type=F32)


def _ada_kernel(c_ref, w_ref, b_ref, o_ref):
    c = c_ref[...]
    cs = c * jax.nn.sigmoid(c)
    o_ref[...] = jnp.dot(cs, w_ref[...], preferred_element_type=F32,
                         precision=lax.Precision.HIGHEST) + b_ref[...]


def _ada_rows(c_rows, w_ada, b_ada):
    n = w_ada.shape[1]
    bn = 1536
    return pl.pallas_call(
        _ada_kernel,
        grid=(n // bn,),
        in_specs=[_const_spec(c_rows.shape),
                  pl.BlockSpec((D_MODEL, bn), lambda j: (0, j)),
                  pl.BlockSpec((1, bn), lambda j: (0, j))],
        out_specs=pl.BlockSpec((c_rows.shape[0], bn), lambda j: (0, j)),
        out_shape=jax.ShapeDtypeStruct((c_rows.shape[0], n), F32),
        compiler_params=_cparams(("arbitrary",)),
        name="ada",
    )(c_rows, w_ada, b_ada.reshape(1, n))


def _swap_slab_and_piece(slabs):
    lane = lax.broadcasted_iota(jnp.int32, (1, LANES), 1)
    out = list(slabs)
    for k in range(3):
        width = S5_GROUP << k
        keep = ((lane >> (4 + k)) & 1) == 0
        nxt = list(out)
        for a0 in range(len(out)):
            if a0 & (1 << k):
                continue
            a1 = a0 | (1 << k)
            nxt[a0] = jnp.where(keep, out[a0], pltpu.roll(out[a1], width, 1))
            nxt[a1] = jnp.where(keep, pltpu.roll(out[a0], LANES - width, 1), out[a1])
        out = nxt
    return out


def _pair_lane_block(slab_idx):
    t_hi, g = divmod(slab_idx, SUBLANES)
    pp, gg = divmod(g, 2)
    return pp, gg * 2 + t_hi


def _store_pair_tiles(u, u_scr, up_ref):
    cpt = u.shape[0] // S5_CHUNK
    for j in range(S5_SLABS):
        u_scr[j] = u[:, j * LANES:(j + 1) * LANES]
    for j in range(S5_SLABS):
        slabs = _swap_slab_and_piece(
            [u_scr[j, pl.ds(t, cpt, stride=S5_CHUNK), :] for t in range(S5_CHUNK)])
        for a in range(S5_CHUNK):
            pp, blk = _pair_lane_block(a)
            up_ref[j * PAIRS_PER_SLAB + pp, :, blk * LANES:(blk + 1) * LANES] = slabs[a].astype(BF16)


def _inproj_kernel(x_ref, mod_ref, g_ref, w_ref, bgate_ref, lng_ref, lnb_ref, wsp_ref, bsp_ref,
                   wpb_ref, u_ref, ga_ref, mb_ref, u_scr):
    m = mod_ref[0]
    h = _modulated_rms(x_ref[0], g_ref[...], m[0:1], m[1:2]).astype(BF16)
    tt = h.shape[0]

    _store_pair_tiles(_bdot(h, w_ref[:, 0:S5_WIDTH]), u_scr, u_ref)

    z0 = S5_WIDTH
    g0 = S5_WIDTH + 2 * SGU_WIDTH
    zv_pre = _bdot(h, w_ref[:, z0 + SGU_WIDTH:z0 + 2 * SGU_WIDTH])
    zu_pre = _bdot(h, w_ref[:, z0:z0 + SGU_WIDTH])
    ga_pre = _bdot(h, w_ref[:, g0:g0 + D_MODEL])
    gb_pre = _bdot(h, w_ref[:, g0 + D_MODEL:g0 + 2 * D_MODEL])

    zv = _gelu_tanh(zv_pre)
    mu = jnp.mean(zv, axis=-1, keepdims=True)
    zc = zv - mu
    var = jnp.mean(zc * zc, axis=-1, keepdims=True)
    v = ((zc * lax.rsqrt(var + EPS)) * lng_ref[...] + lnb_ref[...]).astype(BF16)
    zu = _gelu_tanh(zu_pre)

    lane = lax.broadcasted_iota(jnp.int32, (CHUNK, LANES), 1)
    zero = jnp.zeros((CHUNK, LANES), BF16)
    rows = []
    for c in range(tt // CHUNK):
        tiles = []
        for j in range(SGU_WIDTH // LANES):
            vj = v[c * CHUNK:(c + 1) * CHUNK, j * LANES:(j + 1) * LANES]
            rhs = jnp.concatenate([jnp.where(lane < SGU_GROUP_DIM, vj, zero),
                                   jnp.where(lane >= SGU_GROUP_DIM, vj, zero)], axis=0)
            tiles.append(_bdot(wsp_ref[j], rhs))
        rows.append(jnp.concatenate(tiles, axis=1) + bsp_ref[...])
    s = jnp.concatenate(rows, axis=0)
    yb = (zu * s).astype(BF16)
    pb = _bdot(yb, wpb_ref[...])

    ga = jax.nn.sigmoid(ga_pre + bgate_ref[:, 0:D_MODEL])
    gb = jax.nn.sigmoid(gb_pre + bgate_ref[:, D_MODEL:2 * D_MODEL])
    ga_ref[0] = ga.astype(BF16)
    mb_ref[0] = (gb * pb).astype(BF16)


def _inproj(x, mod3, g_mix, w_in, b_gate, ln_g, ln_b, wsp_pair, bsp_rows, w_proj_b):
    bsz, length, _ = x.shape
    tt = TOK_TILE
    nt = length // tt
    tok = lambda w: pl.BlockSpec((1, tt, w), lambda b, i: (b, i, 0))
    return pl.pallas_call(
        _inproj_kernel,
        grid=(bsz, nt),
        in_specs=[tok(D_MODEL),
                  pl.BlockSpec((1, N_MOD, D_MODEL), lambda b, i: (b, 0, 0)),
                  _const_spec(g_mix.shape), _const_spec(w_in.shape), _const_spec(b_gate.shape),
                  _const_spec(ln_g.shape), _const_spec(ln_b.shape), _const_spec(wsp_pair.shape),
                  _const_spec(bsp_rows.shape), _const_spec(w_proj_b.shape)],
        out_specs=[pl.BlockSpec((S5_PAIRS, tt // S5_CHUNK, PAIR_W), lambda b, i: (0, b * nt + i, 0)),
                   tok(D_MODEL), tok(D_MODEL)],
        out_shape=[jax.ShapeDtypeStruct((S5_PAIRS, bsz * length // S5_CHUNK, PAIR_W), BF16),
                   jax.ShapeDtypeStruct((bsz, length, D_MODEL), BF16),
                   jax.ShapeDtypeStruct((bsz, length, D_MODEL), BF16)],
        scratch_shapes=[pltpu.VMEM((S5_SLABS, tt, LANES), F32)],
        compiler_params=_cparams(("parallel", "parallel")),
        name="inproj",
    )(x, mod3, g_mix, w_in, b_gate, ln_g, ln_b, wsp_pair, bsp_rows, w_proj_b)


def _ctxproj_kernel(x_ref, mod_ref, g_ref, w_ref, u_ref, u_scr):
    m = mod_ref[0]
    h = _modulated_rms(x_ref[0], g_ref[...], m[0:1], m[1:2]).astype(BF16)
    _store_pair_tiles(_bdot(h, w_ref[...]), u_scr, u_ref)


def _ctxproj(ctx, mod3, ctx_row, g_mix, w_in_u):
    bsz, length, _ = ctx.shape
    return pl.pallas_call(
        _ctxproj_kernel,
        grid=(bsz,),
        in_specs=[pl.BlockSpec((1, length, D_MODEL), lambda b: (b, 0, 0)),
                  pl.BlockSpec((1, N_MOD, D_MODEL), lambda b: (ctx_row, 0, 0)),
                  _const_spec(g_mix.shape), _const_spec(w_in_u.shape)],
        out_specs=pl.BlockSpec((S5_PAIRS, length // S5_CHUNK, PAIR_W), lambda b: (0, b, 0)),
        out_shape=jax.ShapeDtypeStruct((S5_PAIRS, bsz * length // S5_CHUNK, PAIR_W), BF16),
        scratch_shapes=[pltpu.VMEM((S5_SLABS, length, LANES), F32)],
        compiler_params=_cparams(("parallel",)),
        name="ctxproj",
    )(ctx, mod3, g_mix, w_in_u)


def _cmul(ar, ai, br, bi):
    return ar * br - ai * bi, ar * bi + ai * br


def _s5_ops_kernel(a_re_ref, a_im_ref, ls_ref, bt_re_ref, bt_im_ref, c_re_ref, c_im_ref, d_ref,
                   wst_ref, wy_ref, coef_ref):
    hp = lax.Precision.HIGHEST
    n_tok = S5_CHUNK
    blk = n_tok * S5_GROUP
    tok_of_lane = lax.broadcasted_iota(jnp.int32, (1, blk), 1) // S5_GROUP
    row8 = lax.broadcasted_iota(jnp.int32, (SUBLANES, 1), 0)
    diag = (lax.broadcasted_iota(jnp.int32, (blk, blk), 0)
            == lax.broadcasted_iota(jnp.int32, (blk, blk), 1))
    contract_n = (((1,), (1,)), ((), ()))

    for gg in range(2):
        rows_g = slice(gg * blk, (gg + 1) * blk)
        cols_g = slice(gg * blk, (gg + 1) * blk)
        cols_other = slice((1 - gg) * blk, (2 - gg) * blk)
        w_intra = jnp.where(diag, d_ref[0, gg], 0.0)
        for d in range(2):
            forward = d == 0
            a_re = a_re_ref[d, 0, gg]
            a_im = a_im_ref[d, 0, gg]
            dt = jnp.exp(ls_ref[d, 0, gg])
            mag = jnp.exp(a_re * dt)
            lam = (mag * jnp.cos(a_im * dt), mag * jnp.sin(a_im * dt))
            den = a_re * a_re + a_im * a_im
            p, q = lam[0] - 1.0, lam[1]
            k = ((p * a_re + q * a_im) / den, (q * a_re - p * a_im) / den)
            bbar = _cmul(*k, bt_re_ref[d, 0, gg], bt_im_ref[d, 0, gg])
            c = (c_re_ref[d, 0, gg], c_im_ref[d, 0, gg])
            pw = [(jnp.ones_like(p), jnp.zeros_like(p))]
            for _ in range(n_tok):
                pw.append(_cmul(*pw[-1], *lam))

            expo = [n_tok - 1 - i if forward else i for i in range(n_tok)]
            cols = [_cmul(*pw[e], *bbar) for e in expo]
            m_re = jnp.concatenate([t[0] for t in cols], axis=0)
            m_im = jnp.concatenate([t[1] for t in cols], axis=0)
            for comp, m in ((2 * d, m_re), (2 * d + 1, m_im)):
                lo = comp * LANES + gg * S5_STATE
                other = comp * LANES + (1 - gg) * S5_STATE
                wst_ref[0, rows_g, lo:lo + S5_STATE] = m.astype(BF16)
                wst_ref[0, rows_g, other:other + S5_STATE] = jnp.zeros((blk, S5_STATE), BF16)

            c_rep = [jnp.concatenate([t] * n_tok, axis=0) for t in c]
            kr = (lax.dot_general(m_re, c_rep[0], contract_n, precision=hp,
                                  preferred_element_type=F32)
                  - lax.dot_general(m_im, c_rep[1], contract_n, precision=hp,
                                    preferred_element_type=F32))
            toep = jnp.zeros((blk, blk), F32)
            for j in range(n_tok):
                s = S5_GROUP * (n_tok - 1 - j if forward else j)
                if s == 0:
                    moved = kr
                elif forward:
                    moved = jnp.concatenate([kr[s:], jnp.zeros((s, blk), F32)], axis=0)
                else:
                    moved = jnp.concatenate([jnp.zeros((s, blk), F32), kr[:blk - s]], axis=0)
                toep = jnp.where(tok_of_lane == j, moved, toep)
            w_intra = w_intra + toep

            expo = [j + 1 if forward else n_tok - j for j in range(n_tok)]
            rows = [_cmul(*pw[e], *c) for e in expo]
            r_re = jnp.concatenate([t[0] for t in rows], axis=0).T
            r_im = -jnp.concatenate([t[1] for t in rows], axis=0).T
            for comp, r in ((2 * d, r_re), (2 * d + 1, r_im)):
                r0 = PAIR_W + comp * LANES + gg * S5_STATE
                wy_ref[0, r0:r0 + S5_STATE, cols_g] = r.astype(BF16)
                wy_ref[0, r0:r0 + S5_STATE, cols_other] = jnp.zeros((S5_STATE, blk), BF16)

            lp = [(jnp.ones_like(p), jnp.zeros_like(p)), pw[n_tok]]
            for _ in range(SUBLANES - 1):
                lp.append(_cmul(*lp[-1], *pw[n_tok]))
            lanes = slice(d * LANES + gg * S5_STATE, d * LANES + (gg + 1) * S5_STATE)
            for kk, s in enumerate((1, 2, 4)):
                valid = (row8 >= s) if forward else (row8 + s < SUBLANES)
                coef_ref[0, 2 * kk, :, lanes] = jnp.where(valid, lp[s][0], 0.0)
                coef_ref[0, 2 * kk + 1, :, lanes] = jnp.where(valid, lp[s][1], 0.0)
            q_re = jnp.zeros((SUBLANES, S5_STATE), F32)
            q_im = jnp.zeros((SUBLANES, S5_STATE), F32)
            for r in range(SUBLANES):
                e = r + 1 if forward else SUBLANES - r
                q_re = jnp.where(row8 == r, lp[e][0], q_re)
                q_im = jnp.where(row8 == r, lp[e][1], q_im)
            coef_ref[0, 6, :, lanes] = q_re
            coef_ref[0, 7, :, lanes] = q_im

        wy_ref[0, rows_g, cols_g] = w_intra.astype(BF16)
        wy_ref[0, rows_g, cols_other] = jnp.zeros((blk, blk), BF16)


def _s5_operators(a_re, a_im, log_step, b_re, b_im, c_re, c_im, d_skip):
    grp = lambda t, *tail: t.reshape(2, S5_PAIRS, 2, *tail)
    row = lambda t: grp(t, 1, t.shape[-1])
    bt_re = grp(jnp.swapaxes(b_re, -1, -2), S5_GROUP, S5_STATE)
    bt_im = grp(jnp.swapaxes(b_im, -1, -2), S5_GROUP, S5_STATE)
    d_tiled = jnp.tile(d_skip.reshape(S5_PAIRS, 2, 1, S5_GROUP), (1, 1, 1, S5_CHUNK))
    per_pair = lambda *tail: pl.BlockSpec((2, 1, 2) + tail, lambda p: (0, p, 0) + (0,) * len(tail))
    out = lambda *tail: pl.BlockSpec((1,) + tail, lambda p: (p,) + (0,) * len(tail))
    return pl.pallas_call(
        _s5_ops_kernel,
        grid=(S5_PAIRS,),
        in_specs=[per_pair(1, S5_STATE), per_pair(1, S5_STATE), per_pair(1, 1),
                  per_pair(S5_GROUP, S5_STATE), per_pair(S5_GROUP, S5_STATE),
                  per_pair(S5_GROUP, S5_STATE), per_pair(S5_GROUP, S5_STATE),
                  pl.BlockSpec((1, 2, 1, S5_CHUNK * S5_GROUP), lambda p: (p, 0, 0, 0))],
        out_specs=[out(PAIR_W, PAIR_W), out(2 * PAIR_W, PAIR_W), out(8, SUBLANES, 2 * LANES)],
        out_shape=[jax.ShapeDtypeStruct((S5_PAIRS, PAIR_W, PAIR_W), BF16),
                   jax.ShapeDtypeStruct((S5_PAIRS, 2 * PAIR_W, PAIR_W), BF16),
                   jax.ShapeDtypeStruct((S5_PAIRS, 8, SUBLANES, 2 * LANES), F32)],
        compiler_params=_cparams(("parallel",)),
        name="s5_ops",
    )(row(a_re), row(a_im), grp(log_step, 1, 1), bt_re, bt_im,
      grp(c_re, S5_GROUP, S5_STATE), grp(c_im, S5_GROUP, S5_STATE), d_tiled)


def _s5_kernel(u_ref, uc_ref, wst_ref, wy_ref, coef_ref, y_ref, s_scr, sin_scr):
    n_main = u_ref.shape[1]
    n_ctx = uc_ref.shape[1]
    n_rows = n_ctx + n_main + n_ctx

    for pp in range(PAIRS_PER_SLAB):
        s_ctx = _bdot(uc_ref[pp], wst_ref[pp])
        s_scr[pp, 0:n_ctx] = s_ctx
        s_scr[pp, n_ctx:n_ctx + n_main] = _bdot(u_ref[pp], wst_ref[pp])
        s_scr[pp, n_ctx + n_main:] = s_ctx

    row = lax.broadcasted_iota(jnp.int32, (SUBLANES, LANES), 0)
    n_groups = n_rows // SUBLANES
    n_iter = (n_ctx + n_main) // SUBLANES

    def group(pp, base, c_re, c_im, lane0, forward):
        rows = pl.ds(base, SUBLANES)
        re_l = slice(lane0, lane0 + LANES)
        im_l = slice(lane0 + LANES, lane0 + 2 * LANES)
        cf = slice(0, LANES) if forward else slice(LANES, 2 * LANES)
        xr = s_scr[pp, rows, re_l]
        xi = s_scr[pp, rows, im_l]
        for k, sh in enumerate((1, 2, 4)):
            ar = coef_ref[pp, 2 * k, :, cf]
            ai = coef_ref[pp, 2 * k + 1, :, cf]
            amt = sh if forward else SUBLANES - sh
            sr = pltpu.roll(xr, amt, 0)
            si = pltpu.roll(xi, amt, 0)
            xr, xi = xr + (ar * sr - ai * si), xi + (ar * si + ai * sr)
        qr = coef_ref[pp, 6, :, cf]
        qi = coef_ref[pp, 7, :, cf]
        hr = xr + (qr * c_re - qi * c_im)
        hi = xi + (qr * c_im + qi * c_re)
        edge = 0 if forward else SUBLANES - 1
        amt = 1 if forward else SUBLANES - 1
        sin_scr[pp, rows, re_l] = jnp.where(row == edge, c_re, pltpu.roll(hr, amt, 0))
        sin_scr[pp, rows, im_l] = jnp.where(row == edge, c_im, pltpu.roll(hi, amt, 0))
        last = SUBLANES - 1 if forward else 0
        return (jnp.broadcast_to(hr[last:last + 1], (SUBLANES, LANES)),
                jnp.broadcast_to(hi[last:last + 1], (SUBLANES, LANES)))

    def body(g, carry):
        new = []
        for pp in range(PAIRS_PER_SLAB):
            f_re, f_im, b_re, b_im = carry[4 * pp:4 * pp + 4]
            f_base = pl.multiple_of(g * SUBLANES, SUBLANES)
            b_base = pl.multiple_of((n_groups - 1 - g) * SUBLANES, SUBLANES)
            f_re, f_im = group(pp, f_base, f_re, f_im, 0, True)
            b_re, b_im = group(pp, b_base, b_re, b_im, 2 * LANES, False)
            new += [f_re, f_im, b_re, b_im]
        return tuple(new)

    zero = jnp.zeros((SUBLANES, LANES), F32)
    lax.fori_loop(0, n_iter, body, (zero,) * (4 * PAIRS_PER_SLAB))

    main_rows = slice(n_ctx, n_ctx + n_main)
    for pp in range(PAIRS_PER_SLAB):
        y_ref[pp] = (_bdot(u_ref[pp], wy_ref[pp, 0:PAIR_W])
                     + _bdot(sin_scr[pp, main_rows, :].astype(BF16), wy_ref[pp, PAIR_W:2 * PAIR_W]))


def _s5(u_pairs, uc_pairs, bsz, w_state, w_y, coef):
    n_main = u_pairs.shape[1] // bsz
    n_ctx = uc_pairs.shape[1] // bsz
    n_rows = n_main + 2 * n_ctx
    pps = PAIRS_PER_SLAB
    rows = lambda n: pl.BlockSpec((pps, n, PAIR_W), lambda j, b: (j, b, 0))
    return pl.pallas_call(
        _s5_kernel,
        grid=(S5_SLABS, bsz),
        in_specs=[rows(n_main), rows(n_ctx),
                  pl.BlockSpec((pps, PAIR_W, PAIR_W), lambda j, b: (j, 0, 0)),
                  pl.BlockSpec((pps, 2 * PAIR_W, PAIR_W), lambda j, b: (j, 0, 0)),
                  pl.BlockSpec((pps, 8, SUBLANES, 2 * LANES), lambda j, b: (j, 0, 0, 0))],
        out_specs=rows(n_main),
        out_shape=jax.ShapeDtypeStruct(u_pairs.shape, F32),
        scratch_shapes=[pltpu.VMEM((pps, n_rows, PAIR_W), F32),
                        pltpu.VMEM((pps, n_rows, PAIR_W), F32)],
        compiler_params=_cparams(("arbitrary", "arbitrary")),
        name="s5",
    )(u_pairs, uc_pairs, w_state, w_y, coef)


def _mix_kernel(y_ref, x_ref, ga_ref, mb_ref, mod_ref, wglu_ref, bglu_ref, wpa_ref, wout_ref, o_ref,
                y_scr):
    m = mod_ref[0]
    cpt = y_ref.shape[1]
    for j in range(S5_SLABS):
        slabs = []
        for a in range(S5_CHUNK):
            pp, blk = _pair_lane_block(a)
            slabs.append(y_ref[j * PAIRS_PER_SLAB + pp, :, blk * LANES:(blk + 1) * LANES])
        slabs = _swap_slab_and_piece(slabs)
        for t in range(S5_CHUNK):
            y_scr[j, pl.ds(t, cpt, stride=S5_CHUNK), :] = slabs[t]
    ya = _gelu_tanh(jnp.concatenate([y_scr[j] for j in range(S5_SLABS)], axis=1))
    glu = ya * jax.nn.sigmoid(_bdot(ya.astype(BF16), wglu_ref[...]) + bglu_ref[...])
    pa = _bdot(glu.astype(BF16), wpa_ref[...])
    merged = (ga_ref[0].astype(F32) * pa + mb_ref[0].astype(F32)).astype(BF16)
    o_ref[0] = x_ref[0] + m[2:3] * _bdot(merged, wout_ref[...])


def _mix(y_pairs, x, ga, mb, mod3, w_glu, b_glu, w_proj_a, w_out):
    bsz, length, _ = x.shape
    tt = TOK_TILE
    nt = length // tt
    tok = lambda w: pl.BlockSpec((1, tt, w), lambda b, i: (b, i, 0))
    return pl.pallas_call(
        _mix_kernel,
        grid=(bsz, nt),
        in_specs=[pl.BlockSpec((S5_PAIRS, tt // S5_CHUNK, PAIR_W), lambda b, i: (0, b * nt + i, 0)),
                  tok(D_MODEL), tok(D_MODEL), tok(D_MODEL),
                  pl.BlockSpec((1, N_MOD, D_MODEL), lambda b, i: (b, 0, 0)),
                  _const_spec(w_glu.shape), _const_spec(b_glu.shape),
                  _const_spec(w_proj_a.shape), _const_spec(w_out.shape)],
        out_specs=tok(D_MODEL),
        out_shape=jax.ShapeDtypeStruct(x.shape, F32),
        scratch_shapes=[pltpu.VMEM((S5_SLABS, tt, LANES), F32)],
        compiler_params=_cparams(("parallel", "parallel")),
        name="mix",
    )(y_pairs, x, ga, mb, mod3, w_glu, b_glu, w_proj_a, w_out)


def _conv_taps(cw_ref, cb_ref, cols):
    width = cols.stop - cols.start
    taps = [jnp.broadcast_to(cw_ref[k:k + 1, cols], (GRID_W, width)) for k in range(9)]
    return taps, jnp.broadcast_to(cb_ref[:, cols], (GRID_W, width))


def _grid_conv_row(u, taps, bias, rb):
    ch = u.shape[1]
    lo = rb * GRID_W
    above, here, below = (u[lo + k * GRID_W:lo + (k + 1) * GRID_W] for k in range(3))
    v = [above * taps[c] + here * taps[3 + c] + below * taps[6 + c] for c in range(3)]
    shape3 = (GRID_W // SUBLANES, SUBLANES, ch)
    sub = lax.broadcasted_iota(jnp.int32, (1, SUBLANES, 1), 1)
    zero = jnp.zeros((1, SUBLANES, ch), F32)
    down = pltpu.roll(v[0].reshape(shape3), 1, 1)
    left = jnp.where(sub == 0, jnp.concatenate([zero, down[:-1]], axis=0), down)
    up = pltpu.roll(v[2].reshape(shape3), SUBLANES - 1, 1)
    right = jnp.where(sub == SUBLANES - 1, jnp.concatenate([up[1:], zero], axis=0), up)
    return v[1] + bias + (left + right).reshape(GRID_W, ch)


def _ffn_kernel(top_ref, x_ref, bot_ref, mod_ref, g_ref, wup_ref, cw_ref, cb_ref, wdn_ref, gf_ref,
                o_ref, h_scr, act_scr):
    i = pl.program_id(1)
    n_i = pl.num_programs(1)
    m = mod_ref[0]
    g = g_ref[...]
    tt = x_ref.shape[1]
    x = x_ref[0]

    def norm(t):
        return _modulated_rms(t, g, m[3:4], m[4:5])

    h_scr[0:GRID_W] = jnp.where(i > 0, norm(top_ref[0]), 0.0).astype(BF16)
    h_scr[GRID_W:GRID_W + tt] = norm(x).astype(BF16)
    h_scr[GRID_W + tt:] = jnp.where(i < n_i - 1, norm(bot_ref[0]), 0.0).astype(BF16)

    h = h_scr[...]

    fc = FFN_CHUNK
    for j in range(FFN_HIDDEN // fc):
        gs = slice(j * fc, (j + 1) * fc)
        vs = slice(FFN_HIDDEN + j * fc, FFN_HIDDEN + (j + 1) * fc)
        ug = _bdot(h, wup_ref[:, gs])
        uv = _bdot(h, wup_ref[:, vs])
        g_taps, g_bias = _conv_taps(cw_ref, cb_ref, gs)
        v_taps, v_bias = _conv_taps(cw_ref, cb_ref, vs)
        for rb in range(tt // GRID_W):
            cg = _grid_conv_row(ug, g_taps, g_bias, rb)
            cv = _grid_conv_row(uv, v_taps, v_bias, rb)
            act_scr[rb * GRID_W:(rb + 1) * GRID_W, gs] = ((cg * jax.nn.sigmoid(cg)) * cv).astype(BF16)

    x2 = x + m[5:6] * _bdot(act_scr[...], wdn_ref[...])
    ms = jnp.mean(x2 * x2, axis=-1, keepdims=True)
    o_ref[0] = (x2 * lax.rsqrt(ms + EPS)) * gf_ref[...]


def _ffn(x1, mod3, g_ffn, w_up, conv_w9, conv_b, w_down, g_final):
    bsz, length, _ = x1.shape
    tt = FFN_TILE
    rows_per_tile = tt // GRID_W
    n_rows = length // GRID_W
    halo = lambda fn: pl.BlockSpec((1, GRID_W, D_MODEL), fn)
    return pl.pallas_call(
        _ffn_kernel,
        grid=(bsz, length // tt),
        in_specs=[halo(lambda b, i: (b, jnp.maximum(i * rows_per_tile - 1, 0), 0)),
                  pl.BlockSpec((1, tt, D_MODEL), lambda b, i: (b, i, 0)),
                  halo(lambda b, i: (b, jnp.minimum((i + 1) * rows_per_tile, n_rows - 1), 0)),
                  pl.BlockSpec((1, N_MOD, D_MODEL), lambda b, i: (b, 0, 0)),
                  _const_spec(g_ffn.shape), _const_spec(w_up.shape), _const_spec(conv_w9.shape),
                  _const_spec(conv_b.shape), _const_spec(w_down.shape), _const_spec(g_final.shape)],
        out_specs=pl.BlockSpec((1, tt, D_MODEL), lambda b, i: (b, i, 0)),
        out_shape=jax.ShapeDtypeStruct(x1.shape, F32),
        scratch_shapes=[pltpu.VMEM((tt + 2 * GRID_W, D_MODEL), BF16),
                        pltpu.VMEM((tt, FFN_HIDDEN), BF16)],
        compiler_params=_cparams(("parallel", "parallel")),
        name="ffn",
    )(x1, x1, x1, mod3, g_ffn, w_up, conv_w9, conv_b, w_down, g_final)


def kernel(x, c, ctx, c_ctx, w_ada, b_ada, g_mix, w_in, s5_a_re, s5_a_im, s5_log_step, s5_b_re, s5_b_im, s5_c_re, s5_c_im, s5_d, s5_w_glu, s5_b_glu, sgu_ln_g, sgu_ln_b, sgu_w, sgu_b, w_proj_a, w_proj_b, b_gate, w_out, g_ffn, w_up, conv_w, conv_b, w_down, g_final):
    bsz = x.shape[0]
    row = lambda t: t.reshape(1, -1)

    ctx_row = bsz
    c_rows = jnp.zeros((SUBLANES, D_MODEL), F32).at[:bsz].set(c).at[ctx_row].set(c_ctx)
    mod3 = _ada_rows(c_rows, w_ada[0], b_ada[0]).reshape(SUBLANES, N_MOD, D_MODEL)

    w_in_b = w_in[0].astype(BF16)
    wsp = sgu_w[0].astype(BF16)
    wsp_pair = jnp.concatenate([wsp[0::2], wsp[1::2]], axis=-1)
    bsp_rows = jnp.repeat(sgu_b[0].T, SGU_GROUP_DIM, axis=1)

    u_pairs, ga, mb = _inproj(x, mod3, row(g_mix[0]), w_in_b, row(b_gate[0]), row(sgu_ln_g[0]),
                              row(sgu_ln_b[0]), wsp_pair, bsp_rows, w_proj_b[0].astype(BF16))
    uc_pairs = _ctxproj(ctx, mod3, ctx_row, row(g_mix[0]), w_in_b[:, :S5_WIDTH])

    w_state, w_y, coef = _s5_operators(
        s5_a_re[0], s5_a_im[0], s5_log_step[0], s5_b_re[0], s5_b_im[0], s5_c_re[0], s5_c_im[0],
        s5_d[0])
    y_pairs = _s5(u_pairs, uc_pairs, bsz, w_state, w_y, coef)

    x1 = _mix(y_pairs, x, ga, mb, mod3, s5_w_glu[0].astype(BF16), row(s5_b_glu[0]),
              w_proj_a[0].astype(BF16), w_out[0].astype(BF16))

    return _ffn(x1, mod3, row(g_ffn[0]), w_up[0].astype(BF16), conv_w[0].reshape(9, -1),
                row(conv_b[0]), w_down[0].astype(BF16), row(g_final))
```

```python
import math

import jax
import jax.numpy as jnp
from jax import lax
from jax.experimental import pallas as pl
from jax.experimental.pallas import tpu as pltpu

F32 = jnp.float32
BF16 = jnp.bfloat16

D_MODEL = 1024
GRID_W = 64
S5_WIDTH = 512
S5_GROUP = 16
S5_GROUPS = S5_WIDTH // S5_GROUP
S5_STATE = 64
SGU_WIDTH = 512
SGU_GROUPS = 8
SGU_GROUP_DIM = SGU_WIDTH // SGU_GROUPS
CHUNK = 128
FFN_HIDDEN = 2816
N_MOD = 6
EPS = 1e-6

LANES = 128
SUBLANES = 8
V7X_VMEM_LIMIT = 56 * 1024 * 1024

S5_CHUNK = 16
S5_PAIRS = S5_GROUPS // 2
PAIR_W = 2 * S5_CHUNK * S5_GROUP
S5_SLABS = S5_WIDTH // LANES
PAIRS_PER_SLAB = S5_PAIRS // S5_SLABS

TOK_TILE = 512
FFN_TILE = 512
FFN_CHUNK = 256


def _cparams(sem):
    return pltpu.CompilerParams(dimension_semantics=sem, vmem_limit_bytes=V7X_VMEM_LIMIT)


def _const_spec(shape):
    zeros = (0,) * len(shape)
    return pl.BlockSpec(shape, lambda *_: zeros, pipeline_mode=pl.Buffered(1))


def _gelu_tanh(x):
    c = math.sqrt(2.0 / math.pi)
    return 0.5 * x * (1.0 + jnp.tanh(c * (x + 0.044715 * (x * x * x))))


def _modulated_rms(x, g, shift, scale):
    ms = jnp.mean(x * x, axis=-1, keepdims=True)
    return (x * lax.rsqrt(ms + EPS)) * (g * (1.0 + scale)) + shift


def _bdot(a, b):
    return jnp.dot(a, b, preferred_element_type=F32)


def _ada_kernel(c_ref, w_ref, b_ref, o_ref):
    c = c_ref[...]
    cs = c * jax.nn.sigmoid(c)
    o_ref[...] = jnp.dot(cs, w_ref[...], preferred_element_type=F32,
                         precision=lax.Precision.HIGHEST) + b_ref[...]


def _ada_rows(c_rows, w_ada, b_ada):
    n = w_ada.shape[1]
    bn = 1536
    return pl.pallas_call(
        _ada_kernel,
        grid=(n // bn,),
        in_specs=[_const_spec(c_rows.shape),
                  pl.BlockSpec((D_MODEL, bn), lambda j: (0, j)),
                  pl.BlockSpec((1, bn), lambda j: (0, j))],
        out_specs=pl.BlockSpec((c_rows.shape[0], bn), lambda j: (0, j)),
        out_shape=jax.ShapeDtypeStruct((c_rows.shape[0], n), F32),
        compiler_params=_cparams(("arbitrary",)),
        name="ada",
    )(c_rows, w_ada, b_ada.reshape(1, n))


def _swap_slab_and_piece(slabs):
    lane = lax.broadcasted_iota(jnp.int32, (1, LANES), 1)
    out = list(slabs)
    for k in range(3):
        width = S5_GROUP << k
        keep = ((lane >> (4 + k)) & 1) == 0
        nxt = list(out)
        for a0 in range(len(out)):
            if a0 & (1 << k):
                continue
            a1 = a0 | (1 << k)
            nxt[a0] = jnp.where(keep, out[a0], pltpu.roll(out[a1], width, 1))
            nxt[a1] = jnp.where(keep, pltpu.roll(out[a0], LANES - width, 1), out[a1])
        out = nxt
    return out


def _pair_lane_block(slab_idx):
    t_hi, g = divmod(slab_idx, SUBLANES)
    pp, gg = divmod(g, 2)
    return pp, gg * 2 + t_hi


def _store_pair_tiles(u, u_scr, up_ref):
    cpt = u.shape[0] // S5_CHUNK
    for j in range(S5_SLABS):
        u_scr[j] = u[:, j * LANES:(j + 1) * LANES]
    for j in range(S5_SLABS):
        slabs = _swap_slab_and_piece(
            [u_scr[j, pl.ds(t, cpt, stride=S5_CHUNK), :] for t in range(S5_CHUNK)])
        for a in range(S5_CHUNK):
            pp, blk = _pair_lane_block(a)
            up_ref[j * PAIRS_PER_SLAB + pp, :, blk * LANES:(blk + 1) * LANES] = slabs[a].astype(BF16)


def _inproj_kernel(x_ref, mod_ref, g_ref, w_ref, bgate_ref, lng_ref, lnb_ref, wsp_ref, bsp_ref,
                   wpb_ref, u_ref, ga_ref, mb_ref, u_scr):
    m = mod_ref[0]
    h = _modulated_rms(x_ref[0], g_ref[...], m[0:1], m[1:2]).astype(BF16)
    tt = h.shape[0]

    _store_pair_tiles(_bdot(h, w_ref[:, 0:S5_WIDTH]), u_scr, u_ref)

    z0 = S5_WIDTH
    g0 = S5_WIDTH + 2 * SGU_WIDTH
    zv_pre = _bdot(h, w_ref[:, z0 + SGU_WIDTH:z0 + 2 * SGU_WIDTH])
    zu_pre = _bdot(h, w_ref[:, z0:z0 + SGU_WIDTH])
    ga_pre = _bdot(h, w_ref[:, g0:g0 + D_MODEL])
    gb_pre = _bdot(h, w_ref[:, g0 + D_MODEL:g0 + 2 * D_MODEL])

    zv = _gelu_tanh(zv_pre)
    mu = jnp.mean(zv, axis=-1, keepdims=True)
    zc = zv - mu
    var = jnp.mean(zc * zc, axis=-1, keepdims=True)
    v = ((zc * lax.rsqrt(var + EPS)) * lng_ref[...] + lnb_ref[...]).astype(BF16)
    zu = _gelu_tanh(zu_pre)

    lane = lax.broadcasted_iota(jnp.int32, (CHUNK, LANES), 1)
    zero = jnp.zeros((CHUNK, LANES), BF16)
    rows = []
    for c in range(tt // CHUNK):
        tiles = []
        for j in range(SGU_WIDTH // LANES):
            vj = v[c * CHUNK:(c + 1) * CHUNK, j * LANES:(j + 1) * LANES]
            rhs = jnp.concatenate([jnp.where(lane < SGU_GROUP_DIM, vj, zero),
                                   jnp.where(lane >= SGU_GROUP_DIM, vj, zero)], axis=0)
            tiles.append(_bdot(wsp_ref[j], rhs))
        rows.append(jnp.concatenate(tiles, axis=1) + bsp_ref[...])
    s = jnp.concatenate(rows, axis=0)
    yb = (zu * s).astype(BF16)
    pb = _bdot(yb, wpb_ref[...])

    ga = jax.nn.sigmoid(ga_pre + bgate_ref[:, 0:D_MODEL])
    gb = jax.nn.sigmoid(gb_pre + bgate_ref[:, D_MODEL:2 * D_MODEL])
    ga_ref[0] = ga.astype(BF16)
    mb_ref[0] = (gb * pb).astype(BF16)


def _inproj(x, mod3, g_mix, w_in, b_gate, ln_g, ln_b, wsp_pair, bsp_rows, w_proj_b):
    bsz, length, _ = x.shape
    tt = TOK_TILE
    nt = length // tt
    tok = lambda w: pl.BlockSpec((1, tt, w), lambda b, i: (b, i, 0))
    return pl.pallas_call(
        _inproj_kernel,
        grid=(bsz, nt),
        in_specs=[tok(D_MODEL),
                  pl.BlockSpec((1, N_MOD, D_MODEL), lambda b, i: (b, 0, 0)),
                  _const_spec(g_mix.shape), _const_spec(w_in.shape), _const_spec(b_gate.shape),
                  _const_spec(ln_g.shape), _const_spec(ln_b.shape), _const_spec(wsp_pair.shape),
                  _const_spec(bsp_rows.shape), _const_spec(w_proj_b.shape)],
        out_specs=[pl.BlockSpec((S5_PAIRS, tt // S5_CHUNK, PAIR_W), lambda b, i: (0, b * nt + i, 0)),
                   tok(D_MODEL), tok(D_MODEL)],
        out_shape=[jax.ShapeDtypeStruct((S5_PAIRS, bsz * length // S5_CHUNK, PAIR_W), BF16),
                   jax.ShapeDtypeStruct((bsz, length, D_MODEL), BF16),
                   jax.ShapeDtypeStruct((bsz, length, D_MODEL), BF16)],
        scratch_shapes=[pltpu.VMEM((S5_SLABS, tt, LANES), F32)],
        compiler_params=_cparams(("parallel", "parallel")),
        name="inproj",
    )(x, mod3, g_mix, w_in, b_gate, ln_g, ln_b, wsp_pair, bsp_rows, w_proj_b)


def _ctxproj_kernel(x_ref, mod_ref, g_ref, w_ref, u_ref, u_scr):
    m = mod_ref[0]
    h = _modulated_rms(x_ref[0], g_ref[...], m[0:1], m[1:2]).astype(BF16)
    _store_pair_tiles(_bdot(h, w_ref[...]), u_scr, u_ref)


def _ctxproj(ctx, mod3, ctx_row, g_mix, w_in_u):
    bsz, length, _ = ctx.shape
    return pl.pallas_call(
        _ctxproj_kernel,
        grid=(bsz,),
        in_specs=[pl.BlockSpec((1, length, D_MODEL), lambda b: (b, 0, 0)),
                  pl.BlockSpec((1, N_MOD, D_MODEL), lambda b: (ctx_row, 0, 0)),
                  _const_spec(g_mix.shape), _const_spec(w_in_u.shape)],
        out_specs=pl.BlockSpec((S5_PAIRS, length // S5_CHUNK, PAIR_W), lambda b: (0, b, 0)),
        out_shape=jax.ShapeDtypeStruct((S5_PAIRS, bsz * length // S5_CHUNK, PAIR_W), BF16),
        scratch_shapes=[pltpu.VMEM((S5_SLABS, length, LANES), F32)],
        compiler_params=_cparams(("parallel",)),
        name="ctxproj",
    )(ctx, mod3, g_mix, w_in_u)


def _cmul(ar, ai, br, bi):
    return ar * br - ai * bi, ar * bi + ai * br


def _s5_ops_kernel(a_re_ref, a_im_ref, ls_ref, bt_re_ref, bt_im_ref, c_re_ref, c_im_ref, d_ref,
                   wst_ref, wy_ref, coef_ref):
    hp = lax.Precision.HIGHEST
    n_tok = S5_CHUNK
    blk = n_tok * S5_GROUP
    tok_of_lane = lax.broadcasted_iota(jnp.int32, (1, blk), 1) // S5_GROUP
    row8 = lax.broadcasted_iota(jnp.int32, (SUBLANES, 1), 0)
    diag = (lax.broadcasted_iota(jnp.int32, (blk, blk), 0)
            == lax.broadcasted_iota(jnp.int32, (blk, blk), 1))
    contract_n = (((1,), (1,)), ((), ()))

    for gg in range(2):
        rows_g = slice(gg * blk, (gg + 1) * blk)
        cols_g = slice(gg * blk, (gg + 1) * blk)
        cols_other = slice((1 - gg) * blk, (2 - gg) * blk)
        w_intra = jnp.where(diag, d_ref[0, gg], 0.0)
        for d in range(2):
            forward = d == 0
            a_re = a_re_ref[d, 0, gg]
            a_im = a_im_ref[d, 0, gg]
            dt = jnp.exp(ls_ref[d, 0, gg])
            mag = jnp.exp(a_re * dt)
            lam = (mag * jnp.cos(a_im * dt), mag * jnp.sin(a_im * dt))
            den = a_re * a_re + a_im * a_im
            p, q = lam[0] - 1.0, lam[1]
            k = ((p * a_re + q * a_im) / den, (q * a_re - p * a_im) / den)
            bbar = _cmul(*k, bt_re_ref[d, 0, gg], bt_im_ref[d, 0, gg])
            c = (c_re_ref[d, 0, gg], c_im_ref[d, 0, gg])
            pw = [(jnp.ones_like(p), jnp.zeros_like(p))]
            for _ in range(n_tok):
                pw.append(_cmul(*pw[-1], *lam))

            expo = [n_tok - 1 - i if forward else i for i in range(n_tok)]
            cols = [_cmul(*pw[e], *bbar) for e in expo]
            m_re = jnp.concatenate([t[0] for t in cols], axis=0)
            m_im = jnp.concatenate([t[1] for t in cols], axis=0)
            for comp, m in ((2 * d, m_re), (2 * d + 1, m_im)):
                lo = comp * LANES + gg * S5_STATE
                other = comp * LANES + (1 - gg) * S5_STATE
                wst_ref[0, rows_g, lo:lo + S5_STATE] = m.astype(BF16)
                wst_ref[0, rows_g, other:other + S5_STATE] = jnp.zeros((blk, S5_STATE), BF16)

            c_rep = [jnp.concatenate([t] * n_tok, axis=0) for t in c]
            kr = (lax.dot_general(m_re, c_rep[0], contract_n, precision=hp,
                                  preferred_element_type=F32)
                  - lax.dot_general(m_im, c_rep[1], contract_n, precision=hp,
                                    preferred_element_type=F32))
            toep = jnp.zeros((blk, blk), F32)
            for j in range(n_tok):
                s = S5_GROUP * (n_tok - 1 - j if forward else j)
                if s == 0:
                    moved = kr
                elif forward:
                    moved = jnp.concatenate([kr[s:], jnp.zeros((s, blk), F32)], axis=0)
                else:
                    moved = jnp.concatenate([jnp.zeros((s, blk), F32), kr[:blk - s]], axis=0)
                toep = jnp.where(tok_of_lane == j, moved, toep)
            w_intra = w_intra + toep

            expo = [j + 1 if forward else n_tok - j for j in range(n_tok)]
            rows = [_cmul(*pw[e], *c) for e in expo]
            r_re = jnp.concatenate([t[0] for t in rows], axis=0).T
            r_im = -jnp.concatenate([t[1] for t in rows], axis=0).T
            for comp, r in ((2 * d, r_re), (2 * d + 1, r_im)):
                r0 = PAIR_W + comp * LANES + gg * S5_STATE
                wy_ref[0, r0:r0 + S5_STATE, cols_g] = r.astype(BF16)
                wy_ref[0, r0:r0 + S5_STATE, cols_other] = jnp.zeros((S5_STATE, blk), BF16)

            lp = [(jnp.ones_like(p), jnp.zeros_like(p)), pw[n_tok]]
            for _ in range(SUBLANES - 1):
                lp.append(_cmul(*lp[-1], *pw[n_tok]))
            lanes = slice(d * LANES + gg * S5_STATE, d * LANES + (gg + 1) * S5_STATE)
            for kk, s in enumerate((1, 2, 4)):
                valid = (row8 >= s) if forward else (row8 + s < SUBLANES)
                coef_ref[0, 2 * kk, :, lanes] = jnp.where(valid, lp[s][0], 0.0)
                coef_ref[0, 2 * kk + 1, :, lanes] = jnp.where(valid, lp[s][1], 0.0)
            q_re = jnp.zeros((SUBLANES, S5_STATE), F32)
            q_im = jnp.zeros((SUBLANES, S5_STATE), F32)
            for r in range(SUBLANES):
                e = r + 1 if forward else SUBLANES - r
                q_re = jnp.where(row8 == r, lp[e][0], q_re)
                q_im = jnp.where(row8 == r, lp[e][1], q_im)
            coef_ref[0, 6, :, lanes] = q_re
            coef_ref[0, 7, :, lanes] = q_im

        wy_ref[0, rows_g, cols_g] = w_intra.astype(BF16)
        wy_ref[0, rows_g, cols_other] = jnp.zeros((blk, blk), BF16)


def _s5_operators(a_re, a_im, log_step, b_re, b_im, c_re, c_im, d_skip):
    grp = lambda t, *tail: t.reshape(2, S5_PAIRS, 2, *tail)
    row = lambda t: grp(t, 1, t.shape[-1])
    bt_re = grp(jnp.swapaxes(b_re, -1, -2), S5_GROUP, S5_STATE)
    bt_im = grp(jnp.swapaxes(b_im, -1, -2), S5_GROUP, S5_STATE)
    d_tiled = jnp.tile(d_skip.reshape(S5_PAIRS, 2, 1, S5_GROUP), (1, 1, 1, S5_CHUNK))
    per_pair = lambda *tail: pl.BlockSpec((2, 1, 2) + tail, lambda p: (0, p, 0) + (0,) * len(tail))
    out = lambda *tail: pl.BlockSpec((1,) + tail, lambda p: (p,) + (0,) * len(tail))
    return pl.pallas_call(
        _s5_ops_kernel,
        grid=(S5_PAIRS,),
        in_specs=[per_pair(1, S5_STATE), per_pair(1, S5_STATE), per_pair(1, 1),
                  per_pair(S5_GROUP, S5_STATE), per_pair(S5_GROUP, S5_STATE),
                  per_pair(S5_GROUP, S5_STATE), per_pair(S5_GROUP, S5_STATE),
                  pl.BlockSpec((1, 2, 1, S5_CHUNK * S5_GROUP), lambda p: (p, 0, 0, 0))],
        out_specs=[out(PAIR_W, PAIR_W), out(2 * PAIR_W, PAIR_W), out(8, SUBLANES, 2 * LANES)],
        out_shape=[jax.ShapeDtypeStruct((S5_PAIRS, PAIR_W, PAIR_W), BF16),
                   jax.ShapeDtypeStruct((S5_PAIRS, 2 * PAIR_W, PAIR_W), BF16),
                   jax.ShapeDtypeStruct((S5_PAIRS, 8, SUBLANES, 2 * LANES), F32)],
        compiler_params=_cparams(("parallel",)),
        name="s5_ops",
    )(row(a_re), row(a_im), grp(log_step, 1, 1), bt_re, bt_im,
      grp(c_re, S5_GROUP, S5_STATE), grp(c_im, S5_GROUP, S5_STATE), d_tiled)


def _s5_kernel(u_ref, uc_ref, wst_ref, wy_ref, coef_ref, y_ref, s_scr, sin_scr):
    n_main = u_ref.shape[1]
    n_ctx = uc_ref.shape[1]
    n_rows = n_ctx + n_main + n_ctx

    for pp in range(PAIRS_PER_SLAB):
        s_ctx = _bdot(uc_ref[pp], wst_ref[pp])
        s_scr[pp, 0:n_ctx] = s_ctx
        s_scr[pp, n_ctx:n_ctx + n_main] = _bdot(u_ref[pp], wst_ref[pp])
        s_scr[pp, n_ctx + n_main:] = s_ctx

    row = lax.broadcasted_iota(jnp.int32, (SUBLANES, LANES), 0)
    n_groups = n_rows // SUBLANES
    n_iter = (n_ctx + n_main) // SUBLANES

    def group(pp, base, c_re, c_im, lane0, forward):
        rows = pl.ds(base, SUBLANES)
        re_l = slice(lane0, lane0 + LANES)
        im_l = slice(lane0 + LANES, lane0 + 2 * LANES)
        cf = slice(0, LANES) if forward else slice(LANES, 2 * LANES)
        xr = s_scr[pp, rows, re_l]
        xi = s_scr[pp, rows, im_l]
        for k, sh in enumerate((1, 2, 4)):
            ar = coef_ref[pp, 2 * k, :, cf]
            ai = coef_ref[pp, 2 * k + 1, :, cf]
            amt = sh if forward else SUBLANES - sh
            sr = pltpu.roll(xr, amt, 0)
            si = pltpu.roll(xi, amt, 0)
            xr, xi = xr + (ar * sr - ai * si), xi + (ar * si + ai * sr)
        qr = coef_ref[pp, 6, :, cf]
        qi = coef_ref[pp, 7, :, cf]
        hr = xr + (qr * c_re - qi * c_im)
        hi = xi + (qr * c_im + qi * c_re)
        edge = 0 if forward else SUBLANES - 1
        amt = 1 if forward else SUBLANES - 1
        sin_scr[pp, rows, re_l] = jnp.where(row == edge, c_re, pltpu.roll(hr, amt, 0))
        sin_scr[pp, rows, im_l] = jnp.where(row == edge, c_im, pltpu.roll(hi, amt, 0))
        last = SUBLANES - 1 if forward else 0
        return (jnp.broadcast_to(hr[last:last + 1], (SUBLANES, LANES)),
                jnp.broadcast_to(hi[last:last + 1], (SUBLANES, LANES)))

    def body(g, carry):
        new = []
        for pp in range(PAIRS_PER_SLAB):
            f_re, f_im, b_re, b_im = carry[4 * pp:4 * pp + 4]
            f_base = pl.multiple_of(g * SUBLANES, SUBLANES)
            b_base = pl.multiple_of((n_groups - 1 - g) * SUBLANES, SUBLANES)
            f_re, f_im = group(pp, f_base, f_re, f_im, 0, True)
            b_re, b_im = group(pp, b_base, b_re, b_im, 2 * LANES, False)
            new += [f_re, f_im, b_re, b_im]
        return tuple(new)

    zero = jnp.zeros((SUBLANES, LANES), F32)
    lax.fori_loop(0, n_iter, body, (zero,) * (4 * PAIRS_PER_SLAB))

    main_rows = slice(n_ctx, n_ctx + n_main)
    for pp in range(PAIRS_PER_SLAB):
        y_ref[pp] = (_bdot(u_ref[pp], wy_ref[pp, 0:PAIR_W])
                     + _bdot(sin_scr[pp, main_rows, :].astype(BF16), wy_ref[pp, PAIR_W:2 * PAIR_W]))


def _s5(u_pairs, uc_pairs, bsz, w_state, w_y, coef):
    n_main = u_pairs.shape[1] // bsz
    n_ctx = uc_pairs.shape[1] // bsz
    n_rows = n_main + 2 * n_ctx
    pps = PAIRS_PER_SLAB
    rows = lambda n: pl.BlockSpec((pps, n, PAIR_W), lambda j, b: (j, b, 0))
    return pl.pallas_call(
        _s5_kernel,
        grid=(S5_SLABS, bsz),
        in_specs=[rows(n_main), rows(n_ctx),
                  pl.BlockSpec((pps, PAIR_W, PAIR_W), lambda j, b: (j, 0, 0)),
                  pl.BlockSpec((pps, 2 * PAIR_W, PAIR_W), lambda j, b: (j, 0, 0)),
                  pl.BlockSpec((pps, 8, SUBLANES, 2 * LANES), lambda j, b: (j, 0, 0, 0))],
        out_specs=rows(n_main),
        out_shape=jax.ShapeDtypeStruct(u_pairs.shape, F32),
        scratch_shapes=[pltpu.VMEM((pps, n_rows, PAIR_W), F32),
                        pltpu.VMEM((pps, n_rows, PAIR_W), F32)],
        compiler_params=_cparams(("arbitrary", "arbitrary")),
        name="s5",
    )(u_pairs, uc_pairs, w_state, w_y, coef)


def _mix_kernel(y_ref, x_ref, ga_ref, mb_ref, mod_ref, wglu_ref, bglu_ref, wpa_ref, wout_ref, o_ref,
                y_scr):
    m = mod_ref[0]
    cpt = y_ref.shape[1]
    for j in range(S5_SLABS):
        slabs = []
        for a in range(S5_CHUNK):
            pp, blk = _pair_lane_block(a)
            slabs.append(y_ref[j * PAIRS_PER_SLAB + pp, :, blk * LANES:(blk + 1) * LANES])
        slabs = _swap_slab_and_piece(slabs)
        for t in range(S5_CHUNK):
            y_scr[j, pl.ds(t, cpt, stride=S5_CHUNK), :] = slabs[t]
    ya = _gelu_tanh(jnp.concatenate([y_scr[j] for j in range(S5_SLABS)], axis=1))
    glu = ya * jax.nn.sigmoid(_bdot(ya.astype(BF16), wglu_ref[...]) + bglu_ref[...])
    pa = _bdot(glu.astype(BF16), wpa_ref[...])
    merged = (ga_ref[0].astype(F32) * pa + mb_ref[0].astype(F32)).astype(BF16)
    o_ref[0] = x_ref[0] + m[2:3] * _bdot(merged, wout_ref[...])


def _mix(y_pairs, x, ga, mb, mod3, w_glu, b_glu, w_proj_a, w_out):
    bsz, length, _ = x.shape
    tt = TOK_TILE
    nt = length // tt
    tok = lambda w: pl.BlockSpec((1, tt, w), lambda b, i: (b, i, 0))
    return pl.pallas_call(
        _mix_kernel,
        grid=(bsz, nt),
        in_specs=[pl.BlockSpec((S5_PAIRS, tt // S5_CHUNK, PAIR_W), lambda b, i: (0, b * nt + i, 0)),
                  tok(D_MODEL), tok(D_MODEL), tok(D_MODEL),
                  pl.BlockSpec((1, N_MOD, D_MODEL), lambda b, i: (b, 0, 0)),
                  _const_spec(w_glu.shape), _const_spec(b_glu.shape),
                  _const_spec(w_proj_a.shape), _const_spec(w_out.shape)],
        out_specs=tok(D_MODEL),
        out_shape=jax.ShapeDtypeStruct(x.shape, F32),
        scratch_shapes=[pltpu.VMEM((S5_SLABS, tt, LANES), F32)],
        compiler_params=_cparams(("parallel", "parallel")),
        name="mix",
    )(y_pairs, x, ga, mb, mod3, w_glu, b_glu, w_proj_a, w_out)


def _conv_taps(cw_ref, cb_ref, cols):
    width = cols.stop - cols.start
    taps = [jnp.broadcast_to(cw_ref[k:k + 1, cols].astype(BF16), (GRID_W, width)) for k in range(9)]
    return taps, jnp.broadcast_to(cb_ref[:, cols].astype(BF16), (GRID_W, width))


def _token_neighbours(u):
    n, ch = u.shape
    shape4 = (n // GRID_W, GRID_W // SUBLANES, SUBLANES, ch)
    sub = lax.broadcasted_iota(jnp.int32, (1, 1, SUBLANES, 1), 2)
    zero = jnp.zeros((shape4[0], 1, SUBLANES, ch), F32)
    u4 = u.reshape(shape4)
    down = pltpu.roll(u4, 1, 2)
    left = jnp.where(sub == 0, jnp.concatenate([zero, down[:, :-1]], axis=1), down)
    up = pltpu.roll(u4, SUBLANES - 1, 2)
    right = jnp.where(sub == SUBLANES - 1, jnp.concatenate([up[:, 1:], zero], axis=1), up)
    return left.reshape(n, ch).astype(BF16), u.astype(BF16), right.reshape(n, ch).astype(BF16)


def _grid_conv_row(nbrs, taps, bias, rb):
    lo = rb * GRID_W
    part = []
    for dr in range(3):
        rows = slice(lo + dr * GRID_W, lo + (dr + 1) * GRID_W)
        left, here, right = (nbrs[dc][rows] * taps[3 * dr + dc] for dc in range(3))
        part.append((left + here) + right)
    return (part[0] + part[1]) + (part[2] + bias)


def _ffn_kernel(top_ref, x_ref, bot_ref, mod_ref, g_ref, wup_ref, cw_ref, cb_ref, wdn_ref, gf_ref,
                o_ref, h_scr, act_scr):
    i = pl.program_id(1)
    n_i = pl.num_programs(1)
    m = mod_ref[0]
    g = g_ref[...]
    tt = x_ref.shape[1]
    x = x_ref[0]

    def norm(t):
        return _modulated_rms(t, g, m[3:4], m[4:5])

    h_scr[0:GRID_W] = jnp.where(i > 0, norm(top_ref[0]), 0.0).astype(BF16)
    h_scr[GRID_W:GRID_W + tt] = norm(x).astype(BF16)
    h_scr[GRID_W + tt:] = jnp.where(i < n_i - 1, norm(bot_ref[0]), 0.0).astype(BF16)

    h = h_scr[...]

    fc = FFN_CHUNK
    for j in range(FFN_HIDDEN // fc):
        gs = slice(j * fc, (j + 1) * fc)
        vs = slice(FFN_HIDDEN + j * fc, FFN_HIDDEN + (j + 1) * fc)
        ug = _token_neighbours(_bdot(h, wup_ref[:, gs]))
        uv = _token_neighbours(_bdot(h, wup_ref[:, vs]))
        g_taps, g_bias = _conv_taps(cw_ref, cb_ref, gs)
        v_taps, v_bias = _conv_taps(cw_ref, cb_ref, vs)
        for rb in range(tt // GRID_W):
            cg = _grid_conv_row(ug, g_taps, g_bias, rb)
            cv = _grid_conv_row(uv, v_taps, v_bias, rb)
            act_scr[rb * GRID_W:(rb + 1) * GRID_W, gs] = (cg * jax.nn.sigmoid(cg)) * cv

    x2 = x + m[5:6] * _bdot(act_scr[...], wdn_ref[...])
    ms = jnp.mean(x2 * x2, axis=-1, keepdims=True)
    o_ref[0] = (x2 * lax.rsqrt(ms + EPS)) * gf_ref[...]


def _ffn(x1, mod3, g_ffn, w_up, conv_w9, conv_b, w_down, g_final):
    bsz, length, _ = x1.shape
    tt = FFN_TILE
    rows_per_tile = tt // GRID_W
    n_rows = length // GRID_W
    halo = lambda fn: pl.BlockSpec((1, GRID_W, D_MODEL), fn)
    return pl.pallas_call(
        _ffn_kernel,
        grid=(bsz, length // tt),
        in_specs=[halo(lambda b, i: (b, jnp.maximum(i * rows_per_tile - 1, 0), 0)),
                  pl.BlockSpec((1, tt, D_MODEL), lambda b, i: (b, i, 0)),
                  halo(lambda b, i: (b, jnp.minimum((i + 1) * rows_per_tile, n_rows - 1), 0)),
                  pl.BlockSpec((1, N_MOD, D_MODEL), lambda b, i: (b, 0, 0)),
                  _const_spec(g_ffn.shape), _const_spec(w_up.shape), _const_spec(conv_w9.shape),
                  _const_spec(conv_b.shape), _const_spec(w_down.shape), _const_spec(g_final.shape)],
        out_specs=pl.BlockSpec((1, tt, D_MODEL), lambda b, i: (b, i, 0)),
        out_shape=jax.ShapeDtypeStruct(x1.shape, F32),
        scratch_shapes=[pltpu.VMEM((tt + 2 * GRID_W, D_MODEL), BF16),
                        pltpu.VMEM((tt, FFN_HIDDEN), BF16)],
        compiler_params=_cparams(("parallel", "parallel")),
        name="ffn",
    )(x1, x1, x1, mod3, g_ffn, w_up, conv_w9, conv_b, w_down, g_final)


def kernel(x, c, ctx, c_ctx, w_ada, b_ada, g_mix, w_in, s5_a_re, s5_a_im, s5_log_step, s5_b_re, s5_b_im, s5_c_re, s5_c_im, s5_d, s5_w_glu, s5_b_glu, sgu_ln_g, sgu_ln_b, sgu_w, sgu_b, w_proj_a, w_proj_b, b_gate, w_out, g_ffn, w_up, conv_w, conv_b, w_down, g_final):
    bsz = x.shape[0]
    row = lambda t: t.reshape(1, -1)

    ctx_row = bsz
    c_rows = jnp.zeros((SUBLANES, D_MODEL), F32).at[:bsz].set(c).at[ctx_row].set(c_ctx)
    mod3 = _ada_rows(c_rows, w_ada[0], b_ada[0]).reshape(SUBLANES, N_MOD, D_MODEL)

    w_in_b = w_in[0].astype(BF16)
    wsp = sgu_w[0].astype(BF16)
    wsp_pair = jnp.concatenate([wsp[0::2], wsp[1::2]], axis=-1)
    bsp_rows = jnp.repeat(sgu_b[0].T, SGU_GROUP_DIM, axis=1)

    u_pairs, ga, mb = _inproj(x, mod3, row(g_mix[0]), w_in_b, row(b_gate[0]), row(sgu_ln_g[0]),
                              row(sgu_ln_b[0]), wsp_pair, bsp_rows, w_proj_b[0].astype(BF16))
    uc_pairs = _ctxproj(ctx, mod3, ctx_row, row(g_mix[0]), w_in_b[:, :S5_WIDTH])

    w_state, w_y, coef = _s5_operators(
        s5_a_re[0], s5_a_im[0], s5_log_step[0], s5_b_re[0], s5_b_im[0], s5_c_re[0], s5_c_im[0],
        s5_d[0])
    y_pairs = _s5(u_pairs, uc_pairs, bsz, w_state, w_y, coef)

    x1 = _mix(y_pairs, x, ga, mb, mod3, s5_w_glu[0].astype(BF16), row(s5_b_glu[0]),
              w_proj_a[0].astype(BF16), w_out[0].astype(BF16))

    return _ffn(x1, mod3, row(g_ffn[0]), w_up[0].astype(BF16), conv_w[0].reshape(9, -1),
                row(conv_b[0]), w_down[0].astype(BF16), row(g_final))
```

```python
import math

import jax
import jax.numpy as jnp
from jax import lax
from jax.experimental import pallas as pl
from jax.experimental.pallas import tpu as pltpu

F32 = jnp.float32
BF16 = jnp.bfloat16

D_MODEL = 1024
GRID_W = 64
S5_WIDTH = 512
S5_GROUP = 16
S5_GROUPS = S5_WIDTH // S5_GROUP
S5_STATE = 64
SGU_WIDTH = 512
SGU_GROUPS = 8
SGU_GROUP_DIM = SGU_WIDTH // SGU_GROUPS
CHUNK = 128
FFN_HIDDEN = 2816
N_MOD = 6
EPS = 1e-6

LANES = 128
SUBLANES = 8
V7X_VMEM_LIMIT = 56 * 1024 * 1024

S5_CHUNK = 16
S5_PAIRS = S5_GROUPS // 2
PAIR_W = 2 * S5_CHUNK * S5_GROUP
S5_SLABS = S5_WIDTH // LANES
PAIRS_PER_SLAB = S5_PAIRS // S5_SLABS

TOK_TILE = 512
ROW_GROUPS = 2
FFN_TILE = 1024
FFN_CHUNK = 256


def _cparams(sem):
    return pltpu.CompilerParams(dimension_semantics=sem, vmem_limit_bytes=V7X_VMEM_LIMIT)


def _const_spec(shape):
    zeros = (0,) * len(shape)
    return pl.BlockSpec(shape, lambda *_: zeros, pipeline_mode=pl.Buffered(1))


def _gelu_tanh(x):
    c = math.sqrt(2.0 / math.pi)
    return 0.5 * x * (1.0 + jnp.tanh(c * (x + 0.044715 * (x * x * x))))


def _modulated_rms(x, g, shift, scale):
    ms = jnp.mean(x * x, axis=-1, keepdims=True)
    return (x * lax.rsqrt(ms + EPS)) * (g * (1.0 + scale)) + shift


def _bdot(a, b):
    return jnp.dot(a, b, preferred_element_type=F32)


def _ada_kernel(c_ref, w_ref, b_ref, o_ref):
    c = c_ref[...]
    cs = c * jax.nn.sigmoid(c)
    o_ref[...] = jnp.dot(cs, w_ref[...], preferred_element_type=F32,
                         precision=lax.Precision.HIGHEST) + b_ref[...]


def _ada_rows(c_rows, w_ada, b_ada):
    n = w_ada.shape[1]
    bn = 1536
    return pl.pallas_call(
        _ada_kernel,
        grid=(n // bn,),
        in_specs=[_const_spec(c_rows.shape),
                  pl.BlockSpec((D_MODEL, bn), lambda j: (0, j)),
                  pl.BlockSpec((1, bn), lambda j: (0, j))],
        out_specs=pl.BlockSpec((c_rows.shape[0], bn), lambda j: (0, j)),
        out_shape=jax.ShapeDtypeStruct((c_rows.shape[0], n), F32),
        compiler_params=_cparams(("arbitrary",)),
        name="ada",
    )(c_rows, w_ada, b_ada.reshape(1, n))


def _swap_slab_and_piece(slabs):
    lane = lax.broadcasted_iota(jnp.int32, (1, LANES), 1)
    out = list(slabs)
    for k in range(3):
        width = S5_GROUP << k
        keep = ((lane >> (4 + k)) & 1) == 0
        nxt = list(out)
        for a0 in range(len(out)):
            if a0 & (1 << k):
                continue
            a1 = a0 | (1 << k)
            nxt[a0] = jnp.where(keep, out[a0], pltpu.roll(out[a1], width, 1))
            nxt[a1] = jnp.where(keep, pltpu.roll(out[a0], LANES - width, 1), out[a1])
        out = nxt
    return out


def _pair_lane_block(slab_idx):
    t_hi, g = divmod(slab_idx, SUBLANES)
    pp, gg = divmod(g, 2)
    return pp, gg * 2 + t_hi


def _store_pair_tiles(u, u_scr, up_ref):
    cpt = u.shape[0] // S5_CHUNK
    for j in range(S5_SLABS):
        u_scr[j] = u[:, j * LANES:(j + 1) * LANES]
    for j in range(S5_SLABS):
        slabs = _swap_slab_and_piece(
            [u_scr[j, pl.ds(t, cpt, stride=S5_CHUNK), :] for t in range(S5_CHUNK)])
        for a in range(S5_CHUNK):
            pp, blk = _pair_lane_block(a)
            up_ref[j * PAIRS_PER_SLAB + pp, :, blk * LANES:(blk + 1) * LANES] = slabs[a].astype(BF16)


def _inproj_kernel(x_ref, mod_ref, g_ref, w_ref, bgate_ref, lng_ref, lnb_ref, wsp_ref, bsp_ref,
                   wpb_ref, u_ref, ga_ref, mb_ref, u_scr):
    m = mod_ref[0]
    h = _modulated_rms(x_ref[0], g_ref[...], m[0:1], m[1:2]).astype(BF16)
    tt = h.shape[0]

    _store_pair_tiles(_bdot(h, w_ref[:, 0:S5_WIDTH]), u_scr, u_ref)

    z0 = S5_WIDTH
    g0 = S5_WIDTH + 2 * SGU_WIDTH
    zv_pre = _bdot(h, w_ref[:, z0 + SGU_WIDTH:z0 + 2 * SGU_WIDTH])
    zu_pre = _bdot(h, w_ref[:, z0:z0 + SGU_WIDTH])
    ga_pre = _bdot(h, w_ref[:, g0:g0 + D_MODEL])
    gb_pre = _bdot(h, w_ref[:, g0 + D_MODEL:g0 + 2 * D_MODEL])

    zv = _gelu_tanh(zv_pre)
    mu = jnp.mean(zv, axis=-1, keepdims=True)
    zc = zv - mu
    var = jnp.mean(zc * zc, axis=-1, keepdims=True)
    v = ((zc * lax.rsqrt(var + EPS)) * lng_ref[...] + lnb_ref[...]).astype(BF16)
    zu = _gelu_tanh(zu_pre)

    lane = lax.broadcasted_iota(jnp.int32, (CHUNK, LANES), 1)
    zero = jnp.zeros((CHUNK, LANES), BF16)
    rows = []
    for c in range(tt // CHUNK):
        tiles = []
        for j in range(SGU_WIDTH // LANES):
            vj = v[c * CHUNK:(c + 1) * CHUNK, j * LANES:(j + 1) * LANES]
            rhs = jnp.concatenate([jnp.where(lane < SGU_GROUP_DIM, vj, zero),
                                   jnp.where(lane >= SGU_GROUP_DIM, vj, zero)], axis=0)
            tiles.append(_bdot(wsp_ref[j], rhs))
        rows.append(jnp.concatenate(tiles, axis=1) + bsp_ref[...])
    s = jnp.concatenate(rows, axis=0)
    yb = (zu * s).astype(BF16)
    pb = _bdot(yb, wpb_ref[...])

    ga = jax.nn.sigmoid(ga_pre + bgate_ref[:, 0:D_MODEL])
    gb = jax.nn.sigmoid(gb_pre + bgate_ref[:, D_MODEL:2 * D_MODEL])
    ga_ref[0] = ga.astype(BF16)
    mb_ref[0] = (gb * pb).astype(BF16)


def _inproj(x, mod3, g_mix, w_in, b_gate, ln_g, ln_b, wsp_pair, bsp_rows, w_proj_b):
    bsz, length, _ = x.shape
    tt = TOK_TILE
    nt = length // tt
    tok = lambda w: pl.BlockSpec((1, tt, w), lambda b, i: (b, i, 0))
    return pl.pallas_call(
        _inproj_kernel,
        grid=(bsz, nt),
        in_specs=[tok(D_MODEL),
                  pl.BlockSpec((1, N_MOD, D_MODEL), lambda b, i: (b, 0, 0)),
                  _const_spec(g_mix.shape), _const_spec(w_in.shape), _const_spec(b_gate.shape),
                  _const_spec(ln_g.shape), _const_spec(ln_b.shape), _const_spec(wsp_pair.shape),
                  _const_spec(bsp_rows.shape), _const_spec(w_proj_b.shape)],
        out_specs=[pl.BlockSpec((S5_PAIRS, tt // S5_CHUNK, PAIR_W), lambda b, i: (0, b * nt + i, 0)),
                   tok(D_MODEL), tok(D_MODEL)],
        out_shape=[jax.ShapeDtypeStruct((S5_PAIRS, bsz * length // S5_CHUNK, PAIR_W), BF16),
                   jax.ShapeDtypeStruct((bsz, length, D_MODEL), BF16),
                   jax.ShapeDtypeStruct((bsz, length, D_MODEL), BF16)],
        scratch_shapes=[pltpu.VMEM((S5_SLABS, tt, LANES), F32)],
        compiler_params=_cparams(("parallel", "parallel")),
        name="inproj",
    )(x, mod3, g_mix, w_in, b_gate, ln_g, ln_b, wsp_pair, bsp_rows, w_proj_b)


def _ctxproj_kernel(x_ref, mod_ref, g_ref, w_ref, u_ref, u_scr):
    m = mod_ref[0]
    h = _modulated_rms(x_ref[0], g_ref[...], m[0:1], m[1:2]).astype(BF16)
    _store_pair_tiles(_bdot(h, w_ref[...]), u_scr, u_ref)


def _ctxproj(ctx, mod3, ctx_row, g_mix, w_in_u):
    bsz, length, _ = ctx.shape
    return pl.pallas_call(
        _ctxproj_kernel,
        grid=(bsz,),
        in_specs=[pl.BlockSpec((1, length, D_MODEL), lambda b: (b, 0, 0)),
                  pl.BlockSpec((1, N_MOD, D_MODEL), lambda b: (ctx_row, 0, 0)),
                  _const_spec(g_mix.shape), _const_spec(w_in_u.shape)],
        out_specs=pl.BlockSpec((S5_PAIRS, length // S5_CHUNK, PAIR_W), lambda b: (0, b, 0)),
        out_shape=jax.ShapeDtypeStruct((S5_PAIRS, bsz * length // S5_CHUNK, PAIR_W), BF16),
        scratch_shapes=[pltpu.VMEM((S5_SLABS, length, LANES), F32)],
        compiler_params=_cparams(("parallel",)),
        name="ctxproj",
    )(ctx, mod3, g_mix, w_in_u)


def _cmul(ar, ai, br, bi):
    return ar * br - ai * bi, ar * bi + ai * br


def _s5_ops_kernel(a_re_ref, a_im_ref, ls_ref, bt_re_ref, bt_im_ref, c_re_ref, c_im_ref, d_ref,
                   wst_ref, wy_ref, coef_ref):
    hp = lax.Precision.HIGHEST
    n_tok = S5_CHUNK
    blk = n_tok * S5_GROUP
    tok_of_lane = lax.broadcasted_iota(jnp.int32, (1, blk), 1) // S5_GROUP
    row8 = lax.broadcasted_iota(jnp.int32, (SUBLANES, 1), 0)
    diag = (lax.broadcasted_iota(jnp.int32, (blk, blk), 0)
            == lax.broadcasted_iota(jnp.int32, (blk, blk), 1))
    contract_n = (((1,), (1,)), ((), ()))

    for gg in range(2):
        rows_g = slice(gg * blk, (gg + 1) * blk)
        cols_g = slice(gg * blk, (gg + 1) * blk)
        cols_other = slice((1 - gg) * blk, (2 - gg) * blk)
        w_intra = jnp.where(diag, d_ref[0, gg], 0.0)
        for d in range(2):
            forward = d == 0
            a_re = a_re_ref[d, 0, gg]
            a_im = a_im_ref[d, 0, gg]
            dt = jnp.exp(ls_ref[d, 0, gg])
            mag = jnp.exp(a_re * dt)
            lam = (mag * jnp.cos(a_im * dt), mag * jnp.sin(a_im * dt))
            den = a_re * a_re + a_im * a_im
            p, q = lam[0] - 1.0, lam[1]
            k = ((p * a_re + q * a_im) / den, (q * a_re - p * a_im) / den)
            bbar = _cmul(*k, bt_re_ref[d, 0, gg], bt_im_ref[d, 0, gg])
            c = (c_re_ref[d, 0, gg], c_im_ref[d, 0, gg])
            pw = [(jnp.ones_like(p), jnp.zeros_like(p))]
            for _ in range(n_tok):
                pw.append(_cmul(*pw[-1], *lam))

            expo = [n_tok - 1 - i if forward else i for i in range(n_tok)]
            cols = [_cmul(*pw[e], *bbar) for e in expo]
            m_re = jnp.concatenate([t[0] for t in cols], axis=0)
            m_im = jnp.concatenate([t[1] for t in cols], axis=0)
            for comp, m in ((2 * d, m_re), (2 * d + 1, m_im)):
                lo = comp * LANES + gg * S5_STATE
                other = comp * LANES + (1 - gg) * S5_STATE
                wst_ref[0, rows_g, lo:lo + S5_STATE] = m.astype(BF16)
                wst_ref[0, rows_g, other:other + S5_STATE] = jnp.zeros((blk, S5_STATE), BF16)

            c_rep = [jnp.concatenate([t] * n_tok, axis=0) for t in c]
            kr = (lax.dot_general(m_re, c_rep[0], contract_n, precision=hp,
                                  preferred_element_type=F32)
                  - lax.dot_general(m_im, c_rep[1], contract_n, precision=hp,
                                    preferred_element_type=F32))
            toep = jnp.zeros((blk, blk), F32)
            for j in range(n_tok):
                s = S5_GROUP * (n_tok - 1 - j if forward else j)
                if s == 0:
                    moved = kr
                elif forward:
                    moved = jnp.concatenate([kr[s:], jnp.zeros((s, blk), F32)], axis=0)
                else:
                    moved = jnp.concatenate([jnp.zeros((s, blk), F32), kr[:blk - s]], axis=0)
                toep = jnp.where(tok_of_lane == j, moved, toep)
            w_intra = w_intra + toep

            expo = [j + 1 if forward else n_tok - j for j in range(n_tok)]
            rows = [_cmul(*pw[e], *c) for e in expo]
            r_re = jnp.concatenate([t[0] for t in rows], axis=0).T
            r_im = -jnp.concatenate([t[1] for t in rows], axis=0).T
            for comp, r in ((2 * d, r_re), (2 * d + 1, r_im)):
                r0 = PAIR_W + comp * LANES + gg * S5_STATE
                wy_ref[0, r0:r0 + S5_STATE, cols_g] = r.astype(BF16)
                wy_ref[0, r0:r0 + S5_STATE, cols_other] = jnp.zeros((S5_STATE, blk), BF16)

            lp = [(jnp.ones_like(p), jnp.zeros_like(p)), pw[n_tok]]
            for _ in range(SUBLANES - 1):
                lp.append(_cmul(*lp[-1], *pw[n_tok]))
            lanes = slice(d * LANES + gg * S5_STATE, d * LANES + (gg + 1) * S5_STATE)
            for kk, s in enumerate((1, 2, 4)):
                valid = (row8 >= s) if forward else (row8 + s < SUBLANES)
                coef_ref[0, 2 * kk, :, lanes] = jnp.where(valid, lp[s][0], 0.0)
                coef_ref[0, 2 * kk + 1, :, lanes] = jnp.where(valid, lp[s][1], 0.0)
            q_re = jnp.zeros((SUBLANES, S5_STATE), F32)
            q_im = jnp.zeros((SUBLANES, S5_STATE), F32)
            for r in range(SUBLANES):
                e = r + 1 if forward else SUBLANES - r
                q_re = jnp.where(row8 == r, lp[e][0], q_re)
                q_im = jnp.where(row8 == r, lp[e][1], q_im)
            coef_ref[0, 6, :, lanes] = q_re
            coef_ref[0, 7, :, lanes] = q_im

        wy_ref[0, rows_g, cols_g] = w_intra.astype(BF16)
        wy_ref[0, rows_g, cols_other] = jnp.zeros((blk, blk), BF16)


def _s5_operators(a_re, a_im, log_step, b_re, b_im, c_re, c_im, d_skip):
    grp = lambda t, *tail: t.reshape(2, S5_PAIRS, 2, *tail)
    row = lambda t: grp(t, 1, t.shape[-1])
    bt_re = grp(jnp.swapaxes(b_re, -1, -2), S5_GROUP, S5_STATE)
    bt_im = grp(jnp.swapaxes(b_im, -1, -2), S5_GROUP, S5_STATE)
    d_tiled = jnp.tile(d_skip.reshape(S5_PAIRS, 2, 1, S5_GROUP), (1, 1, 1, S5_CHUNK))
    per_pair = lambda *tail: pl.BlockSpec((2, 1, 2) + tail, lambda p: (0, p, 0) + (0,) * len(tail))
    out = lambda *tail: pl.BlockSpec((1,) + tail, lambda p: (p,) + (0,) * len(tail))
    return pl.pallas_call(
        _s5_ops_kernel,
        grid=(S5_PAIRS,),
        in_specs=[per_pair(1, S5_STATE), per_pair(1, S5_STATE), per_pair(1, 1),
                  per_pair(S5_GROUP, S5_STATE), per_pair(S5_GROUP, S5_STATE),
                  per_pair(S5_GROUP, S5_STATE), per_pair(S5_GROUP, S5_STATE),
                  pl.BlockSpec((1, 2, 1, S5_CHUNK * S5_GROUP), lambda p: (p, 0, 0, 0))],
        out_specs=[out(PAIR_W, PAIR_W), out(2 * PAIR_W, PAIR_W), out(8, SUBLANES, 2 * LANES)],
        out_shape=[jax.ShapeDtypeStruct((S5_PAIRS, PAIR_W, PAIR_W), BF16),
                   jax.ShapeDtypeStruct((S5_PAIRS, 2 * PAIR_W, PAIR_W), BF16),
                   jax.ShapeDtypeStruct((S5_PAIRS, 8, SUBLANES, 2 * LANES), F32)],
        compiler_params=_cparams(("parallel",)),
        name="s5_ops",
    )(row(a_re), row(a_im), grp(log_step, 1, 1), bt_re, bt_im,
      grp(c_re, S5_GROUP, S5_STATE), grp(c_im, S5_GROUP, S5_STATE), d_tiled)


def _s5_kernel(u_ref, uc_ref, wst_ref, wy_ref, coef_ref, y_ref, s_scr, sin_scr):
    n_main = u_ref.shape[1]
    n_ctx = uc_ref.shape[1]
    n_rows = n_ctx + n_main + n_ctx

    for pp in range(PAIRS_PER_SLAB):
        s_ctx = _bdot(uc_ref[pp], wst_ref[pp])
        s_scr[pp, 0:n_ctx] = s_ctx
        s_scr[pp, n_ctx:n_ctx + n_main] = _bdot(u_ref[pp], wst_ref[pp])
        s_scr[pp, n_ctx + n_main:] = s_ctx

    row = lax.broadcasted_iota(jnp.int32, (SUBLANES, LANES), 0)
    n_groups = n_rows // SUBLANES
    n_iter = (n_ctx + n_main) // SUBLANES

    def group(pp, base, c_re, c_im, lane0, forward):
        rows = pl.ds(base, SUBLANES)
        re_l = slice(lane0, lane0 + LANES)
        im_l = slice(lane0 + LANES, lane0 + 2 * LANES)
        cf = slice(0, LANES) if forward else slice(LANES, 2 * LANES)
        xr = s_scr[pp, rows, re_l]
        xi = s_scr[pp, rows, im_l]
        for k, sh in enumerate((1, 2, 4)):
            ar = coef_ref[pp, 2 * k, :, cf]
            ai = coef_ref[pp, 2 * k + 1, :, cf]
            amt = sh if forward else SUBLANES - sh
            sr = pltpu.roll(xr, amt, 0)
            si = pltpu.roll(xi, amt, 0)
            xr, xi = xr + (ar * sr - ai * si), xi + (ar * si + ai * sr)
        qr = coef_ref[pp, 6, :, cf]
        qi = coef_ref[pp, 7, :, cf]
        hr = xr + (qr * c_re - qi * c_im)
        hi = xi + (qr * c_im + qi * c_re)
        edge = 0 if forward else SUBLANES - 1
        amt = 1 if forward else SUBLANES - 1
        sin_scr[pp, rows, re_l] = jnp.where(row == edge, c_re, pltpu.roll(hr, amt, 0))
        sin_scr[pp, rows, im_l] = jnp.where(row == edge, c_im, pltpu.roll(hi, amt, 0))
        last = SUBLANES - 1 if forward else 0
        return (jnp.broadcast_to(hr[last:last + 1], (SUBLANES, LANES)),
                jnp.broadcast_to(hi[last:last + 1], (SUBLANES, LANES)))

    def body(g, carry):
        new = []
        for pp in range(PAIRS_PER_SLAB):
            f_re, f_im, b_re, b_im = carry[4 * pp:4 * pp + 4]
            f_base = pl.multiple_of(g * SUBLANES, SUBLANES)
            b_base = pl.multiple_of((n_groups - 1 - g) * SUBLANES, SUBLANES)
            f_re, f_im = group(pp, f_base, f_re, f_im, 0, True)
            b_re, b_im = group(pp, b_base, b_re, b_im, 2 * LANES, False)
            new += [f_re, f_im, b_re, b_im]
        return tuple(new)

    zero = jnp.zeros((SUBLANES, LANES), F32)
    lax.fori_loop(0, n_iter, body, (zero,) * (4 * PAIRS_PER_SLAB))

    main_rows = slice(n_ctx, n_ctx + n_main)
    for pp in range(PAIRS_PER_SLAB):
        y_ref[pp] = (_bdot(u_ref[pp], wy_ref[pp, 0:PAIR_W])
                     + _bdot(sin_scr[pp, main_rows, :].astype(BF16), wy_ref[pp, PAIR_W:2 * PAIR_W]))


def _s5(u_pairs, uc_pairs, bsz, w_state, w_y, coef):
    n_main = u_pairs.shape[1] // bsz
    n_ctx = uc_pairs.shape[1] // bsz
    n_rows = n_main + 2 * n_ctx
    pps = PAIRS_PER_SLAB
    rows = lambda n: pl.BlockSpec((pps, n, PAIR_W), lambda j, b: (j, b, 0))
    return pl.pallas_call(
        _s5_kernel,
        grid=(S5_SLABS, bsz),
        in_specs=[rows(n_main), rows(n_ctx),
                  pl.BlockSpec((pps, PAIR_W, PAIR_W), lambda j, b: (j, 0, 0)),
                  pl.BlockSpec((pps, 2 * PAIR_W, PAIR_W), lambda j, b: (j, 0, 0)),
                  pl.BlockSpec((pps, 8, SUBLANES, 2 * LANES), lambda j, b: (j, 0, 0, 0))],
        out_specs=rows(n_main),
        out_shape=jax.ShapeDtypeStruct(u_pairs.shape, F32),
        scratch_shapes=[pltpu.VMEM((pps, n_rows, PAIR_W), F32),
                        pltpu.VMEM((pps, n_rows, PAIR_W), F32)],
        compiler_params=_cparams(("arbitrary", "arbitrary")),
        name="s5",
    )(u_pairs, uc_pairs, w_state, w_y, coef)


def _mix_kernel(y_ref, x_ref, ga_ref, mb_ref, mod_ref, wglu_ref, bglu_ref, wpa_ref, wout_ref, o_ref,
                y_scr):
    m = mod_ref[0]
    tt = x_ref.shape[1]
    groups = [slice(k * tt // ROW_GROUPS, (k + 1) * tt // ROW_GROUPS) for k in range(ROW_GROUPS)]
    cpg = y_ref.shape[1] // ROW_GROUPS
    for k, r in enumerate(groups):
        for j in range(S5_SLABS):
            slabs = []
            for a in range(S5_CHUNK):
                pp, blk = _pair_lane_block(a)
                slabs.append(y_ref[j * PAIRS_PER_SLAB + pp, k * cpg:(k + 1) * cpg,
                                   blk * LANES:(blk + 1) * LANES])
            slabs = _swap_slab_and_piece(slabs)
            for t in range(S5_CHUNK):
                y_scr[j, pl.ds(r.start + t, cpg, stride=S5_CHUNK), :] = slabs[t]
    ya = [_gelu_tanh(jnp.concatenate([y_scr[j, r, :] for j in range(S5_SLABS)], axis=1))
          for r in groups]
    gate = [_bdot(t.astype(BF16), wglu_ref[...]) for t in ya]
    glu = [t * jax.nn.sigmoid(g + bglu_ref[...]) for t, g in zip(ya, gate)]
    pa = [_bdot(t.astype(BF16), wpa_ref[...]) for t in glu]
    merged = [(ga_ref[0, r, :].astype(F32) * p + mb_ref[0, r, :].astype(F32)).astype(BF16)
              for r, p in zip(groups, pa)]
    out = [_bdot(t, wout_ref[...]) for t in merged]
    for r, t in zip(groups, out):
        o_ref[0, r, :] = x_ref[0, r, :] + m[2:3] * t


def _mix(y_pairs, x, ga, mb, mod3, w_glu, b_glu, w_proj_a, w_out):
    bsz, length, _ = x.shape
    tt = TOK_TILE
    nt = length // tt
    tok = lambda w: pl.BlockSpec((1, tt, w), lambda b, i: (b, i, 0))
    return pl.pallas_call(
        _mix_kernel,
        grid=(bsz, nt),
        in_specs=[pl.BlockSpec((S5_PAIRS, tt // S5_CHUNK, PAIR_W), lambda b, i: (0, b * nt + i, 0)),
                  tok(D_MODEL), tok(D_MODEL), tok(D_MODEL),
                  pl.BlockSpec((1, N_MOD, D_MODEL), lambda b, i: (b, 0, 0)),
                  _const_spec(w_glu.shape), _const_spec(b_glu.shape),
                  _const_spec(w_proj_a.shape), _const_spec(w_out.shape)],
        out_specs=tok(D_MODEL),
        out_shape=jax.ShapeDtypeStruct(x.shape, F32),
        scratch_shapes=[pltpu.VMEM((S5_SLABS, tt, LANES), F32)],
        compiler_params=_cparams(("parallel", "parallel")),
        name="mix",
    )(y_pairs, x, ga, mb, mod3, w_glu, b_glu, w_proj_a, w_out)


def _conv_taps(cw_ref, cb_ref, cols):
    width = cols.stop - cols.start
    taps = [jnp.broadcast_to(cw_ref[k:k + 1, cols].astype(BF16), (GRID_W, width)) for k in range(9)]
    return taps, jnp.broadcast_to(cb_ref[:, cols].astype(BF16), (GRID_W, width))


def _token_neighbours(u):
    n, ch = u.shape
    shape4 = (n // GRID_W, GRID_W // SUBLANES, SUBLANES, ch)
    sub = lax.broadcasted_iota(jnp.int32, (1, 1, SUBLANES, 1), 2)
    zero = jnp.zeros((shape4[0], 1, SUBLANES, ch), F32)
    u4 = u.reshape(shape4)
    down = pltpu.roll(u4, 1, 2)
    left = jnp.where(sub == 0, jnp.concatenate([zero, down[:, :-1]], axis=1), down)
    up = pltpu.roll(u4, SUBLANES - 1, 2)
    right = jnp.where(sub == SUBLANES - 1, jnp.concatenate([up[:, 1:], zero], axis=1), up)
    return left.reshape(n, ch).astype(BF16), u.astype(BF16), right.reshape(n, ch).astype(BF16)


def _grid_conv_row(nbrs, taps, bias, rb):
    lo = rb * GRID_W
    part = []
    for dr in range(3):
        rows = slice(lo + dr * GRID_W, lo + (dr + 1) * GRID_W)
        left, here, right = (nbrs[dc][rows] * taps[3 * dr + dc] for dc in range(3))
        part.append((left + here) + right)
    return (part[0] + part[1]) + (part[2] + bias)


def _ffn_kernel(top_ref, x_ref, bot_ref, mod_ref, g_ref, wup_ref, cw_ref, cb_ref, wdn_ref, gf_ref,
                o_ref, h_scr, act_scr):
    i = pl.program_id(1)
    n_i = pl.num_programs(1)
    m = mod_ref[0]
    g = g_ref[...]
    tt = x_ref.shape[1]
    x = x_ref[0]

    def norm(t):
        return _modulated_rms(t, g, m[3:4], m[4:5])

    h_scr[0:GRID_W] = jnp.where(i > 0, norm(top_ref[0]), 0.0).astype(BF16)
    h_scr[GRID_W:GRID_W + tt] = norm(x).astype(BF16)
    h_scr[GRID_W + tt:] = jnp.where(i < n_i - 1, norm(bot_ref[0]), 0.0).astype(BF16)

    h = h_scr[...]

    fc = FFN_CHUNK
    for j in range(FFN_HIDDEN // fc):
        gs = slice(j * fc, (j + 1) * fc)
        vs = slice(FFN_HIDDEN + j * fc, FFN_HIDDEN + (j + 1) * fc)
        ug = _token_neighbours(_bdot(h, wup_ref[:, gs]))
        uv = _token_neighbours(_bdot(h, wup_ref[:, vs]))
        g_taps, g_bias = _conv_taps(cw_ref, cb_ref, gs)
        v_taps, v_bias = _conv_taps(cw_ref, cb_ref, vs)
        for rb in range(tt // GRID_W):
            cg = _grid_conv_row(ug, g_taps, g_bias, rb)
            cv = _grid_conv_row(uv, v_taps, v_bias, rb)
            act_scr[rb * GRID_W:(rb + 1) * GRID_W, gs] = (cg * jax.nn.sigmoid(cg)) * cv

    x2 = x + m[5:6] * _bdot(act_scr[...], wdn_ref[...])
    ms = jnp.mean(x2 * x2, axis=-1, keepdims=True)
    o_ref[0] = (x2 * lax.rsqrt(ms + EPS)) * gf_ref[...]


def _ffn(x1, mod3, g_ffn, w_up, conv_w9, conv_b, w_down, g_final):
    bsz, length, _ = x1.shape
    tt = FFN_TILE
    rows_per_tile = tt // GRID_W
    n_rows = length // GRID_W
    halo = lambda fn: pl.BlockSpec((1, GRID_W, D_MODEL), fn)
    return pl.pallas_call(
        _ffn_kernel,
        grid=(bsz, length // tt),
        in_specs=[halo(lambda b, i: (b, jnp.maximum(i * rows_per_tile - 1, 0), 0)),
                  pl.BlockSpec((1, tt, D_MODEL), lambda b, i: (b, i, 0)),
                  halo(lambda b, i: (b, jnp.minimum((i + 1) * rows_per_tile, n_rows - 1), 0)),
                  pl.BlockSpec((1, N_MOD, D_MODEL), lambda b, i: (b, 0, 0)),
                  _const_spec(g_ffn.shape), _const_spec(w_up.shape), _const_spec(conv_w9.shape),
                  _const_spec(conv_b.shape), _const_spec(w_down.shape), _const_spec(g_final.shape)],
        out_specs=pl.BlockSpec((1, tt, D_MODEL), lambda b, i: (b, i, 0)),
        out_shape=jax.ShapeDtypeStruct(x1.shape, F32),
        scratch_shapes=[pltpu.VMEM((tt + 2 * GRID_W, D_MODEL), BF16),
                        pltpu.VMEM((tt, FFN_HIDDEN), BF16)],
        compiler_params=_cparams(("parallel", "parallel")),
        name="ffn",
    )(x1, x1, x1, mod3, g_ffn, w_up, conv_w9, conv_b, w_down, g_final)


def kernel(x, c, ctx, c_ctx, w_ada, b_ada, g_mix, w_in, s5_a_re, s5_a_im, s5_log_step, s5_b_re, s5_b_im, s5_c_re, s5_c_im, s5_d, s5_w_glu, s5_b_glu, sgu_ln_g, sgu_ln_b, sgu_w, sgu_b, w_proj_a, w_proj_b, b_gate, w_out, g_ffn, w_up, conv_w, conv_b, w_down, g_final):
    bsz = x.shape[0]
    row = lambda t: t.reshape(1, -1)

    ctx_row = bsz
    c_rows = jnp.zeros((SUBLANES, D_MODEL), F32).at[:bsz].set(c).at[ctx_row].set(c_ctx)
    mod3 = _ada_rows(c_rows, w_ada[0], b_ada[0]).reshape(SUBLANES, N_MOD, D_MODEL)

    w_in_b = w_in[0].astype(BF16)
    wsp = sgu_w[0].astype(BF16)
    wsp_pair = jnp.concatenate([wsp[0::2], wsp[1::2]], axis=-1)
    bsp_rows = jnp.repeat(sgu_b[0].T, SGU_GROUP_DIM, axis=1)

    u_pairs, ga, mb = _inproj(x, mod3, row(g_mix[0]), w_in_b, row(b_gate[0]), row(sgu_ln_g[0]),
                              row(sgu_ln_b[0]), wsp_pair, bsp_rows, w_proj_b[0].astype(BF16))
    uc_pairs = _ctxproj(ctx, mod3, ctx_row, row(g_mix[0]), w_in_b[:, :S5_WIDTH])

    w_state, w_y, coef = _s5_operators(
        s5_a_re[0], s5_a_im[0], s5_log_step[0], s5_b_re[0], s5_b_im[0], s5_c_re[0], s5_c_im[0],
        s5_d[0])
    y_pairs = _s5(u_pairs, uc_pairs, bsz, w_state, w_y, coef)

    x1 = _mix(y_pairs, x, ga, mb, mod3, s5_w_glu[0].astype(BF16), row(s5_b_glu[0]),
              w_proj_a[0].astype(BF16), w_out[0].astype(BF16))

    return _ffn(x1, mod3, row(g_ffn[0]), w_up[0].astype(BF16), conv_w[0].reshape(9, -1),
                row(conv_b[0]), w_down[0].astype(BF16), row(g_final))
```

```python
import math

import jax
import jax.numpy as jnp
from jax import lax
from jax.experimental import pallas as pl
from jax.experimental.pallas import tpu as pltpu

F32 = jnp.float32
BF16 = jnp.bfloat16

D_MODEL = 1024
GRID_W = 64
S5_WIDTH = 512
S5_GROUP = 16
S5_GROUPS = S5_WIDTH // S5_GROUP
S5_STATE = 64
SGU_WIDTH = 512
SGU_GROUPS = 8
SGU_GROUP_DIM = SGU_WIDTH // SGU_GROUPS
CHUNK = 128
FFN_HIDDEN = 2816
N_MOD = 6
EPS = 1e-6

LANES = 128
SUBLANES = 8
V7X_VMEM_LIMIT = 56 * 1024 * 1024

S5_CHUNK = 16
S5_PAIRS = S5_GROUPS // 2
PAIR_W = 2 * S5_CHUNK * S5_GROUP
S5_SLABS = S5_WIDTH // LANES
PAIRS_PER_SLAB = S5_PAIRS // S5_SLABS

TOK_TILE = 512
ROW_GROUPS = 2
FFN_TILE = 512
FFN_CHUNK = 256


def _cparams(sem):
    return pltpu.CompilerParams(dimension_semantics=sem, vmem_limit_bytes=V7X_VMEM_LIMIT)


def _const_spec(shape):
    zeros = (0,) * len(shape)
    return pl.BlockSpec(shape, lambda *_: zeros, pipeline_mode=pl.Buffered(1))


def _gelu_tanh(x):
    c = math.sqrt(2.0 / math.pi)
    return 0.5 * x * (1.0 + jnp.tanh(c * (x + 0.044715 * (x * x * x))))


def _modulated_rms(x, g, shift, scale):
    ms = jnp.mean(x * x, axis=-1, keepdims=True)
    return (x * lax.rsqrt(ms + EPS)) * (g * (1.0 + scale)) + shift


def _bdot(a, b):
    return jnp.dot(a, b, preferred_element_type=F32)


def _ada_kernel(c_ref, w_ref, b_ref, o_ref):
    c = c_ref[...]
    cs = c * jax.nn.sigmoid(c)
    o_ref[...] = jnp.dot(cs, w_ref[...], preferred_element_type=F32,
                         precision=lax.Precision.HIGHEST) + b_ref[...]


def _ada_rows(c_rows, w_ada, b_ada):
    n = w_ada.shape[1]
    bn = 1536
    return pl.pallas_call(
        _ada_kernel,
        grid=(n // bn,),
        in_specs=[_const_spec(c_rows.shape),
                  pl.BlockSpec((D_MODEL, bn), lambda j: (0, j)),
                  pl.BlockSpec((1, bn), lambda j: (0, j))],
        out_specs=pl.BlockSpec((c_rows.shape[0], bn), lambda j: (0, j)),
        out_shape=jax.ShapeDtypeStruct((c_rows.shape[0], n), F32),
        compiler_params=_cparams(("arbitrary",)),
        name="ada",
    )(c_rows, w_ada, b_ada.reshape(1, n))


def _swap_slab_and_piece(slabs):
    lane = lax.broadcasted_iota(jnp.int32, (1, LANES), 1)
    out = list(slabs)
    for k in range(3):
        width = S5_GROUP << k
        keep = ((lane >> (4 + k)) & 1) == 0
        nxt = list(out)
        for a0 in range(len(out)):
            if a0 & (1 << k):
                continue
            a1 = a0 | (1 << k)
            nxt[a0] = jnp.where(keep, out[a0], pltpu.roll(out[a1], width, 1))
            nxt[a1] = jnp.where(keep, pltpu.roll(out[a0], LANES - width, 1), out[a1])
        out = nxt
    return out


def _pair_lane_block(slab_idx):
    t_hi, g = divmod(slab_idx, SUBLANES)
    pp, gg = divmod(g, 2)
    return pp, gg * 2 + t_hi


def _store_pair_tiles(u, u_scr, up_ref, row0=0):
    n_chunks = u.shape[0] // S5_CHUNK
    chunks = slice(row0 // S5_CHUNK, row0 // S5_CHUNK + n_chunks)
    for j in range(S5_SLABS):
        u_scr[j, row0:row0 + u.shape[0], :] = u[:, j * LANES:(j + 1) * LANES]
    for j in range(S5_SLABS):
        slabs = _swap_slab_and_piece(
            [u_scr[j, pl.ds(row0 + t, n_chunks, stride=S5_CHUNK), :] for t in range(S5_CHUNK)])
        for a in range(S5_CHUNK):
            pp, blk = _pair_lane_block(a)
            up_ref[j * PAIRS_PER_SLAB + pp, chunks, blk * LANES:(blk + 1) * LANES] = (
                slabs[a].astype(BF16))


def _inproj_kernel(x_ref, mod_ref, g_ref, w_ref, bgate_ref, lng_ref, lnb_ref, wsp_ref, bsp_ref,
                   wpb_ref, u_ref, ga_ref, mb_ref, u_scr):
    m = mod_ref[0]
    tt = x_ref.shape[1]
    z0 = S5_WIDTH
    g0 = S5_WIDTH + 2 * SGU_WIDTH
    lane = lax.broadcasted_iota(jnp.int32, (CHUNK, LANES), 1)
    zero = jnp.zeros((CHUNK, LANES), BF16)

    groups = [slice(k * tt // ROW_GROUPS, (k + 1) * tt // ROW_GROUPS) for k in range(ROW_GROUPS)]
    proj = []
    for r in groups:
        h = _modulated_rms(x_ref[0, r, :], g_ref[...], m[0:1], m[1:2]).astype(BF16)
        proj.append(dict(
            u=_bdot(h, w_ref[:, 0:S5_WIDTH]),
            zv=_bdot(h, w_ref[:, z0 + SGU_WIDTH:z0 + 2 * SGU_WIDTH]),
            zu=_bdot(h, w_ref[:, z0:z0 + SGU_WIDTH]),
            ga=_bdot(h, w_ref[:, g0:g0 + D_MODEL]),
            gb=_bdot(h, w_ref[:, g0 + D_MODEL:g0 + 2 * D_MODEL])))

    for r, p in zip(groups, proj):
        _store_pair_tiles(p["u"], u_scr, u_ref, r.start)

        zv = _gelu_tanh(p["zv"])
        mu = jnp.mean(zv, axis=-1, keepdims=True)
        zc = zv - mu
        var = jnp.mean(zc * zc, axis=-1, keepdims=True)
        v = ((zc * lax.rsqrt(var + EPS)) * lng_ref[...] + lnb_ref[...]).astype(BF16)
        zu = _gelu_tanh(p["zu"])

        rows = []
        for c in range(zv.shape[0] // CHUNK):
            tiles = []
            for j in range(SGU_WIDTH // LANES):
                vj = v[c * CHUNK:(c + 1) * CHUNK, j * LANES:(j + 1) * LANES]
                rhs = jnp.concatenate([jnp.where(lane < SGU_GROUP_DIM, vj, zero),
                                       jnp.where(lane >= SGU_GROUP_DIM, vj, zero)], axis=0)
                tiles.append(_bdot(wsp_ref[j], rhs))
            rows.append(jnp.concatenate(tiles, axis=1) + bsp_ref[...])
        s = jnp.concatenate(rows, axis=0)
        pb = _bdot((zu * s).astype(BF16), wpb_ref[...])

        ga = jax.nn.sigmoid(p["ga"] + bgate_ref[:, 0:D_MODEL])
        gb = jax.nn.sigmoid(p["gb"] + bgate_ref[:, D_MODEL:2 * D_MODEL])
        ga_ref[0, r, :] = ga.astype(BF16)
        mb_ref[0, r, :] = (gb * pb).astype(BF16)


def _inproj(x, mod3, g_mix, w_in, b_gate, ln_g, ln_b, wsp_pair, bsp_rows, w_proj_b):
    bsz, length, _ = x.shape
    tt = TOK_TILE
    nt = length // tt
    tok = lambda w: pl.BlockSpec((1, tt, w), lambda b, i: (b, i, 0))
    return pl.pallas_call(
        _inproj_kernel,
        grid=(bsz, nt),
        in_specs=[tok(D_MODEL),
                  pl.BlockSpec((1, N_MOD, D_MODEL), lambda b, i: (b, 0, 0)),
                  _const_spec(g_mix.shape), _const_spec(w_in.shape), _const_spec(b_gate.shape),
                  _const_spec(ln_g.shape), _const_spec(ln_b.shape), _const_spec(wsp_pair.shape),
                  _const_spec(bsp_rows.shape), _const_spec(w_proj_b.shape)],
        out_specs=[pl.BlockSpec((S5_PAIRS, tt // S5_CHUNK, PAIR_W), lambda b, i: (0, b * nt + i, 0)),
                   tok(D_MODEL), tok(D_MODEL)],
        out_shape=[jax.ShapeDtypeStruct((S5_PAIRS, bsz * length // S5_CHUNK, PAIR_W), BF16),
                   jax.ShapeDtypeStruct((bsz, length, D_MODEL), BF16),
                   jax.ShapeDtypeStruct((bsz, length, D_MODEL), BF16)],
        scratch_shapes=[pltpu.VMEM((S5_SLABS, tt, LANES), F32)],
        compiler_params=_cparams(("parallel", "parallel")),
        name="inproj",
    )(x, mod3, g_mix, w_in, b_gate, ln_g, ln_b, wsp_pair, bsp_rows, w_proj_b)


def _ctxproj_kernel(x_ref, mod_ref, g_ref, w_ref, u_ref, u_scr):
    m = mod_ref[0]
    h = _modulated_rms(x_ref[0], g_ref[...], m[0:1], m[1:2]).astype(BF16)
    _store_pair_tiles(_bdot(h, w_ref[...]), u_scr, u_ref)


def _ctxproj(ctx, mod3, ctx_row, g_mix, w_in_u):
    bsz, length, _ = ctx.shape
    return pl.pallas_call(
        _ctxproj_kernel,
        grid=(bsz,),
        in_specs=[pl.BlockSpec((1, length, D_MODEL), lambda b: (b, 0, 0)),
                  pl.BlockSpec((1, N_MOD, D_MODEL), lambda b: (ctx_row, 0, 0)),
                  _const_spec(g_mix.shape), _const_spec(w_in_u.shape)],
        out_specs=pl.BlockSpec((S5_PAIRS, length // S5_CHUNK, PAIR_W), lambda b: (0, b, 0)),
        out_shape=jax.ShapeDtypeStruct((S5_PAIRS, bsz * length // S5_CHUNK, PAIR_W), BF16),
        scratch_shapes=[pltpu.VMEM((S5_SLABS, length, LANES), F32)],
        compiler_params=_cparams(("parallel",)),
        name="ctxproj",
    )(ctx, mod3, g_mix, w_in_u)


def _cmul(ar, ai, br, bi):
    return ar * br - ai * bi, ar * bi + ai * br


def _s5_ops_kernel(a_re_ref, a_im_ref, ls_ref, bt_re_ref, bt_im_ref, c_re_ref, c_im_ref, d_ref,
                   wst_ref, wy_ref, coef_ref):
    hp = lax.Precision.HIGHEST
    n_tok = S5_CHUNK
    blk = n_tok * S5_GROUP
    tok_of_lane = lax.broadcasted_iota(jnp.int32, (1, blk), 1) // S5_GROUP
    row8 = lax.broadcasted_iota(jnp.int32, (SUBLANES, 1), 0)
    diag = (lax.broadcasted_iota(jnp.int32, (blk, blk), 0)
            == lax.broadcasted_iota(jnp.int32, (blk, blk), 1))
    contract_n = (((1,), (1,)), ((), ()))

    for gg in range(2):
        rows_g = slice(gg * blk, (gg + 1) * blk)
        cols_g = slice(gg * blk, (gg + 1) * blk)
        cols_other = slice((1 - gg) * blk, (2 - gg) * blk)
        w_intra = jnp.where(diag, d_ref[0, gg], 0.0)
        for d in range(2):
            forward = d == 0
            a_re = a_re_ref[d, 0, gg]
            a_im = a_im_ref[d, 0, gg]
            dt = jnp.exp(ls_ref[d, 0, gg])
            mag = jnp.exp(a_re * dt)
            lam = (mag * jnp.cos(a_im * dt), mag * jnp.sin(a_im * dt))
            den = a_re * a_re + a_im * a_im
            p, q = lam[0] - 1.0, lam[1]
            k = ((p * a_re + q * a_im) / den, (q * a_re - p * a_im) / den)
            bbar = _cmul(*k, bt_re_ref[d, 0, gg], bt_im_ref[d, 0, gg])
            c = (c_re_ref[d, 0, gg], c_im_ref[d, 0, gg])
            pw = [(jnp.ones_like(p), jnp.zeros_like(p))]
            for _ in range(n_tok):
                pw.append(_cmul(*pw[-1], *lam))

            expo = [n_tok - 1 - i if forward else i for i in range(n_tok)]
            cols = [_cmul(*pw[e], *bbar) for e in expo]
            m_re = jnp.concatenate([t[0] for t in cols], axis=0)
            m_im = jnp.concatenate([t[1] for t in cols], axis=0)
            for comp, m in ((2 * d, m_re), (2 * d + 1, m_im)):
                lo = comp * LANES + gg * S5_STATE
                other = comp * LANES + (1 - gg) * S5_STATE
                wst_ref[0, rows_g, lo:lo + S5_STATE] = m.astype(BF16)
                wst_ref[0, rows_g, other:other + S5_STATE] = jnp.zeros((blk, S5_STATE), BF16)

            c_rep = [jnp.concatenate([t] * n_tok, axis=0) for t in c]
            kr = (lax.dot_general(m_re, c_rep[0], contract_n, precision=hp,
                                  preferred_element_type=F32)
                  - lax.dot_general(m_im, c_rep[1], contract_n, precision=hp,
                                    preferred_element_type=F32))
            toep = jnp.zeros((blk, blk), F32)
            for j in range(n_tok):
                s = S5_GROUP * (n_tok - 1 - j if forward else j)
                if s == 0:
                    moved = kr
                elif forward:
                    moved = jnp.concatenate([kr[s:], jnp.zeros((s, blk), F32)], axis=0)
                else:
                    moved = jnp.concatenate([jnp.zeros((s, blk), F32), kr[:blk - s]], axis=0)
                toep = jnp.where(tok_of_lane == j, moved, toep)
            w_intra = w_intra + toep

            expo = [j + 1 if forward else n_tok - j for j in range(n_tok)]
            rows = [_cmul(*pw[e], *c) for e in expo]
            r_re = jnp.concatenate([t[0] for t in rows], axis=0).T
            r_im = -jnp.concatenate([t[1] for t in rows], axis=0).T
            for comp, r in ((2 * d, r_re), (2 * d + 1, r_im)):
                r0 = PAIR_W + comp * LANES + gg * S5_STATE
                wy_ref[0, r0:r0 + S5_STATE, cols_g] = r.astype(BF16)
                wy_ref[0, r0:r0 + S5_STATE, cols_other] = jnp.zeros((S5_STATE, blk), BF16)

            lp = [(jnp.ones_like(p), jnp.zeros_like(p)), pw[n_tok]]
            for _ in range(SUBLANES - 1):
                lp.append(_cmul(*lp[-1], *pw[n_tok]))
            lanes = slice(d * LANES + gg * S5_STATE, d * LANES + (gg + 1) * S5_STATE)
            for kk, s in enumerate((1, 2, 4)):
                valid = (row8 >= s) if forward else (row8 + s < SUBLANES)
                coef_ref[0, 2 * kk, :, lanes] = jnp.where(valid, lp[s][0], 0.0)
                coef_ref[0, 2 * kk + 1, :, lanes] = jnp.where(valid, lp[s][1], 0.0)
            q_re = jnp.zeros((SUBLANES, S5_STATE), F32)
            q_im = jnp.zeros((SUBLANES, S5_STATE), F32)
            for r in range(SUBLANES):
                e = r + 1 if forward else SUBLANES - r
                q_re = jnp.where(row8 == r, lp[e][0], q_re)
                q_im = jnp.where(row8 == r, lp[e][1], q_im)
            coef_ref[0, 6, :, lanes] = q_re
            coef_ref[0, 7, :, lanes] = q_im

        wy_ref[0, rows_g, cols_g] = w_intra.astype(BF16)
        wy_ref[0, rows_g, cols_other] = jnp.zeros((blk, blk), BF16)


def _s5_operators(a_re, a_im, log_step, b_re, b_im, c_re, c_im, d_skip):
    grp = lambda t, *tail: t.reshape(2, S5_PAIRS, 2, *tail)
    row = lambda t: grp(t, 1, t.shape[-1])
    bt_re = grp(jnp.swapaxes(b_re, -1, -2), S5_GROUP, S5_STATE)
    bt_im = grp(jnp.swapaxes(b_im, -1, -2), S5_GROUP, S5_STATE)
    d_tiled = jnp.tile(d_skip.reshape(S5_PAIRS, 2, 1, S5_GROUP), (1, 1, 1, S5_CHUNK))
    per_pair = lambda *tail: pl.BlockSpec((2, 1, 2) + tail, lambda p: (0, p, 0) + (0,) * len(tail))
    out = lambda *tail: pl.BlockSpec((1,) + tail, lambda p: (p,) + (0,) * len(tail))
    return pl.pallas_call(
        _s5_ops_kernel,
        grid=(S5_PAIRS,),
        in_specs=[per_pair(1, S5_STATE), per_pair(1, S5_STATE), per_pair(1, 1),
                  per_pair(S5_GROUP, S5_STATE), per_pair(S5_GROUP, S5_STATE),
                  per_pair(S5_GROUP, S5_STATE), per_pair(S5_GROUP, S5_STATE),
                  pl.BlockSpec((1, 2, 1, S5_CHUNK * S5_GROUP), lambda p: (p, 0, 0, 0))],
        out_specs=[out(PAIR_W, PAIR_W), out(2 * PAIR_W, PAIR_W), out(8, SUBLANES, 2 * LANES)],
        out_shape=[jax.ShapeDtypeStruct((S5_PAIRS, PAIR_W, PAIR_W), BF16),
                   jax.ShapeDtypeStruct((S5_PAIRS, 2 * PAIR_W, PAIR_W), BF16),
                   jax.ShapeDtypeStruct((S5_PAIRS, 8, SUBLANES, 2 * LANES), F32)],
        compiler_params=_cparams(("parallel",)),
        name="s5_ops",
    )(row(a_re), row(a_im), grp(log_step, 1, 1), bt_re, bt_im,
      grp(c_re, S5_GROUP, S5_STATE), grp(c_im, S5_GROUP, S5_STATE), d_tiled)


def _s5_kernel(u_ref, uc_ref, wst_ref, wy_ref, coef_ref, y_ref, s_scr, sin_scr):
    n_main = u_ref.shape[1]
    n_ctx = uc_ref.shape[1]
    n_rows = n_ctx + n_main + n_ctx

    for pp in range(PAIRS_PER_SLAB):
        s_ctx = _bdot(uc_ref[pp], wst_ref[pp])
        s_scr[pp, 0:n_ctx] = s_ctx
        s_scr[pp, n_ctx:n_ctx + n_main] = _bdot(u_ref[pp], wst_ref[pp])
        s_scr[pp, n_ctx + n_main:] = s_ctx

    row = lax.broadcasted_iota(jnp.int32, (SUBLANES, LANES), 0)
    n_groups = n_rows // SUBLANES
    n_iter = (n_ctx + n_main) // SUBLANES

    def group(pp, base, c_re, c_im, lane0, forward):
        rows = pl.ds(base, SUBLANES)
        re_l = slice(lane0, lane0 + LANES)
        im_l = slice(lane0 + LANES, lane0 + 2 * LANES)
        cf = slice(0, LANES) if forward else slice(LANES, 2 * LANES)
        xr = s_scr[pp, rows, re_l]
        xi = s_scr[pp, rows, im_l]
        for k, sh in enumerate((1, 2, 4)):
            ar = coef_ref[pp, 2 * k, :, cf]
            ai = coef_ref[pp, 2 * k + 1, :, cf]
            amt = sh if forward else SUBLANES - sh
            sr = pltpu.roll(xr, amt, 0)
            si = pltpu.roll(xi, amt, 0)
            xr, xi = xr + (ar * sr - ai * si), xi + (ar * si + ai * sr)
        qr = coef_ref[pp, 6, :, cf]
        qi = coef_ref[pp, 7, :, cf]
        hr = xr + (qr * c_re - qi * c_im)
        hi = xi + (qr * c_im + qi * c_re)
        edge = 0 if forward else SUBLANES - 1
        amt = 1 if forward else SUBLANES - 1
        sin_scr[pp, rows, re_l] = jnp.where(row == edge, c_re, pltpu.roll(hr, amt, 0))
        sin_scr[pp, rows, im_l] = jnp.where(row == edge, c_im, pltpu.roll(hi, amt, 0))
        last = SUBLANES - 1 if forward else 0
        return (jnp.broadcast_to(hr[last:last + 1], (SUBLANES, LANES)),
                jnp.broadcast_to(hi[last:last + 1], (SUBLANES, LANES)))

    def body(g, carry):
        new = []
        for pp in range(PAIRS_PER_SLAB):
            f_re, f_im, b_re, b_im = carry[4 * pp:4 * pp + 4]
            f_base = pl.multiple_of(g * SUBLANES, SUBLANES)
            b_base = pl.multiple_of((n_groups - 1 - g) * SUBLANES, SUBLANES)
            f_re, f_im = group(pp, f_base, f_re, f_im, 0, True)
            b_re, b_im = group(pp, b_base, b_re, b_im, 2 * LANES, False)
            new += [f_re, f_im, b_re, b_im]
        return tuple(new)

    zero = jnp.zeros((SUBLANES, LANES), F32)
    lax.fori_loop(0, n_iter, body, (zero,) * (4 * PAIRS_PER_SLAB))

    main_rows = slice(n_ctx, n_ctx + n_main)
    for pp in range(PAIRS_PER_SLAB):
        y_ref[pp] = (_bdot(u_ref[pp], wy_ref[pp, 0:PAIR_W])
                     + _bdot(sin_scr[pp, main_rows, :].astype(BF16), wy_ref[pp, PAIR_W:2 * PAIR_W]))


def _s5(u_pairs, uc_pairs, bsz, w_state, w_y, coef):
    n_main = u_pairs.shape[1] // bsz
    n_ctx = uc_pairs.shape[1] // bsz
    n_rows = n_main + 2 * n_ctx
    pps = PAIRS_PER_SLAB
    rows = lambda n: pl.BlockSpec((pps, n, PAIR_W), lambda j, b: (j, b, 0))
    return pl.pallas_call(
        _s5_kernel,
        grid=(S5_SLABS, bsz),
        in_specs=[rows(n_main), rows(n_ctx),
                  pl.BlockSpec((pps, PAIR_W, PAIR_W), lambda j, b: (j, 0, 0)),
                  pl.BlockSpec((pps, 2 * PAIR_W, PAIR_W), lambda j, b: (j, 0, 0)),
                  pl.BlockSpec((pps, 8, SUBLANES, 2 * LANES), lambda j, b: (j, 0, 0, 0))],
        out_specs=rows(n_main),
        out_shape=jax.ShapeDtypeStruct(u_pairs.shape, F32),
        scratch_shapes=[pltpu.VMEM((pps, n_rows, PAIR_W), F32),
                        pltpu.VMEM((pps, n_rows, PAIR_W), F32)],
        compiler_params=_cparams(("arbitrary", "arbitrary")),
        name="s5",
    )(u_pairs, uc_pairs, w_state, w_y, coef)


def _mix_kernel(y_ref, x_ref, ga_ref, mb_ref, mod_ref, wglu_ref, bglu_ref, wpa_ref, wout_ref, o_ref,
                y_scr):
    m = mod_ref[0]
    tt = x_ref.shape[1]
    groups = [slice(k * tt // ROW_GROUPS, (k + 1) * tt // ROW_GROUPS) for k in range(ROW_GROUPS)]
    cpg = y_ref.shape[1] // ROW_GROUPS
    for k, r in enumerate(groups):
        for j in range(S5_SLABS):
            slabs = []
            for a in range(S5_CHUNK):
                pp, blk = _pair_lane_block(a)
                slabs.append(y_ref[j * PAIRS_PER_SLAB + pp, k * cpg:(k + 1) * cpg,
                                   blk * LANES:(blk + 1) * LANES])
            slabs = _swap_slab_and_piece(slabs)
            for t in range(S5_CHUNK):
                y_scr[j, pl.ds(r.start + t, cpg, stride=S5_CHUNK), :] = slabs[t]
    ya = [_gelu_tanh(jnp.concatenate([y_scr[j, r, :] for j in range(S5_SLABS)], axis=1))
          for r in groups]
    gate = [_bdot(t.astype(BF16), wglu_ref[...]) for t in ya]
    glu = [t * jax.nn.sigmoid(g + bglu_ref[...]) for t, g in zip(ya, gate)]
    pa = [_bdot(t.astype(BF16), wpa_ref[...]) for t in glu]
    merged = [(ga_ref[0, r, :].astype(F32) * p + mb_ref[0, r, :].astype(F32)).astype(BF16)
              for r, p in zip(groups, pa)]
    out = [_bdot(t, wout_ref[...]) for t in merged]
    for r, t in zip(groups, out):
        o_ref[0, r, :] = x_ref[0, r, :] + m[2:3] * t


def _mix(y_pairs, x, ga, mb, mod3, w_glu, b_glu, w_proj_a, w_out):
    bsz, length, _ = x.shape
    tt = TOK_TILE
    nt = length // tt
    tok = lambda w: pl.BlockSpec((1, tt, w), lambda b, i: (b, i, 0))
    return pl.pallas_call(
        _mix_kernel,
        grid=(bsz, nt),
        in_specs=[pl.BlockSpec((S5_PAIRS, tt // S5_CHUNK, PAIR_W), lambda b, i: (0, b * nt + i, 0)),
                  tok(D_MODEL), tok(D_MODEL), tok(D_MODEL),
                  pl.BlockSpec((1, N_MOD, D_MODEL), lambda b, i: (b, 0, 0)),
                  _const_spec(w_glu.shape), _const_spec(b_glu.shape),
                  _const_spec(w_proj_a.shape), _const_spec(w_out.shape)],
        out_specs=tok(D_MODEL),
        out_shape=jax.ShapeDtypeStruct(x.shape, F32),
        scratch_shapes=[pltpu.VMEM((S5_SLABS, tt, LANES), F32)],
        compiler_params=_cparams(("parallel", "parallel")),
        name="mix",
    )(y_pairs, x, ga, mb, mod3, w_glu, b_glu, w_proj_a, w_out)


def _conv_taps(cw_ref, cb_ref, cols):
    width = cols.stop - cols.start
    taps = [jnp.broadcast_to(cw_ref[k:k + 1, cols].astype(BF16), (GRID_W, width)) for k in range(9)]
    return taps, jnp.broadcast_to(cb_ref[:, cols].astype(BF16), (GRID_W, width))


def _token_neighbours(u):
    n, ch = u.shape
    shape4 = (n // GRID_W, GRID_W // SUBLANES, SUBLANES, ch)
    sub = lax.broadcasted_iota(jnp.int32, (1, 1, SUBLANES, 1), 2)
    zero = jnp.zeros((shape4[0], 1, SUBLANES, ch), F32)
    u4 = u.reshape(shape4)
    down = pltpu.roll(u4, 1, 2)
    left = jnp.where(sub == 0, jnp.concatenate([zero, down[:, :-1]], axis=1), down)
    up = pltpu.roll(u4, SUBLANES - 1, 2)
    right = jnp.where(sub == SUBLANES - 1, jnp.concatenate([up[:, 1:], zero], axis=1), up)
    return left.reshape(n, ch).astype(BF16), u.astype(BF16), right.reshape(n, ch).astype(BF16)


def _grid_conv_row(nbrs, taps, bias, rb):
    lo = rb * GRID_W
    part = []
    for dr in range(3):
        rows = slice(lo + dr * GRID_W, lo + (dr + 1) * GRID_W)
        left, here, right = (nbrs[dc][rows] * taps[3 * dr + dc] for dc in range(3))
        part.append((left + here) + right)
    return (part[0] + part[1]) + (part[2] + bias)


def _ffn_kernel(top_ref, x_ref, bot_ref, mod_ref, g_ref, wup_ref, cw_ref, cb_ref, wdn_ref, gf_ref,
                o_ref, h_scr, act_scr):
    i = pl.program_id(1)
    n_i = pl.num_programs(1)
    m = mod_ref[0]
    g = g_ref[...]
    tt = x_ref.shape[1]
    x = x_ref[0]

    def norm(t):
        return _modulated_rms(t, g, m[3:4], m[4:5])

    h_scr[0:GRID_W] = jnp.where(i > 0, norm(top_ref[0]), 0.0).astype(BF16)
    h_scr[GRID_W:GRID_W + tt] = norm(x).astype(BF16)
    h_scr[GRID_W + tt:] = jnp.where(i < n_i - 1, norm(bot_ref[0]), 0.0).astype(BF16)

    h = h_scr[...]

    fc = FFN_CHUNK
    for j in range(FFN_HIDDEN // fc):
        gs = slice(j * fc, (j + 1) * fc)
        vs = slice(FFN_HIDDEN + j * fc, FFN_HIDDEN + (j + 1) * fc)
        ug = _token_neighbours(_bdot(h, wup_ref[:, gs]))
        uv = _token_neighbours(_bdot(h, wup_ref[:, vs]))
        g_taps, g_bias = _conv_taps(cw_ref, cb_ref, gs)
        v_taps, v_bias = _conv_taps(cw_ref, cb_ref, vs)
        for rb in range(tt // GRID_W):
            cg = _grid_conv_row(ug, g_taps, g_bias, rb)
            cv = _grid_conv_row(uv, v_taps, v_bias, rb)
            act_scr[rb * GRID_W:(rb + 1) * GRID_W, gs] = (cg * jax.nn.sigmoid(cg)) * cv

    x2 = x + m[5:6] * _bdot(act_scr[...], wdn_ref[...])
    ms = jnp.mean(x2 * x2, axis=-1, keepdims=True)
    o_ref[0] = (x2 * lax.rsqrt(ms + EPS)) * gf_ref[...]


def _ffn(x1, mod3, g_ffn, w_up, conv_w9, conv_b, w_down, g_final):
    bsz, length, _ = x1.shape
    tt = FFN_TILE
    rows_per_tile = tt // GRID_W
    n_rows = length // GRID_W
    halo = lambda fn: pl.BlockSpec((1, GRID_W, D_MODEL), fn)
    return pl.pallas_call(
        _ffn_kernel,
        grid=(bsz, length // tt),
        in_specs=[halo(lambda b, i: (b, jnp.maximum(i * rows_per_tile - 1, 0), 0)),
                  pl.BlockSpec((1, tt, D_MODEL), lambda b, i: (b, i, 0)),
                  halo(lambda b, i: (b, jnp.minimum((i + 1) * rows_per_tile, n_rows - 1), 0)),
                  pl.BlockSpec((1, N_MOD, D_MODEL), lambda b, i: (b, 0, 0)),
                  _const_spec(g_ffn.shape), _const_spec(w_up.shape), _const_spec(conv_w9.shape),
                  _const_spec(conv_b.shape), _const_spec(w_down.shape), _const_spec(g_final.shape)],
        out_specs=pl.BlockSpec((1, tt, D_MODEL), lambda b, i: (b, i, 0)),
        out_shape=jax.ShapeDtypeStruct(x1.shape, F32),
        scratch_shapes=[pltpu.VMEM((tt + 2 * GRID_W, D_MODEL), BF16),
                        pltpu.VMEM((tt, FFN_HIDDEN), BF16)],
        compiler_params=_cparams(("parallel", "parallel")),
        name="ffn",
    )(x1, x1, x1, mod3, g_ffn, w_up, conv_w9, conv_b, w_down, g_final)


def kernel(x, c, ctx, c_ctx, w_ada, b_ada, g_mix, w_in, s5_a_re, s5_a_im, s5_log_step, s5_b_re, s5_b_im, s5_c_re, s5_c_im, s5_d, s5_w_glu, s5_b_glu, sgu_ln_g, sgu_ln_b, sgu_w, sgu_b, w_proj_a, w_proj_b, b_gate, w_out, g_ffn, w_up, conv_w, conv_b, w_down, g_final):
    bsz = x.shape[0]
    row = lambda t: t.reshape(1, -1)

    ctx_row = bsz
    c_rows = jnp.zeros((SUBLANES, D_MODEL), F32).at[:bsz].set(c).at[ctx_row].set(c_ctx)
    mod3 = _ada_rows(c_rows, w_ada[0], b_ada[0]).reshape(SUBLANES, N_MOD, D_MODEL)

    w_in_b = w_in[0].astype(BF16)
    wsp = sgu_w[0].astype(BF16)
    wsp_pair = jnp.concatenate([wsp[0::2], wsp[1::2]], axis=-1)
    bsp_rows = jnp.repeat(sgu_b[0].T, SGU_GROUP_DIM, axis=1)

    u_pairs, ga, mb = _inproj(x, mod3, row(g_mix[0]), w_in_b, row(b_gate[0]), row(sgu_ln_g[0]),
                              row(sgu_ln_b[0]), wsp_pair, bsp_rows, w_proj_b[0].astype(BF16))
    uc_pairs = _ctxproj(ctx, mod3, ctx_row, row(g_mix[0]), w_in_b[:, :S5_WIDTH])

    w_state, w_y, coef = _s5_operators(
        s5_a_re[0], s5_a_im[0], s5_log_step[0], s5_b_re[0], s5_b_im[0], s5_c_re[0], s5_c_im[0],
        s5_d[0])
    y_pairs = _s5(u_pairs, uc_pairs, bsz, w_state, w_y, coef)

    x1 = _mix(y_pairs, x, ga, mb, mod3, s5_w_glu[0].astype(BF16), row(s5_b_glu[0]),
              w_proj_a[0].astype(BF16), w_out[0].astype(BF16))

    return _ffn(x1, mod3, row(g_ffn[0]), w_up[0].astype(BF16), conv_w[0].reshape(9, -1),
                row(conv_b[0]), w_down[0].astype(BF16), row(g_final))
```

```python
import math

import jax
import jax.numpy as jnp
from jax import lax
from jax.experimental import pallas as pl
from jax.experimental.pallas import tpu as pltpu

F32 = jnp.float32
BF16 = jnp.bfloat16

D_MODEL = 1024
GRID_W = 64
S5_WIDTH = 512
S5_GROUP = 16
S5_GROUPS = S5_WIDTH // S5_GROUP
S5_STATE = 64
SGU_WIDTH = 512
SGU_GROUPS = 8
SGU_GROUP_DIM = SGU_WIDTH // SGU_GROUPS
CHUNK = 128
FFN_HIDDEN = 2816
N_MOD = 6
EPS = 1e-6

LANES = 128
SUBLANES = 8
V7X_VMEM_LIMIT = 56 * 1024 * 1024

S5_CHUNK = 16
S5_PAIRS = S5_GROUPS // 2
PAIR_W = 2 * S5_CHUNK * S5_GROUP
S5_SLABS = S5_WIDTH // LANES
PAIRS_PER_SLAB = S5_PAIRS // S5_SLABS

TOK_TILE = 512
ROW_GROUPS = 2
FFN_TILE = 512
FFN_CHUNK = 256


def _cparams(sem):
    return pltpu.CompilerParams(dimension_semantics=sem, vmem_limit_bytes=V7X_VMEM_LIMIT)


def _const_spec(shape):
    zeros = (0,) * len(shape)
    return pl.BlockSpec(shape, lambda *_: zeros, pipeline_mode=pl.Buffered(1))


def _gelu_tanh(x):
    c = math.sqrt(2.0 / math.pi)
    return 0.5 * x * (1.0 + jnp.tanh(c * (x + 0.044715 * (x * x * x))))


def _modulated_rms(x, g, shift, scale):
    ms = jnp.mean(x * x, axis=-1, keepdims=True)
    return (x * lax.rsqrt(ms + EPS)) * (g * (1.0 + scale)) + shift


def _bdot(a, b):
    return jnp.dot(a, b, preferred_element_type=F32)


def _ada_kernel(c_ref, w_ref, b_ref, o_ref):
    c = c_ref[...]
    cs = c * jax.nn.sigmoid(c)
    o_ref[...] = jnp.dot(cs, w_ref[...], preferred_element_type=F32,
                         precision=lax.Precision.HIGHEST) + b_ref[...]


def _ada_rows(c_rows, w_ada, b_ada):
    n = w_ada.shape[1]
    bn = 1536
    return pl.pallas_call(
        _ada_kernel,
        grid=(n // bn,),
        in_specs=[_const_spec(c_rows.shape),
                  pl.BlockSpec((D_MODEL, bn), lambda j: (0, j)),
                  pl.BlockSpec((1, bn), lambda j: (0, j))],
        out_specs=pl.BlockSpec((c_rows.shape[0], bn), lambda j: (0, j)),
        out_shape=jax.ShapeDtypeStruct((c_rows.shape[0], n), F32),
        compiler_params=_cparams(("arbitrary",)),
        name="ada",
    )(c_rows, w_ada, b_ada.reshape(1, n))


def _swap_slab_and_piece(slabs):
    lane = lax.broadcasted_iota(jnp.int32, (1, LANES), 1)
    out = list(slabs)
    for k in range(3):
        width = S5_GROUP << k
        keep = ((lane >> (4 + k)) & 1) == 0
        nxt = list(out)
        for a0 in range(len(out)):
            if a0 & (1 << k):
                continue
            a1 = a0 | (1 << k)
            nxt[a0] = jnp.where(keep, out[a0], pltpu.roll(out[a1], width, 1))
            nxt[a1] = jnp.where(keep, pltpu.roll(out[a0], LANES - width, 1), out[a1])
        out = nxt
    return out


def _pair_lane_block(slab_idx):
    t_hi, g = divmod(slab_idx, SUBLANES)
    pp, gg = divmod(g, 2)
    return pp, gg * 2 + t_hi


def _store_pair_tiles(u, u_scr, up_ref, row0=0):
    n_chunks = u.shape[0] // S5_CHUNK
    chunks = slice(row0 // S5_CHUNK, row0 // S5_CHUNK + n_chunks)
    for j in range(S5_SLABS):
        u_scr[j, row0:row0 + u.shape[0], :] = u[:, j * LANES:(j + 1) * LANES]
    for j in range(S5_SLABS):
        slabs = _swap_slab_and_piece(
            [u_scr[j, pl.ds(row0 + t, n_chunks, stride=S5_CHUNK), :] for t in range(S5_CHUNK)])
        for a in range(S5_CHUNK):
            pp, blk = _pair_lane_block(a)
            up_ref[j * PAIRS_PER_SLAB + pp, chunks, blk * LANES:(blk + 1) * LANES] = (
                slabs[a].astype(BF16))


def _inproj_kernel(x_ref, mod_ref, g_ref, w_ref, bgate_ref, lng_ref, lnb_ref, wsp_ref, bsp_ref,
                   wpb_ref, u_ref, ga_ref, mb_ref, u_scr):
    m = mod_ref[0]
    tt = x_ref.shape[1]
    z0 = S5_WIDTH
    g0 = S5_WIDTH + 2 * SGU_WIDTH
    lane = lax.broadcasted_iota(jnp.int32, (CHUNK, LANES), 1)
    zero = jnp.zeros((CHUNK, LANES), BF16)

    groups = [slice(k * tt // ROW_GROUPS, (k + 1) * tt // ROW_GROUPS) for k in range(ROW_GROUPS)]
    proj = []
    for r in groups:
        h = _modulated_rms(x_ref[0, r, :], g_ref[...], m[0:1], m[1:2]).astype(BF16)
        proj.append(dict(
            u=_bdot(h, w_ref[:, 0:S5_WIDTH]),
            zv=_bdot(h, w_ref[:, z0 + SGU_WIDTH:z0 + 2 * SGU_WIDTH]),
            zu=_bdot(h, w_ref[:, z0:z0 + SGU_WIDTH]),
            ga=_bdot(h, w_ref[:, g0:g0 + D_MODEL]),
            gb=_bdot(h, w_ref[:, g0 + D_MODEL:g0 + 2 * D_MODEL])))

    for r, p in zip(groups, proj):
        _store_pair_tiles(p["u"], u_scr, u_ref, r.start)

        zv = _gelu_tanh(p["zv"])
        mu = jnp.mean(zv, axis=-1, keepdims=True)
        zc = zv - mu
        var = jnp.mean(zc * zc, axis=-1, keepdims=True)
        v = ((zc * lax.rsqrt(var + EPS)) * lng_ref[...] + lnb_ref[...]).astype(BF16)
        zu = _gelu_tanh(p["zu"])

        rows = []
        for c in range(zv.shape[0] // CHUNK):
            tiles = []
            for j in range(SGU_WIDTH // LANES):
                vj = v[c * CHUNK:(c + 1) * CHUNK, j * LANES:(j + 1) * LANES]
                rhs = jnp.concatenate([jnp.where(lane < SGU_GROUP_DIM, vj, zero),
                                       jnp.where(lane >= SGU_GROUP_DIM, vj, zero)], axis=0)
                tiles.append(_bdot(wsp_ref[j], rhs))
            rows.append(jnp.concatenate(tiles, axis=1) + bsp_ref[...])
        s = jnp.concatenate(rows, axis=0)
        pb = _bdot((zu * s).astype(BF16), wpb_ref[...])

        ga = jax.nn.sigmoid(p["ga"] + bgate_ref[:, 0:D_MODEL])
        gb = jax.nn.sigmoid(p["gb"] + bgate_ref[:, D_MODEL:2 * D_MODEL])
        ga_ref[0, r, :] = ga.astype(BF16)
        mb_ref[0, r, :] = (gb * pb).astype(BF16)


def _inproj(x, mod3, g_mix, w_in, b_gate, ln_g, ln_b, wsp_pair, bsp_rows, w_proj_b):
    bsz, length, _ = x.shape
    tt = TOK_TILE
    nt = length // tt
    tok = lambda w: pl.BlockSpec((1, tt, w), lambda b, i: (b, i, 0))
    return pl.pallas_call(
        _inproj_kernel,
        grid=(bsz, nt),
        in_specs=[tok(D_MODEL),
                  pl.BlockSpec((1, N_MOD, D_MODEL), lambda b, i: (b, 0, 0)),
                  _const_spec(g_mix.shape), _const_spec(w_in.shape), _const_spec(b_gate.shape),
                  _const_spec(ln_g.shape), _const_spec(ln_b.shape), _const_spec(wsp_pair.shape),
                  _const_spec(bsp_rows.shape), _const_spec(w_proj_b.shape)],
        out_specs=[pl.BlockSpec((S5_PAIRS, tt // S5_CHUNK, PAIR_W), lambda b, i: (0, b * nt + i, 0)),
                   tok(D_MODEL), tok(D_MODEL)],
        out_shape=[jax.ShapeDtypeStruct((S5_PAIRS, bsz * length // S5_CHUNK, PAIR_W), BF16),
                   jax.ShapeDtypeStruct((bsz, length, D_MODEL), BF16),
                   jax.ShapeDtypeStruct((bsz, length, D_MODEL), BF16)],
        scratch_shapes=[pltpu.VMEM((S5_SLABS, tt, LANES), F32)],
        compiler_params=_cparams(("parallel", "parallel")),
        name="inproj",
    )(x, mod3, g_mix, w_in, b_gate, ln_g, ln_b, wsp_pair, bsp_rows, w_proj_b)


def _ctxproj_kernel(x_ref, mod_ref, g_ref, w_ref, u_ref, u_scr):
    m = mod_ref[0]
    h = _modulated_rms(x_ref[0], g_ref[...], m[0:1], m[1:2]).astype(BF16)
    _store_pair_tiles(_bdot(h, w_ref[...]), u_scr, u_ref)


def _ctxproj(ctx, mod3, ctx_row, g_mix, w_in_u):
    bsz, length, _ = ctx.shape
    return pl.pallas_call(
        _ctxproj_kernel,
        grid=(bsz,),
        in_specs=[pl.BlockSpec((1, length, D_MODEL), lambda b: (b, 0, 0)),
                  pl.BlockSpec((1, N_MOD, D_MODEL), lambda b: (ctx_row, 0, 0)),
                  _const_spec(g_mix.shape), _const_spec(w_in_u.shape)],
        out_specs=pl.BlockSpec((S5_PAIRS, length // S5_CHUNK, PAIR_W), lambda b: (0, b, 0)),
        out_shape=jax.ShapeDtypeStruct((S5_PAIRS, bsz * length // S5_CHUNK, PAIR_W), BF16),
        scratch_shapes=[pltpu.VMEM((S5_SLABS, length, LANES), F32)],
        compiler_params=_cparams(("parallel",)),
        name="ctxproj",
    )(ctx, mod3, g_mix, w_in_u)


def _cmul(ar, ai, br, bi):
    return ar * br - ai * bi, ar * bi + ai * br


def _s5_ops_kernel(a_re_ref, a_im_ref, ls_ref, bt_re_ref, bt_im_ref, c_re_ref, c_im_ref, d_ref,
                   wst_ref, wy_ref, coef_ref):
    hp = lax.Precision.HIGHEST
    n_tok = S5_CHUNK
    blk = n_tok * S5_GROUP
    tok_of_lane = lax.broadcasted_iota(jnp.int32, (1, blk), 1) // S5_GROUP
    row8 = lax.broadcasted_iota(jnp.int32, (SUBLANES, 1), 0)
    diag = (lax.broadcasted_iota(jnp.int32, (blk, blk), 0)
            == lax.broadcasted_iota(jnp.int32, (blk, blk), 1))
    contract_n = (((1,), (1,)), ((), ()))

    for gg in range(2):
        rows_g = slice(gg * blk, (gg + 1) * blk)
        cols_g = slice(gg * blk, (gg + 1) * blk)
        cols_other = slice((1 - gg) * blk, (2 - gg) * blk)
        w_intra = jnp.where(diag, d_ref[0, gg], 0.0)
        for d in range(2):
            forward = d == 0
            a_re = a_re_ref[d, 0, gg]
            a_im = a_im_ref[d, 0, gg]
            dt = jnp.exp(ls_ref[d, 0, gg])
            mag = jnp.exp(a_re * dt)
            lam = (mag * jnp.cos(a_im * dt), mag * jnp.sin(a_im * dt))
            den = a_re * a_re + a_im * a_im
            p, q = lam[0] - 1.0, lam[1]
            k = ((p * a_re + q * a_im) / den, (q * a_re - p * a_im) / den)
            bbar = _cmul(*k, bt_re_ref[d, 0, gg], bt_im_ref[d, 0, gg])
            c = (c_re_ref[d, 0, gg], c_im_ref[d, 0, gg])
            pw = [(jnp.ones_like(p), jnp.zeros_like(p))]
            for _ in range(n_tok):
                pw.append(_cmul(*pw[-1], *lam))

            expo = [n_tok - 1 - i if forward else i for i in range(n_tok)]
            cols = [_cmul(*pw[e], *bbar) for e in expo]
            m_re = jnp.concatenate([t[0] for t in cols], axis=0)
            m_im = jnp.concatenate([t[1] for t in cols], axis=0)
            for comp, m in ((2 * d, m_re), (2 * d + 1, m_im)):
                lo = comp * LANES + gg * S5_STATE
                other = comp * LANES + (1 - gg) * S5_STATE
                wst_ref[0, rows_g, lo:lo + S5_STATE] = m.astype(BF16)
                wst_ref[0, rows_g, other:other + S5_STATE] = jnp.zeros((blk, S5_STATE), BF16)

            c_rep = [jnp.concatenate([t] * n_tok, axis=0) for t in c]
            kr = (lax.dot_general(m_re, c_rep[0], contract_n, precision=hp,
                                  preferred_element_type=F32)
                  - lax.dot_general(m_im, c_rep[1], contract_n, precision=hp,
                                    preferred_element_type=F32))
            toep = jnp.zeros((blk, blk), F32)
            for j in range(n_tok):
                s = S5_GROUP * (n_tok - 1 - j if forward else j)
                if s == 0:
                    moved = kr
                elif forward:
                    moved = jnp.concatenate([kr[s:], jnp.zeros((s, blk), F32)], axis=0)
                else:
                    moved = jnp.concatenate([jnp.zeros((s, blk), F32), kr[:blk - s]], axis=0)
                toep = jnp.where(tok_of_lane == j, moved, toep)
            w_intra = w_intra + toep

            expo = [j + 1 if forward else n_tok - j for j in range(n_tok)]
            rows = [_cmul(*pw[e], *c) for e in expo]
            r_re = jnp.concatenate([t[0] for t in rows], axis=0).T
            r_im = -jnp.concatenate([t[1] for t in rows], axis=0).T
            for comp, r in ((2 * d, r_re), (2 * d + 1, r_im)):
                r0 = PAIR_W + comp * LANES + gg * S5_STATE
                wy_ref[0, r0:r0 + S5_STATE, cols_g] = r.astype(BF16)
                wy_ref[0, r0:r0 + S5_STATE, cols_other] = jnp.zeros((S5_STATE, blk), BF16)

            lp = [(jnp.ones_like(p), jnp.zeros_like(p)), pw[n_tok]]
            for _ in range(SUBLANES - 1):
                lp.append(_cmul(*lp[-1], *pw[n_tok]))
            lanes = slice(d * LANES + gg * S5_STATE, d * LANES + (gg + 1) * S5_STATE)
            for kk, s in enumerate((1, 2, 4)):
                valid = (row8 >= s) if forward else (row8 + s < SUBLANES)
                coef_ref[0, 2 * kk, :, lanes] = jnp.where(valid, lp[s][0], 0.0)
                coef_ref[0, 2 * kk + 1, :, lanes] = jnp.where(valid, lp[s][1], 0.0)
            q_re = jnp.zeros((SUBLANES, S5_STATE), F32)
            q_im = jnp.zeros((SUBLANES, S5_STATE), F32)
            for r in range(SUBLANES):
                e = r + 1 if forward else SUBLANES - r
                q_re = jnp.where(row8 == r, lp[e][0], q_re)
                q_im = jnp.where(row8 == r, lp[e][1], q_im)
            coef_ref[0, 6, :, lanes] = q_re
            coef_ref[0, 7, :, lanes] = q_im

        wy_ref[0, rows_g, cols_g] = w_intra.astype(BF16)
        wy_ref[0, rows_g, cols_other] = jnp.zeros((blk, blk), BF16)


def _s5_operators(a_re, a_im, log_step, b_re, b_im, c_re, c_im, d_skip):
    grp = lambda t, *tail: t.reshape(2, S5_PAIRS, 2, *tail)
    row = lambda t: grp(t, 1, t.shape[-1])
    bt_re = grp(jnp.swapaxes(b_re, -1, -2), S5_GROUP, S5_STATE)
    bt_im = grp(jnp.swapaxes(b_im, -1, -2), S5_GROUP, S5_STATE)
    d_tiled = jnp.tile(d_skip.reshape(S5_PAIRS, 2, 1, S5_GROUP), (1, 1, 1, S5_CHUNK))
    per_pair = lambda *tail: pl.BlockSpec((2, 1, 2) + tail, lambda p: (0, p, 0) + (0,) * len(tail))
    out = lambda *tail: pl.BlockSpec((1,) + tail, lambda p: (p,) + (0,) * len(tail))
    return pl.pallas_call(
        _s5_ops_kernel,
        grid=(S5_PAIRS,),
        in_specs=[per_pair(1, S5_STATE), per_pair(1, S5_STATE), per_pair(1, 1),
                  per_pair(S5_GROUP, S5_STATE), per_pair(S5_GROUP, S5_STATE),
                  per_pair(S5_GROUP, S5_STATE), per_pair(S5_GROUP, S5_STATE),
                  pl.BlockSpec((1, 2, 1, S5_CHUNK * S5_GROUP), lambda p: (p, 0, 0, 0))],
        out_specs=[out(PAIR_W, PAIR_W), out(2 * PAIR_W, PAIR_W), out(8, SUBLANES, 2 * LANES)],
        out_shape=[jax.ShapeDtypeStruct((S5_PAIRS, PAIR_W, PAIR_W), BF16),
                   jax.ShapeDtypeStruct((S5_PAIRS, 2 * PAIR_W, PAIR_W), BF16),
                   jax.ShapeDtypeStruct((S5_PAIRS, 8, SUBLANES, 2 * LANES), F32)],
        compiler_params=_cparams(("parallel",)),
        name="s5_ops",
    )(row(a_re), row(a_im), grp(log_step, 1, 1), bt_re, bt_im,
      grp(c_re, S5_GROUP, S5_STATE), grp(c_im, S5_GROUP, S5_STATE), d_tiled)


def _s5_kernel(u_ref, uc_ref, wst_ref, wy_ref, coef_ref, y_ref, s_a, sin_a, s_b, sin_b):
    n_main = u_ref.shape[1]
    n_ctx = uc_ref.shape[1]
    n_rows = n_ctx + n_main + n_ctx
    half = PAIRS_PER_SLAB // 2
    halves = [(list(range(0, half)), s_a, sin_a), (list(range(half, PAIRS_PER_SLAB)), s_b, sin_b)]

    for pairs, s_scr, _ in halves:
        for k, pp in enumerate(pairs):
            s_ctx = _bdot(uc_ref[pp], wst_ref[pp])
            s_scr[k, 0:n_ctx] = s_ctx
            s_scr[k, n_ctx:n_ctx + n_main] = _bdot(u_ref[pp], wst_ref[pp])
            s_scr[k, n_ctx + n_main:] = s_ctx
    y_intra = [_bdot(u_ref[pp], wy_ref[pp, 0:PAIR_W]) for pp in range(PAIRS_PER_SLAB)]

    row = lax.broadcasted_iota(jnp.int32, (SUBLANES, LANES), 0)
    n_groups = n_rows // SUBLANES
    n_iter = (n_ctx + n_main) // SUBLANES

    def group(s_scr, sin_scr, k, pp, base, c_re, c_im, lane0, forward):
        rows = pl.ds(base, SUBLANES)
        re_l = slice(lane0, lane0 + LANES)
        im_l = slice(lane0 + LANES, lane0 + 2 * LANES)
        cf = slice(0, LANES) if forward else slice(LANES, 2 * LANES)
        xr = s_scr[k, rows, re_l]
        xi = s_scr[k, rows, im_l]
        for step, sh in enumerate((1, 2, 4)):
            ar = coef_ref[pp, 2 * step, :, cf]
            ai = coef_ref[pp, 2 * step + 1, :, cf]
            amt = sh if forward else SUBLANES - sh
            sr = pltpu.roll(xr, amt, 0)
            si = pltpu.roll(xi, amt, 0)
            xr, xi = xr + (ar * sr - ai * si), xi + (ar * si + ai * sr)
        qr = coef_ref[pp, 6, :, cf]
        qi = coef_ref[pp, 7, :, cf]
        hr = xr + (qr * c_re - qi * c_im)
        hi = xi + (qr * c_im + qi * c_re)
        edge = 0 if forward else SUBLANES - 1
        amt = 1 if forward else SUBLANES - 1
        sin_scr[k, rows, re_l] = jnp.where(row == edge, c_re, pltpu.roll(hr, amt, 0))
        sin_scr[k, rows, im_l] = jnp.where(row == edge, c_im, pltpu.roll(hi, amt, 0))
        last = SUBLANES - 1 if forward else 0
        return (jnp.broadcast_to(hr[last:last + 1], (SUBLANES, LANES)),
                jnp.broadcast_to(hi[last:last + 1], (SUBLANES, LANES)))

    zero = jnp.zeros((SUBLANES, LANES), F32)
    main_rows = slice(n_ctx, n_ctx + n_main)
    for pairs, s_scr, sin_scr in halves:
        carry = [zero] * (4 * len(pairs))
        for g in range(n_iter):
            new = []
            for k, pp in enumerate(pairs):
                f_re, f_im, b_re, b_im = carry[4 * k:4 * k + 4]
                f_re, f_im = group(s_scr, sin_scr, k, pp, g * SUBLANES, f_re, f_im, 0, True)
                b_re, b_im = group(s_scr, sin_scr, k, pp, (n_groups - 1 - g) * SUBLANES,
                                   b_re, b_im, 2 * LANES, False)
                new += [f_re, f_im, b_re, b_im]
            carry = new
        for k, pp in enumerate(pairs):
            y_ref[pp] = y_intra[pp] + _bdot(sin_scr[k, main_rows, :].astype(BF16),
                                            wy_ref[pp, PAIR_W:2 * PAIR_W])


def _s5(u_pairs, uc_pairs, bsz, w_state, w_y, coef):
    n_main = u_pairs.shape[1] // bsz
    n_ctx = uc_pairs.shape[1] // bsz
    n_rows = n_main + 2 * n_ctx
    pps = PAIRS_PER_SLAB
    rows = lambda n: pl.BlockSpec((pps, n, PAIR_W), lambda j, b: (j, b, 0))
    return pl.pallas_call(
        _s5_kernel,
        grid=(S5_SLABS, bsz),
        in_specs=[rows(n_main), rows(n_ctx),
                  pl.BlockSpec((pps, PAIR_W, PAIR_W), lambda j, b: (j, 0, 0)),
                  pl.BlockSpec((pps, 2 * PAIR_W, PAIR_W), lambda j, b: (j, 0, 0)),
                  pl.BlockSpec((pps, 8, SUBLANES, 2 * LANES), lambda j, b: (j, 0, 0, 0))],
        out_specs=rows(n_main),
        out_shape=jax.ShapeDtypeStruct(u_pairs.shape, F32),
        scratch_shapes=[pltpu.VMEM((pps // 2, n_rows, PAIR_W), F32)] * 4,
        compiler_params=_cparams(("arbitrary", "arbitrary")),
        name="s5",
    )(u_pairs, uc_pairs, w_state, w_y, coef)


def _mix_kernel(y_ref, x_ref, ga_ref, mb_ref, mod_ref, wglu_ref, bglu_ref, wpa_ref, wout_ref, o_ref,
                y_scr):
    m = mod_ref[0]
    tt = x_ref.shape[1]
    groups = [slice(k * tt // ROW_GROUPS, (k + 1) * tt // ROW_GROUPS) for k in range(ROW_GROUPS)]
    cpg = y_ref.shape[1] // ROW_GROUPS
    for k, r in enumerate(groups):
        for j in range(S5_SLABS):
            slabs = []
            for a in range(S5_CHUNK):
                pp, blk = _pair_lane_block(a)
                slabs.append(y_ref[j * PAIRS_PER_SLAB + pp, k * cpg:(k + 1) * cpg,
                                   blk * LANES:(blk + 1) * LANES])
            slabs = _swap_slab_and_piece(slabs)
            for t in range(S5_CHUNK):
                y_scr[j, pl.ds(r.start + t, cpg, stride=S5_CHUNK), :] = slabs[t]
    ya = [_gelu_tanh(jnp.concatenate([y_scr[j, r, :] for j in range(S5_SLABS)], axis=1))
          for r in groups]
    gate = [_bdot(t.astype(BF16), wglu_ref[...]) for t in ya]
    glu = [t * jax.nn.sigmoid(g + bglu_ref[...]) for t, g in zip(ya, gate)]
    pa = [_bdot(t.astype(BF16), wpa_ref[...]) for t in glu]
    merged = [(ga_ref[0, r, :].astype(F32) * p + mb_ref[0, r, :].astype(F32)).astype(BF16)
              for r, p in zip(groups, pa)]
    out = [_bdot(t, wout_ref[...]) for t in merged]
    for r, t in zip(groups, out):
        o_ref[0, r, :] = x_ref[0, r, :] + m[2:3] * t


def _mix(y_pairs, x, ga, mb, mod3, w_glu, b_glu, w_proj_a, w_out):
    bsz, length, _ = x.shape
    tt = TOK_TILE
    nt = length // tt
    tok = lambda w: pl.BlockSpec((1, tt, w), lambda b, i: (b, i, 0))
    return pl.pallas_call(
        _mix_kernel,
        grid=(bsz, nt),
        in_specs=[pl.BlockSpec((S5_PAIRS, tt // S5_CHUNK, PAIR_W), lambda b, i: (0, b * nt + i, 0)),
                  tok(D_MODEL), tok(D_MODEL), tok(D_MODEL),
                  pl.BlockSpec((1, N_MOD, D_MODEL), lambda b, i: (b, 0, 0)),
                  _const_spec(w_glu.shape), _const_spec(b_glu.shape),
                  _const_spec(w_proj_a.shape), _const_spec(w_out.shape)],
        out_specs=tok(D_MODEL),
        out_shape=jax.ShapeDtypeStruct(x.shape, F32),
        scratch_shapes=[pltpu.VMEM((S5_SLABS, tt, LANES), F32)],
        compiler_params=_cparams(("parallel", "parallel")),
        name="mix",
    )(y_pairs, x, ga, mb, mod3, w_glu, b_glu, w_proj_a, w_out)


def _conv_taps(cw_ref, cb_ref, cols):
    width = cols.stop - cols.start
    taps = [jnp.broadcast_to(cw_ref[k:k + 1, cols].astype(BF16), (GRID_W, width)) for k in range(9)]
    return taps, jnp.broadcast_to(cb_ref[:, cols].astype(BF16), (GRID_W, width))


def _token_neighbours(u):
    n, ch = u.shape
    shape4 = (n // GRID_W, GRID_W // SUBLANES, SUBLANES, ch)
    sub = lax.broadcasted_iota(jnp.int32, (1, 1, SUBLANES, 1), 2)
    zero = jnp.zeros((shape4[0], 1, SUBLANES, ch), F32)
    u4 = u.reshape(shape4)
    down = pltpu.roll(u4, 1, 2)
    left = jnp.where(sub == 0, jnp.concatenate([zero, down[:, :-1]], axis=1), down)
    up = pltpu.roll(u4, SUBLANES - 1, 2)
    right = jnp.where(sub == SUBLANES - 1, jnp.concatenate([up[:, 1:], zero], axis=1), up)
    return left.reshape(n, ch).astype(BF16), u.astype(BF16), right.reshape(n, ch).astype(BF16)


def _grid_conv_row(nbrs, taps, bias, rb):
    lo = rb * GRID_W
    part = []
    for dr in range(3):
        rows = slice(lo + dr * GRID_W, lo + (dr + 1) * GRID_W)
        left, here, right = (nbrs[dc][rows] * taps[3 * dr + dc] for dc in range(3))
        part.append((left + here) + right)
    return (part[0] + part[1]) + (part[2] + bias)


def _ffn_kernel(top_ref, x_ref, bot_ref, mod_ref, g_ref, wup_ref, cw_ref, cb_ref, wdn_ref, gf_ref,
                o_ref, h_scr, act_scr):
    i = pl.program_id(1)
    n_i = pl.num_programs(1)
    m = mod_ref[0]
    g = g_ref[...]
    tt = x_ref.shape[1]
    x = x_ref[0]

    def norm(t):
        return _modulated_rms(t, g, m[3:4], m[4:5])

    h_scr[0:GRID_W] = jnp.where(i > 0, norm(top_ref[0]), 0.0).astype(BF16)
    h_scr[GRID_W:GRID_W + tt] = norm(x).astype(BF16)
    h_scr[GRID_W + tt:] = jnp.where(i < n_i - 1, norm(bot_ref[0]), 0.0).astype(BF16)

    h = h_scr[...]

    fc = FFN_CHUNK
    for j in range(FFN_HIDDEN // fc):
        gs = slice(j * fc, (j + 1) * fc)
        vs = slice(FFN_HIDDEN + j * fc, FFN_HIDDEN + (j + 1) * fc)
        ug = _token_neighbours(_bdot(h, wup_ref[:, gs]))
        uv = _token_neighbours(_bdot(h, wup_ref[:, vs]))
        g_taps, g_bias = _conv_taps(cw_ref, cb_ref, gs)
        v_taps, v_bias = _conv_taps(cw_ref, cb_ref, vs)
        for rb in range(tt // GRID_W):
            cg = _grid_conv_row(ug, g_taps, g_bias, rb)
            cv = _grid_conv_row(uv, v_taps, v_bias, rb)
            act_scr[rb * GRID_W:(rb + 1) * GRID_W, gs] = (cg * jax.nn.sigmoid(cg)) * cv

    x2 = x + m[5:6] * _bdot(act_scr[...], wdn_ref[...])
    ms = jnp.mean(x2 * x2, axis=-1, keepdims=True)
    o_ref[0] = (x2 * lax.rsqrt(ms + EPS)) * gf_ref[...]


def _ffn(x1, mod3, g_ffn, w_up, conv_w9, conv_b, w_down, g_final):
    bsz, length, _ = x1.shape
    tt = FFN_TILE
    rows_per_tile = tt // GRID_W
    n_rows = length // GRID_W
    halo = lambda fn: pl.BlockSpec((1, GRID_W, D_MODEL), fn)
    return pl.pallas_call(
        _ffn_kernel,
        grid=(bsz, length // tt),
        in_specs=[halo(lambda b, i: (b, jnp.maximum(i * rows_per_tile - 1, 0), 0)),
                  pl.BlockSpec((1, tt, D_MODEL), lambda b, i: (b, i, 0)),
                  halo(lambda b, i: (b, jnp.minimum((i + 1) * rows_per_tile, n_rows - 1), 0)),
                  pl.BlockSpec((1, N_MOD, D_MODEL), lambda b, i: (b, 0, 0)),
                  _const_spec(g_ffn.shape), _const_spec(w_up.shape), _const_spec(conv_w9.shape),
                  _const_spec(conv_b.shape), _const_spec(w_down.shape), _const_spec(g_final.shape)],
        out_specs=pl.BlockSpec((1, tt, D_MODEL), lambda b, i: (b, i, 0)),
        out_shape=jax.ShapeDtypeStruct(x1.shape, F32),
        scratch_shapes=[pltpu.VMEM((tt + 2 * GRID_W, D_MODEL), BF16),
                        pltpu.VMEM((tt, FFN_HIDDEN), BF16)],
        compiler_params=_cparams(("parallel", "parallel")),
        name="ffn",
    )(x1, x1, x1, mod3, g_ffn, w_up, conv_w9, conv_b, w_down, g_final)


def kernel(x, c, ctx, c_ctx, w_ada, b_ada, g_mix, w_in, s5_a_re, s5_a_im, s5_log_step, s5_b_re, s5_b_im, s5_c_re, s5_c_im, s5_d, s5_w_glu, s5_b_glu, sgu_ln_g, sgu_ln_b, sgu_w, sgu_b, w_proj_a, w_proj_b, b_gate, w_out, g_ffn, w_up, conv_w, conv_b, w_down, g_final):
    bsz = x.shape[0]
    row = lambda t: t.reshape(1, -1)

    ctx_row = bsz
    c_rows = jnp.zeros((SUBLANES, D_MODEL), F32).at[:bsz].set(c).at[ctx_row].set(c_ctx)
    mod3 = _ada_rows(c_rows, w_ada[0], b_ada[0]).reshape(SUBLANES, N_MOD, D_MODEL)

    w_in_b = w_in[0].astype(BF16)
    wsp = sgu_w[0].astype(BF16)
    wsp_pair = jnp.concatenate([wsp[0::2], wsp[1::2]], axis=-1)
    bsp_rows = jnp.repeat(sgu_b[0].T, SGU_GROUP_DIM, axis=1)

    u_pairs, ga, mb = _inproj(x, mod3, row(g_mix[0]), w_in_b, row(b_gate[0]), row(sgu_ln_g[0]),
                              row(sgu_ln_b[0]), wsp_pair, bsp_rows, w_proj_b[0].astype(BF16))
    uc_pairs = _ctxproj(ctx, mod3, ctx_row, row(g_mix[0]), w_in_b[:, :S5_WIDTH])

    w_state, w_y, coef = _s5_operators(
        s5_a_re[0], s5_a_im[0], s5_log_step[0], s5_b_re[0], s5_b_im[0], s5_c_re[0], s5_c_im[0],
        s5_d[0])
    y_pairs = _s5(u_pairs, uc_pairs, bsz, w_state, w_y, coef)

    x1 = _mix(y_pairs, x, ga, mb, mod3, s5_w_glu[0].astype(BF16), row(s5_b_glu[0]),
              w_proj_a[0].astype(BF16), w_out[0].astype(BF16))

    return _ffn(x1, mod3, row(g_ffn[0]), w_up[0].astype(BF16), conv_w[0].reshape(9, -1),
                row(conv_b[0]), w_down[0].astype(BF16), row(g_final))
```

```python
import math

import jax
import jax.numpy as jnp
from jax import lax
from jax.experimental import pallas as pl
from jax.experimental.pallas import tpu as pltpu

F32 = jnp.float32
BF16 = jnp.bfloat16

D_MODEL = 1024
GRID_W = 64
S5_WIDTH = 512
S5_GROUP = 16
S5_GROUPS = S5_WIDTH // S5_GROUP
S5_STATE = 64
SGU_WIDTH = 512
SGU_GROUPS = 8
SGU_GROUP_DIM = SGU_WIDTH // SGU_GROUPS
CHUNK = 128
FFN_HIDDEN = 2816
N_MOD = 6
EPS = 1e-6

LANES = 128
SUBLANES = 8
V7X_VMEM_LIMIT = 56 * 1024 * 1024

S5_CHUNK = 16
S5_PAIRS = S5_GROUPS // 2
PAIR_W = 2 * S5_CHUNK * S5_GROUP
S5_SLABS = S5_WIDTH // LANES
PAIRS_PER_SLAB = S5_PAIRS // S5_SLABS

TOK_TILE = 512
ROW_GROUPS = 2
FFN_TILE = 512
FFN_CHUNK = 256


def _cparams(sem):
    return pltpu.CompilerParams(dimension_semantics=sem, vmem_limit_bytes=V7X_VMEM_LIMIT)


def _const_spec(shape):
    zeros = (0,) * len(shape)
    return pl.BlockSpec(shape, lambda *_: zeros, pipeline_mode=pl.Buffered(1))


def _gelu_tanh(x):
    c = math.sqrt(2.0 / math.pi)
    return 0.5 * x * (1.0 + jnp.tanh(c * (x + 0.044715 * (x * x * x))))


def _modulated_rms(x, g, shift, scale):
    ms = jnp.mean(x * x, axis=-1, keepdims=True)
    return (x * lax.rsqrt(ms + EPS)) * (g * (1.0 + scale)) + shift


def _bdot(a, b):
    return jnp.dot(a, b, preferred_element_type=F32)


def _ada_kernel(c_ref, w_ref, b_ref, o_ref):
    c = c_ref[...]
    cs = c * jax.nn.sigmoid(c)
    o_ref[...] = jnp.dot(cs, w_ref[...], preferred_element_type=F32,
                         precision=lax.Precision.HIGHEST) + b_ref[...]


def _ada_rows(c_rows, w_ada, b_ada):
    n = w_ada.shape[1]
    bn = 1536
    return pl.pallas_call(
        _ada_kernel,
        grid=(n // bn,),
        in_specs=[_const_spec(c_rows.shape),
                  pl.BlockSpec((D_MODEL, bn), lambda j: (0, j)),
                  pl.BlockSpec((1, bn), lambda j: (0, j))],
        out_specs=pl.BlockSpec((c_rows.shape[0], bn), lambda j: (0, j)),
        out_shape=jax.ShapeDtypeStruct((c_rows.shape[0], n), F32),
        compiler_params=_cparams(("arbitrary",)),
        name="ada",
    )(c_rows, w_ada, b_ada.reshape(1, n))


def _swap_slab_and_piece(slabs):
    lane = lax.broadcasted_iota(jnp.int32, (1, LANES), 1)
    out = list(slabs)
    for k in range(3):
        width = S5_GROUP << k
        keep = ((lane >> (4 + k)) & 1) == 0
        nxt = list(out)
        for a0 in range(len(out)):
            if a0 & (1 << k):
                continue
            a1 = a0 | (1 << k)
            nxt[a0] = jnp.where(keep, out[a0], pltpu.roll(out[a1], width, 1))
            nxt[a1] = jnp.where(keep, pltpu.roll(out[a0], LANES - width, 1), out[a1])
        out = nxt
    return out


def _pair_lane_block(slab_idx):
    t_hi, g = divmod(slab_idx, SUBLANES)
    pp, gg = divmod(g, 2)
    return pp, gg * 2 + t_hi


def _store_pair_tiles(u, u_scr, up_ref, row0=0):
    n_chunks = u.shape[0] // S5_CHUNK
    chunks = slice(row0 // S5_CHUNK, row0 // S5_CHUNK + n_chunks)
    for j in range(S5_SLABS):
        u_scr[j, row0:row0 + u.shape[0], :] = u[:, j * LANES:(j + 1) * LANES]
    for j in range(S5_SLABS):
        slabs = _swap_slab_and_piece(
            [u_scr[j, pl.ds(row0 + t, n_chunks, stride=S5_CHUNK), :] for t in range(S5_CHUNK)])
        for a in range(S5_CHUNK):
            pp, blk = _pair_lane_block(a)
            up_ref[j * PAIRS_PER_SLAB + pp, chunks, blk * LANES:(blk + 1) * LANES] = (
                slabs[a].astype(BF16))


def _inproj_kernel(x_ref, mod_ref, g_ref, w_ref, bgate_ref, lng_ref, lnb_ref, wsp_ref, bsp_ref,
                   wpb_ref, u_ref, ga_ref, mb_ref, u_scr):
    m = mod_ref[0]
    tt = x_ref.shape[1]
    z0 = S5_WIDTH
    g0 = S5_WIDTH + 2 * SGU_WIDTH
    lane = lax.broadcasted_iota(jnp.int32, (CHUNK, LANES), 1)
    zero = jnp.zeros((CHUNK, LANES), BF16)

    groups = [slice(k * tt // ROW_GROUPS, (k + 1) * tt // ROW_GROUPS) for k in range(ROW_GROUPS)]
    proj = []
    for r in groups:
        h = _modulated_rms(x_ref[0, r, :], g_ref[...], m[0:1], m[1:2]).astype(BF16)
        proj.append(dict(
            u=_bdot(h, w_ref[:, 0:S5_WIDTH]),
            zv=_bdot(h, w_ref[:, z0 + SGU_WIDTH:z0 + 2 * SGU_WIDTH]),
            zu=_bdot(h, w_ref[:, z0:z0 + SGU_WIDTH]),
            ga=_bdot(h, w_ref[:, g0:g0 + D_MODEL]),
            gb=_bdot(h, w_ref[:, g0 + D_MODEL:g0 + 2 * D_MODEL])))

    for r, p in zip(groups, proj):
        _store_pair_tiles(p["u"], u_scr, u_ref, r.start)

        zv = _gelu_tanh(p["zv"])
        mu = jnp.mean(zv, axis=-1, keepdims=True)
        zc = zv - mu
        var = jnp.mean(zc * zc, axis=-1, keepdims=True)
        v = ((zc * lax.rsqrt(var + EPS)) * lng_ref[...] + lnb_ref[...]).astype(BF16)
        zu = _gelu_tanh(p["zu"])

        rows = []
        for c in range(zv.shape[0] // CHUNK):
            tiles = []
            for j in range(SGU_WIDTH // LANES):
                vj = v[c * CHUNK:(c + 1) * CHUNK, j * LANES:(j + 1) * LANES]
                rhs = jnp.concatenate([jnp.where(lane < SGU_GROUP_DIM, vj, zero),
                                       jnp.where(lane >= SGU_GROUP_DIM, vj, zero)], axis=0)
                tiles.append(_bdot(wsp_ref[j], rhs))
            rows.append(jnp.concatenate(tiles, axis=1) + bsp_ref[...])
        s = jnp.concatenate(rows, axis=0)
        pb = _bdot((zu * s).astype(BF16), wpb_ref[...])

        ga = jax.nn.sigmoid(p["ga"] + bgate_ref[:, 0:D_MODEL])
        gb = jax.nn.sigmoid(p["gb"] + bgate_ref[:, D_MODEL:2 * D_MODEL])
        ga_ref[0, r, :] = ga.astype(BF16)
        mb_ref[0, r, :] = (gb * pb).astype(BF16)


def _inproj(x, mod3, g_mix, w_in, b_gate, ln_g, ln_b, wsp_pair, bsp_rows, w_proj_b):
    bsz, length, _ = x.shape
    tt = TOK_TILE
    nt = length // tt
    tok = lambda w: pl.BlockSpec((1, tt, w), lambda b, i: (b, i, 0))
    return pl.pallas_call(
        _inproj_kernel,
        grid=(bsz, nt),
        in_specs=[tok(D_MODEL),
                  pl.BlockSpec((1, N_MOD, D_MODEL), lambda b, i: (b, 0, 0)),
                  _const_spec(g_mix.shape), _const_spec(w_in.shape), _const_spec(b_gate.shape),
                  _const_spec(ln_g.shape), _const_spec(ln_b.shape), _const_spec(wsp_pair.shape),
                  _const_spec(bsp_rows.shape), _const_spec(w_proj_b.shape)],
        out_specs=[pl.BlockSpec((S5_PAIRS, tt // S5_CHUNK, PAIR_W), lambda b, i: (0, b * nt + i, 0)),
                   tok(D_MODEL), tok(D_MODEL)],
        out_shape=[jax.ShapeDtypeStruct((S5_PAIRS, bsz * length // S5_CHUNK, PAIR_W), BF16),
                   jax.ShapeDtypeStruct((bsz, length, D_MODEL), BF16),
                   jax.ShapeDtypeStruct((bsz, length, D_MODEL), BF16)],
        scratch_shapes=[pltpu.VMEM((S5_SLABS, tt, LANES), F32)],
        compiler_params=_cparams(("parallel", "parallel")),
        name="inproj",
    )(x, mod3, g_mix, w_in, b_gate, ln_g, ln_b, wsp_pair, bsp_rows, w_proj_b)


def _ctxproj_kernel(x_ref, mod_ref, g_ref, w_ref, u_ref, u_scr):
    m = mod_ref[0]
    h = _modulated_rms(x_ref[0], g_ref[...], m[0:1], m[1:2]).astype(BF16)
    _store_pair_tiles(_bdot(h, w_ref[...]), u_scr, u_ref)


def _ctxproj(ctx, mod3, ctx_row, g_mix, w_in_u):
    bsz, length, _ = ctx.shape
    return pl.pallas_call(
        _ctxproj_kernel,
        grid=(bsz,),
        in_specs=[pl.BlockSpec((1, length, D_MODEL), lambda b: (b, 0, 0)),
                  pl.BlockSpec((1, N_MOD, D_MODEL), lambda b: (ctx_row, 0, 0)),
                  _const_spec(g_mix.shape), _const_spec(w_in_u.shape)],
        out_specs=pl.BlockSpec((S5_PAIRS, length // S5_CHUNK, PAIR_W), lambda b: (0, b, 0)),
        out_shape=jax.ShapeDtypeStruct((S5_PAIRS, bsz * length // S5_CHUNK, PAIR_W), BF16),
        scratch_shapes=[pltpu.VMEM((S5_SLABS, length, LANES), F32)],
        compiler_params=_cparams(("parallel",)),
        name="ctxproj",
    )(ctx, mod3, g_mix, w_in_u)


def _cmul(ar, ai, br, bi):
    return ar * br - ai * bi, ar * bi + ai * br


def _s5_ops_kernel(a_re_ref, a_im_ref, ls_ref, bt_re_ref, bt_im_ref, c_re_ref, c_im_ref, d_ref,
                   wst_ref, wy_ref, coef_ref):
    hp = lax.Precision.HIGHEST
    n_tok = S5_CHUNK
    blk = n_tok * S5_GROUP
    tok_of_lane = lax.broadcasted_iota(jnp.int32, (1, blk), 1) // S5_GROUP
    row8 = lax.broadcasted_iota(jnp.int32, (SUBLANES, 1), 0)
    diag = (lax.broadcasted_iota(jnp.int32, (blk, blk), 0)
            == lax.broadcasted_iota(jnp.int32, (blk, blk), 1))
    contract_n = (((1,), (1,)), ((), ()))

    for gg in range(2):
        rows_g = slice(gg * blk, (gg + 1) * blk)
        cols_g = slice(gg * blk, (gg + 1) * blk)
        cols_other = slice((1 - gg) * blk, (2 - gg) * blk)
        w_intra = jnp.where(diag, d_ref[0, gg], 0.0)
        for d in range(2):
            forward = d == 0
            a_re = a_re_ref[d, 0, gg]
            a_im = a_im_ref[d, 0, gg]
            dt = jnp.exp(ls_ref[d, 0, gg])
            mag = jnp.exp(a_re * dt)
            lam = (mag * jnp.cos(a_im * dt), mag * jnp.sin(a_im * dt))
            den = a_re * a_re + a_im * a_im
            p, q = lam[0] - 1.0, lam[1]
            k = ((p * a_re + q * a_im) / den, (q * a_re - p * a_im) / den)
            bbar = _cmul(*k, bt_re_ref[d, 0, gg], bt_im_ref[d, 0, gg])
            c = (c_re_ref[d, 0, gg], c_im_ref[d, 0, gg])
            pw = [(jnp.ones_like(p), jnp.zeros_like(p))]
            for _ in range(n_tok):
                pw.append(_cmul(*pw[-1], *lam))

            expo = [n_tok - 1 - i if forward else i for i in range(n_tok)]
            cols = [_cmul(*pw[e], *bbar) for e in expo]
            m_re = jnp.concatenate([t[0] for t in cols], axis=0)
            m_im = jnp.concatenate([t[1] for t in cols], axis=0)
            for comp, m in ((2 * d, m_re), (2 * d + 1, m_im)):
                lo = comp * LANES + gg * S5_STATE
                other = comp * LANES + (1 - gg) * S5_STATE
                wst_ref[0, rows_g, lo:lo + S5_STATE] = m.astype(BF16)
                wst_ref[0, rows_g, other:other + S5_STATE] = jnp.zeros((blk, S5_STATE), BF16)

            c_rep = [jnp.concatenate([t] * n_tok, axis=0) for t in c]
            kr = (lax.dot_general(m_re, c_rep[0], contract_n, precision=hp,
                                  preferred_element_type=F32)
                  - lax.dot_general(m_im, c_rep[1], contract_n, precision=hp,
                                    preferred_element_type=F32))
            toep = jnp.zeros((blk, blk), F32)
            for j in range(n_tok):
                s = S5_GROUP * (n_tok - 1 - j if forward else j)
                if s == 0:
                    moved = kr
                elif forward:
                    moved = jnp.concatenate([kr[s:], jnp.zeros((s, blk), F32)], axis=0)
                else:
                    moved = jnp.concatenate([jnp.zeros((s, blk), F32), kr[:blk - s]], axis=0)
                toep = jnp.where(tok_of_lane == j, moved, toep)
            w_intra = w_intra + toep

            expo = [j + 1 if forward else n_tok - j for j in range(n_tok)]
            rows = [_cmul(*pw[e], *c) for e in expo]
            r_re = jnp.concatenate([t[0] for t in rows], axis=0).T
            r_im = -jnp.concatenate([t[1] for t in rows], axis=0).T
            for comp, r in ((2 * d, r_re), (2 * d + 1, r_im)):
                r0 = PAIR_W + comp * LANES + gg * S5_STATE
                wy_ref[0, r0:r0 + S5_STATE, cols_g] = r.astype(BF16)
                wy_ref[0, r0:r0 + S5_STATE, cols_other] = jnp.zeros((S5_STATE, blk), BF16)

            lp = [(jnp.ones_like(p), jnp.zeros_like(p)), pw[n_tok]]
            for _ in range(SUBLANES - 1):
                lp.append(_cmul(*lp[-1], *pw[n_tok]))
            lanes = slice(d * LANES + gg * S5_STATE, d * LANES + (gg + 1) * S5_STATE)
            for kk, s in enumerate((1, 2, 4)):
                valid = (row8 >= s) if forward else (row8 + s < SUBLANES)
                coef_ref[0, 2 * kk, :, lanes] = jnp.where(valid, lp[s][0], 0.0)
                coef_ref[0, 2 * kk + 1, :, lanes] = jnp.where(valid, lp[s][1], 0.0)
            q_re = jnp.zeros((SUBLANES, S5_STATE), F32)
            q_im = jnp.zeros((SUBLANES, S5_STATE), F32)
            for r in range(SUBLANES):
                e = r + 1 if forward else SUBLANES - r
                q_re = jnp.where(row8 == r, lp[e][0], q_re)
                q_im = jnp.where(row8 == r, lp[e][1], q_im)
            coef_ref[0, 6, :, lanes] = q_re
            coef_ref[0, 7, :, lanes] = q_im

        wy_ref[0, rows_g, cols_g] = w_intra.astype(BF16)
        wy_ref[0, rows_g, cols_other] = jnp.zeros((blk, blk), BF16)


def _s5_operators(a_re, a_im, log_step, b_re, b_im, c_re, c_im, d_skip):
    grp = lambda t, *tail: t.reshape(2, S5_PAIRS, 2, *tail)
    row = lambda t: grp(t, 1, t.shape[-1])
    bt_re = grp(jnp.swapaxes(b_re, -1, -2), S5_GROUP, S5_STATE)
    bt_im = grp(jnp.swapaxes(b_im, -1, -2), S5_GROUP, S5_STATE)
    d_tiled = jnp.tile(d_skip.reshape(S5_PAIRS, 2, 1, S5_GROUP), (1, 1, 1, S5_CHUNK))
    per_pair = lambda *tail: pl.BlockSpec((2, 1, 2) + tail, lambda p: (0, p, 0) + (0,) * len(tail))
    out = lambda *tail: pl.BlockSpec((1,) + tail, lambda p: (p,) + (0,) * len(tail))
    return pl.pallas_call(
        _s5_ops_kernel,
        grid=(S5_PAIRS,),
        in_specs=[per_pair(1, S5_STATE), per_pair(1, S5_STATE), per_pair(1, 1),
                  per_pair(S5_GROUP, S5_STATE), per_pair(S5_GROUP, S5_STATE),
                  per_pair(S5_GROUP, S5_STATE), per_pair(S5_GROUP, S5_STATE),
                  pl.BlockSpec((1, 2, 1, S5_CHUNK * S5_GROUP), lambda p: (p, 0, 0, 0))],
        out_specs=[out(PAIR_W, PAIR_W), out(2 * PAIR_W, PAIR_W), out(8, SUBLANES, 2 * LANES)],
        out_shape=[jax.ShapeDtypeStruct((S5_PAIRS, PAIR_W, PAIR_W), BF16),
                   jax.ShapeDtypeStruct((S5_PAIRS, 2 * PAIR_W, PAIR_W), BF16),
                   jax.ShapeDtypeStruct((S5_PAIRS, 8, SUBLANES, 2 * LANES), F32)],
        compiler_params=_cparams(("parallel",)),
        name="s5_ops",
    )(row(a_re), row(a_im), grp(log_step, 1, 1), bt_re, bt_im,
      grp(c_re, S5_GROUP, S5_STATE), grp(c_im, S5_GROUP, S5_STATE), d_tiled)


def _s5_kernel(u_ref, uc_ref, wst_ref, wy_ref, coef_ref, y_ref, s_a, sin_a, s_b, sin_b):
    n_main = u_ref.shape[1]
    n_ctx = uc_ref.shape[1]
    n_rows = n_ctx + n_main + n_ctx
    half = PAIRS_PER_SLAB // 2
    halves = [(list(range(0, half)), s_a, sin_a), (list(range(half, PAIRS_PER_SLAB)), s_b, sin_b)]

    for pairs, s_scr, _ in halves:
        for k, pp in enumerate(pairs):
            s_ctx = _bdot(uc_ref[pp], wst_ref[pp])
            s_scr[k, 0:n_ctx] = s_ctx
            s_scr[k, n_ctx:n_ctx + n_main] = _bdot(u_ref[pp], wst_ref[pp])
            s_scr[k, n_ctx + n_main:] = s_ctx
    y_intra = [_bdot(u_ref[pp], wy_ref[pp, 0:PAIR_W]) for pp in range(PAIRS_PER_SLAB)]

    row = lax.broadcasted_iota(jnp.int32, (SUBLANES, LANES), 0)
    n_groups = n_rows // SUBLANES
    n_iter = (n_ctx + n_main) // SUBLANES

    def group(s_scr, sin_scr, k, pp, base, c_re, c_im, lane0, forward):
        rows = pl.ds(base, SUBLANES)
        re_l = slice(lane0, lane0 + LANES)
        im_l = slice(lane0 + LANES, lane0 + 2 * LANES)
        cf = slice(0, LANES) if forward else slice(LANES, 2 * LANES)
        xr = s_scr[k, rows, re_l]
        xi = s_scr[k, rows, im_l]
        for step, sh in enumerate((1, 2, 4)):
            ar = coef_ref[pp, 2 * step, :, cf]
            ai = coef_ref[pp, 2 * step + 1, :, cf]
            amt = sh if forward else SUBLANES - sh
            sr = pltpu.roll(xr, amt, 0)
            si = pltpu.roll(xi, amt, 0)
            xr, xi = xr + (ar * sr - ai * si), xi + (ar * si + ai * sr)
        qr = coef_ref[pp, 6, :, cf]
        qi = coef_ref[pp, 7, :, cf]
        hr = xr + (qr * c_re - qi * c_im)
        hi = xi + (qr * c_im + qi * c_re)
        edge = 0 if forward else SUBLANES - 1
        amt = 1 if forward else SUBLANES - 1
        sin_scr[k, rows, re_l] = jnp.where(row == edge, c_re, pltpu.roll(hr, amt, 0))
        sin_scr[k, rows, im_l] = jnp.where(row == edge, c_im, pltpu.roll(hi, amt, 0))
        last = SUBLANES - 1 if forward else 0
        return (jnp.broadcast_to(hr[last:last + 1], (SUBLANES, LANES)),
                jnp.broadcast_to(hi[last:last + 1], (SUBLANES, LANES)))

    zero = jnp.zeros((SUBLANES, LANES), F32)
    main_rows = slice(n_ctx, n_ctx + n_main)
    for pairs, s_scr, sin_scr in halves:
        carry = [zero] * (4 * len(pairs))
        for g in range(n_iter):
            new = []
            for k, pp in enumerate(pairs):
                f_re, f_im, b_re, b_im = carry[4 * k:4 * k + 4]
                f_re, f_im = group(s_scr, sin_scr, k, pp, g * SUBLANES, f_re, f_im, 0, True)
                b_re, b_im = group(s_scr, sin_scr, k, pp, (n_groups - 1 - g) * SUBLANES,
                                   b_re, b_im, 2 * LANES, False)
                new += [f_re, f_im, b_re, b_im]
            carry = new
        for k, pp in enumerate(pairs):
            y_ref[pp] = y_intra[pp] + _bdot(sin_scr[k, main_rows, :].astype(BF16),
                                            wy_ref[pp, PAIR_W:2 * PAIR_W])


def _s5(u_pairs, uc_pairs, bsz, w_state, w_y, coef):
    n_main = u_pairs.shape[1] // bsz
    n_ctx = uc_pairs.shape[1] // bsz
    n_rows = n_main + 2 * n_ctx
    pps = PAIRS_PER_SLAB
    rows = lambda n: pl.BlockSpec((pps, n, PAIR_W), lambda j, b: (j, b, 0))
    return pl.pallas_call(
        _s5_kernel,
        grid=(S5_SLABS, bsz),
        in_specs=[rows(n_main), rows(n_ctx),
                  pl.BlockSpec((pps, PAIR_W, PAIR_W), lambda j, b: (j, 0, 0)),
                  pl.BlockSpec((pps, 2 * PAIR_W, PAIR_W), lambda j, b: (j, 0, 0)),
                  pl.BlockSpec((pps, 8, SUBLANES, 2 * LANES), lambda j, b: (j, 0, 0, 0))],
        out_specs=rows(n_main),
        out_shape=jax.ShapeDtypeStruct(u_pairs.shape, F32),
        scratch_shapes=[pltpu.VMEM((pps // 2, n_rows, PAIR_W), F32)] * 4,
        compiler_params=_cparams(("arbitrary", "arbitrary")),
        name="s5",
    )(u_pairs, uc_pairs, w_state, w_y, coef)


def _pair_tiles_to_tokens(y_ref, y_scr, slot):
    n_chunks = y_ref.shape[1]
    for j in range(S5_SLABS):
        slabs = []
        for a in range(S5_CHUNK):
            pp, blk = _pair_lane_block(a)
            slabs.append(y_ref[j * PAIRS_PER_SLAB + pp, :, blk * LANES:(blk + 1) * LANES])
        slabs = _swap_slab_and_piece(slabs)
        for t in range(S5_CHUNK):
            y_scr[slot, j, pl.ds(t, n_chunks, stride=S5_CHUNK), :] = slabs[t]


def _mix_kernel(y0_ref, ynext_ref, x_ref, ga_ref, mb_ref, mod_ref, wglu_ref, bglu_ref, wpa_ref,
                wout_ref, o_ref, y_scr):
    step = pl.program_id(0)
    cur = step % 2
    m = mod_ref[0]
    tt = x_ref.shape[1]

    @pl.when(step == 0)
    def _():
        _pair_tiles_to_tokens(y0_ref, y_scr, 0)

    groups = [slice(k * tt // ROW_GROUPS, (k + 1) * tt // ROW_GROUPS) for k in range(ROW_GROUPS)]
    ya = [_gelu_tanh(jnp.concatenate([y_scr[cur, j, r, :] for j in range(S5_SLABS)], axis=1))
          for r in groups]
    gate = [_bdot(t.astype(BF16), wglu_ref[...]) for t in ya]
    glu = [t * jax.nn.sigmoid(g + bglu_ref[...]) for t, g in zip(ya, gate)]
    pa = [_bdot(t.astype(BF16), wpa_ref[...]) for t in glu]
    merged = [(ga_ref[0, r, :].astype(F32) * p + mb_ref[0, r, :].astype(F32)).astype(BF16)
              for r, p in zip(groups, pa)]
    out = [_bdot(t, wout_ref[...]) for t in merged]
    for r, t in zip(groups, out):
        o_ref[0, r, :] = x_ref[0, r, :] + m[2:3] * t

    _pair_tiles_to_tokens(ynext_ref, y_scr, 1 - cur)


def _mix(y_pairs, x, ga, mb, mod3, w_glu, b_glu, w_proj_a, w_out):
    bsz, length, _ = x.shape
    tt = TOK_TILE
    nt = length // tt
    steps = bsz * nt
    tok = lambda w: pl.BlockSpec((1, tt, w), lambda s: (s // nt, s % nt, 0))
    y_tile = lambda fn: pl.BlockSpec((S5_PAIRS, tt // S5_CHUNK, PAIR_W), fn)
    return pl.pallas_call(
        _mix_kernel,
        grid=(steps,),
        in_specs=[y_tile(lambda s: (0, 0, 0)),
                  y_tile(lambda s: (0, jnp.minimum(s + 1, steps - 1), 0)),
                  tok(D_MODEL), tok(D_MODEL), tok(D_MODEL),
                  pl.BlockSpec((1, N_MOD, D_MODEL), lambda s: (s // nt, 0, 0)),
                  _const_spec(w_glu.shape), _const_spec(b_glu.shape),
                  _const_spec(w_proj_a.shape), _const_spec(w_out.shape)],
        out_specs=tok(D_MODEL),
        out_shape=jax.ShapeDtypeStruct(x.shape, F32),
        scratch_shapes=[pltpu.VMEM((2, S5_SLABS, tt, LANES), F32)],
        compiler_params=_cparams(("arbitrary",)),
        name="mix",
    )(y_pairs, y_pairs, x, ga, mb, mod3, w_glu, b_glu, w_proj_a, w_out)


def _conv_taps(cw_ref, cb_ref, cols):
    width = cols.stop - cols.start
    taps = [jnp.broadcast_to(cw_ref[k:k + 1, cols].astype(BF16), (GRID_W, width)) for k in range(9)]
    return taps, jnp.broadcast_to(cb_ref[:, cols].astype(BF16), (GRID_W, width))


def _token_neighbours(u):
    n, ch = u.shape
    shape4 = (n // GRID_W, GRID_W // SUBLANES, SUBLANES, ch)
    sub = lax.broadcasted_iota(jnp.int32, (1, 1, SUBLANES, 1), 2)
    zero = jnp.zeros((shape4[0], 1, SUBLANES, ch), F32)
    u4 = u.reshape(shape4)
    down = pltpu.roll(u4, 1, 2)
    left = jnp.where(sub == 0, jnp.concatenate([zero, down[:, :-1]], axis=1), down)
    up = pltpu.roll(u4, SUBLANES - 1, 2)
    right = jnp.where(sub == SUBLANES - 1, jnp.concatenate([up[:, 1:], zero], axis=1), up)
    return left.reshape(n, ch).astype(BF16), u.astype(BF16), right.reshape(n, ch).astype(BF16)


def _grid_conv_row(nbrs, taps, bias, rb):
    lo = rb * GRID_W
    part = []
    for dr in range(3):
        rows = slice(lo + dr * GRID_W, lo + (dr + 1) * GRID_W)
        left, here, right = (nbrs[dc][rows] * taps[3 * dr + dc] for dc in range(3))
        part.append((left + here) + right)
    return (part[0] + part[1]) + (part[2] + bias)


def _ffn_kernel(top_ref, x_ref, bot_ref, mod_ref, g_ref, wup_ref, cw_ref, cb_ref, wdn_ref, gf_ref,
                o_ref, h_scr, act_scr):
    i = pl.program_id(1)
    n_i = pl.num_programs(1)
    m = mod_ref[0]
    g = g_ref[...]
    tt = x_ref.shape[1]
    x = x_ref[0]

    def norm(t):
        return _modulated_rms(t, g, m[3:4], m[4:5])

    h_scr[0:GRID_W] = jnp.where(i > 0, norm(top_ref[0]), 0.0).astype(BF16)
    h_scr[GRID_W:GRID_W + tt] = norm(x).astype(BF16)
    h_scr[GRID_W + tt:] = jnp.where(i < n_i - 1, norm(bot_ref[0]), 0.0).astype(BF16)

    h = h_scr[...]

    fc = FFN_CHUNK
    for j in range(FFN_HIDDEN // fc):
        gs = slice(j * fc, (j + 1) * fc)
        vs = slice(FFN_HIDDEN + j * fc, FFN_HIDDEN + (j + 1) * fc)
        ug = _token_neighbours(_bdot(h, wup_ref[:, gs]))
        uv = _token_neighbours(_bdot(h, wup_ref[:, vs]))
        g_taps, g_bias = _conv_taps(cw_ref, cb_ref, gs)
        v_taps, v_bias = _conv_taps(cw_ref, cb_ref, vs)
        for rb in range(tt // GRID_W):
            cg = _grid_conv_row(ug, g_taps, g_bias, rb)
            cv = _grid_conv_row(uv, v_taps, v_bias, rb)
            act_scr[rb * GRID_W:(rb + 1) * GRID_W, gs] = (cg * jax.nn.sigmoid(cg)) * cv

    x2 = x + m[5:6] * _bdot(act_scr[...], wdn_ref[...])
    ms = jnp.mean(x2 * x2, axis=-1, keepdims=True)
    o_ref[0] = (x2 * lax.rsqrt(ms + EPS)) * gf_ref[...]


def _ffn(x1, mod3, g_ffn, w_up, conv_w9, conv_b, w_down, g_final):
    bsz, length, _ = x1.shape
    tt = FFN_TILE
    rows_per_tile = tt // GRID_W
    n_rows = length // GRID_W
    halo = lambda fn: pl.BlockSpec((1, GRID_W, D_MODEL), fn)
    return pl.pallas_call(
        _ffn_kernel,
        grid=(bsz, length // tt),
        in_specs=[halo(lambda b, i: (b, jnp.maximum(i * rows_per_tile - 1, 0), 0)),
                  pl.BlockSpec((1, tt, D_MODEL), lambda b, i: (b, i, 0)),
                  halo(lambda b, i: (b, jnp.minimum((i + 1) * rows_per_tile, n_rows - 1), 0)),
                  pl.BlockSpec((1, N_MOD, D_MODEL), lambda b, i: (b, 0, 0)),
                  _const_spec(g_ffn.shape), _const_spec(w_up.shape), _const_spec(conv_w9.shape),
                  _const_spec(conv_b.shape), _const_spec(w_down.shape), _const_spec(g_final.shape)],
        out_specs=pl.BlockSpec((1, tt, D_MODEL), lambda b, i: (b, i, 0)),
        out_shape=jax.ShapeDtypeStruct(x1.shape, F32),
        scratch_shapes=[pltpu.VMEM((tt + 2 * GRID_W, D_MODEL), BF16),
                        pltpu.VMEM((tt, FFN_HIDDEN), BF16)],
        compiler_params=_cparams(("parallel", "parallel")),
        name="ffn",
    )(x1, x1, x1, mod3, g_ffn, w_up, conv_w9, conv_b, w_down, g_final)


def kernel(x, c, ctx, c_ctx, w_ada, b_ada, g_mix, w_in, s5_a_re, s5_a_im, s5_log_step, s5_b_re, s5_b_im, s5_c_re, s5_c_im, s5_d, s5_w_glu, s5_b_glu, sgu_ln_g, sgu_ln_b, sgu_w, sgu_b, w_proj_a, w_proj_b, b_gate, w_out, g_ffn, w_up, conv_w, conv_b, w_down, g_final):
    bsz = x.shape[0]
    row = lambda t: t.reshape(1, -1)

    ctx_row = bsz
    c_rows = jnp.zeros((SUBLANES, D_MODEL), F32).at[:bsz].set(c).at[ctx_row].set(c_ctx)
    mod3 = _ada_rows(c_rows, w_ada[0], b_ada[0]).reshape(SUBLANES, N_MOD, D_MODEL)

    w_in_b = w_in[0].astype(BF16)
    wsp = sgu_w[0].astype(BF16)
    wsp_pair = jnp.concatenate([wsp[0::2], wsp[1::2]], axis=-1)
    bsp_rows = jnp.repeat(sgu_b[0].T, SGU_GROUP_DIM, axis=1)

    u_pairs, ga, mb = _inproj(x, mod3, row(g_mix[0]), w_in_b, row(b_gate[0]), row(sgu_ln_g[0]),
                              row(sgu_ln_b[0]), wsp_pair, bsp_rows, w_proj_b[0].astype(BF16))
    uc_pairs = _ctxproj(ctx, mod3, ctx_row, row(g_mix[0]), w_in_b[:, :S5_WIDTH])

    w_state, w_y, coef = _s5_operators(
        s5_a_re[0], s5_a_im[0], s5_log_step[0], s5_b_re[0], s5_b_im[0], s5_c_re[0], s5_c_im[0],
        s5_d[0])
    y_pairs = _s5(u_pairs, uc_pairs, bsz, w_state, w_y, coef)

    x1 = _mix(y_pairs, x, ga, mb, mod3, s5_w_glu[0].astype(BF16), row(s5_b_glu[0]),
              w_proj_a[0].astype(BF16), w_out[0].astype(BF16))

    return _ffn(x1, mod3, row(g_ffn[0]), w_up[0].astype(BF16), conv_w[0].reshape(9, -1),
                row(conv_b[0]), w_down[0].astype(BF16), row(g_final))
```

```python
import math

import jax
import jax.numpy as jnp
from jax import lax
from jax.experimental import pallas as pl
from jax.experimental.pallas import tpu as pltpu

F32 = jnp.float32
BF16 = jnp.bfloat16

D_MODEL = 1024
GRID_W = 64
S5_WIDTH = 512
S5_GROUP = 16
S5_GROUPS = S5_WIDTH // S5_GROUP
S5_STATE = 64
SGU_WIDTH = 512
SGU_GROUPS = 8
SGU_GROUP_DIM = SGU_WIDTH // SGU_GROUPS
CHUNK = 128
FFN_HIDDEN = 2816
N_MOD = 6
EPS = 1e-6

LANES = 128
SUBLANES = 8
V7X_VMEM_LIMIT = 56 * 1024 * 1024

S5_CHUNK = 16
S5_PAIRS = S5_GROUPS // 2
PAIR_W = 2 * S5_CHUNK * S5_GROUP
S5_SLABS = S5_WIDTH // LANES
PAIRS_PER_SLAB = S5_PAIRS // S5_SLABS

TOK_TILE = 1024
ROW_GROUPS = 2
FFN_TILE = 512
FFN_CHUNK = 256


def _cparams(sem):
    return pltpu.CompilerParams(dimension_semantics=sem, vmem_limit_bytes=V7X_VMEM_LIMIT)


def _const_spec(shape):
    zeros = (0,) * len(shape)
    return pl.BlockSpec(shape, lambda *_: zeros, pipeline_mode=pl.Buffered(1))


def _gelu_tanh(x):
    c = math.sqrt(2.0 / math.pi)
    return 0.5 * x * (1.0 + jnp.tanh(c * (x + 0.044715 * (x * x * x))))


def _modulated_rms(x, g, shift, scale):
    ms = jnp.mean(x * x, axis=-1, keepdims=True)
    return (x * lax.rsqrt(ms + EPS)) * (g * (1.0 + scale)) + shift


def _bdot(a, b):
    return jnp.dot(a, b, preferred_element_type=F32)


def _ada_kernel(c_ref, w_ref, b_ref, o_ref):
    c = c_ref[...]
    cs = c * jax.nn.sigmoid(c)
    o_ref[...] = jnp.dot(cs, w_ref[...], preferred_element_type=F32,
                         precision=lax.Precision.HIGHEST) + b_ref[...]


def _ada_rows(c_rows, w_ada, b_ada):
    n = w_ada.shape[1]
    bn = 1536
    return pl.pallas_call(
        _ada_kernel,
        grid=(n // bn,),
        in_specs=[_const_spec(c_rows.shape),
                  pl.BlockSpec((D_MODEL, bn), lambda j: (0, j)),
                  pl.BlockSpec((1, bn), lambda j: (0, j))],
        out_specs=pl.BlockSpec((c_rows.shape[0], bn), lambda j: (0, j)),
        out_shape=jax.ShapeDtypeStruct((c_rows.shape[0], n), F32),
        compiler_params=_cparams(("arbitrary",)),
        name="ada",
    )(c_rows, w_ada, b_ada.reshape(1, n))


def _swap_slab_and_piece(slabs):
    lane = lax.broadcasted_iota(jnp.int32, (1, LANES), 1)
    out = list(slabs)
    for k in range(3):
        width = S5_GROUP << k
        keep = ((lane >> (4 + k)) & 1) == 0
        nxt = list(out)
        for a0 in range(len(out)):
            if a0 & (1 << k):
                continue
            a1 = a0 | (1 << k)
            nxt[a0] = jnp.where(keep, out[a0], pltpu.roll(out[a1], width, 1))
            nxt[a1] = jnp.where(keep, pltpu.roll(out[a0], LANES - width, 1), out[a1])
        out = nxt
    return out


def _pair_lane_block(slab_idx):
    t_hi, g = divmod(slab_idx, SUBLANES)
    pp, gg = divmod(g, 2)
    return pp, gg * 2 + t_hi


def _store_pair_tiles(u, u_scr, up_ref, row0=0):
    n_chunks = u.shape[0] // S5_CHUNK
    chunks = slice(row0 // S5_CHUNK, row0 // S5_CHUNK + n_chunks)
    for j in range(S5_SLABS):
        u_scr[j, row0:row0 + u.shape[0], :] = u[:, j * LANES:(j + 1) * LANES]
    for j in range(S5_SLABS):
        slabs = _swap_slab_and_piece(
            [u_scr[j, pl.ds(row0 + t, n_chunks, stride=S5_CHUNK), :] for t in range(S5_CHUNK)])
        for a in range(S5_CHUNK):
            pp, blk = _pair_lane_block(a)
            up_ref[j * PAIRS_PER_SLAB + pp, chunks, blk * LANES:(blk + 1) * LANES] = (
                slabs[a].astype(BF16))


def _inproj_kernel(x_ref, mod_ref, g_ref, w_ref, bgate_ref, lng_ref, lnb_ref, wsp_ref, bsp_ref,
                   wpb_ref, u_ref, ga_ref, mb_ref, u_scr):
    m = mod_ref[0]
    tt = x_ref.shape[1]
    z0 = S5_WIDTH
    g0 = S5_WIDTH + 2 * SGU_WIDTH
    lane = lax.broadcasted_iota(jnp.int32, (CHUNK, LANES), 1)
    zero = jnp.zeros((CHUNK, LANES), BF16)

    groups = [slice(k * tt // ROW_GROUPS, (k + 1) * tt // ROW_GROUPS) for k in range(ROW_GROUPS)]
    proj = []
    for r in groups:
        h = _modulated_rms(x_ref[0, r, :], g_ref[...], m[0:1], m[1:2]).astype(BF16)
        proj.append(dict(
            u=_bdot(h, w_ref[:, 0:S5_WIDTH]),
            zv=_bdot(h, w_ref[:, z0 + SGU_WIDTH:z0 + 2 * SGU_WIDTH]),
            zu=_bdot(h, w_ref[:, z0:z0 + SGU_WIDTH]),
            ga=_bdot(h, w_ref[:, g0:g0 + D_MODEL]),
            gb=_bdot(h, w_ref[:, g0 + D_MODEL:g0 + 2 * D_MODEL])))

    for r, p in zip(groups, proj):
        _store_pair_tiles(p["u"], u_scr, u_ref, r.start)

        zv = _gelu_tanh(p["zv"])
        mu = jnp.mean(zv, axis=-1, keepdims=True)
        zc = zv - mu
        var = jnp.mean(zc * zc, axis=-1, keepdims=True)
        v = ((zc * lax.rsqrt(var + EPS)) * lng_ref[...] + lnb_ref[...]).astype(BF16)
        zu = _gelu_tanh(p["zu"])

        rows = []
        for c in range(zv.shape[0] // CHUNK):
            tiles = []
            for j in range(SGU_WIDTH // LANES):
                vj = v[c * CHUNK:(c + 1) * CHUNK, j * LANES:(j + 1) * LANES]
                rhs = jnp.concatenate([jnp.where(lane < SGU_GROUP_DIM, vj, zero),
                                       jnp.where(lane >= SGU_GROUP_DIM, vj, zero)], axis=0)
                tiles.append(_bdot(wsp_ref[j], rhs))
            rows.append(jnp.concatenate(tiles, axis=1) + bsp_ref[...])
        s = jnp.concatenate(rows, axis=0)
        pb = _bdot((zu * s).astype(BF16), wpb_ref[...])

        ga = jax.nn.sigmoid(p["ga"] + bgate_ref[:, 0:D_MODEL])
        gb = jax.nn.sigmoid(p["gb"] + bgate_ref[:, D_MODEL:2 * D_MODEL])
        ga_ref[0, r, :] = ga.astype(BF16)
        mb_ref[0, r, :] = (gb * pb).astype(BF16)


def _inproj(x, mod3, g_mix, w_in, b_gate, ln_g, ln_b, wsp_pair, bsp_rows, w_proj_b):
    bsz, length, _ = x.shape
    tt = TOK_TILE
    nt = length // tt
    tok = lambda w: pl.BlockSpec((1, tt, w), lambda b, i: (b, i, 0))
    return pl.pallas_call(
        _inproj_kernel,
        grid=(bsz, nt),
        in_specs=[tok(D_MODEL),
                  pl.BlockSpec((1, N_MOD, D_MODEL), lambda b, i: (b, 0, 0)),
                  _const_spec(g_mix.shape), _const_spec(w_in.shape), _const_spec(b_gate.shape),
                  _const_spec(ln_g.shape), _const_spec(ln_b.shape), _const_spec(wsp_pair.shape),
                  _const_spec(bsp_rows.shape), _const_spec(w_proj_b.shape)],
        out_specs=[pl.BlockSpec((S5_PAIRS, tt // S5_CHUNK, PAIR_W), lambda b, i: (0, b * nt + i, 0)),
                   tok(D_MODEL), tok(D_MODEL)],
        out_shape=[jax.ShapeDtypeStruct((S5_PAIRS, bsz * length // S5_CHUNK, PAIR_W), BF16),
                   jax.ShapeDtypeStruct((bsz, length, D_MODEL), BF16),
                   jax.ShapeDtypeStruct((bsz, length, D_MODEL), BF16)],
        scratch_shapes=[pltpu.VMEM((S5_SLABS, tt, LANES), F32)],
        compiler_params=_cparams(("parallel", "parallel")),
        name="inproj",
    )(x, mod3, g_mix, w_in, b_gate, ln_g, ln_b, wsp_pair, bsp_rows, w_proj_b)


def _ctxproj_kernel(x_ref, mod_ref, g_ref, w_ref, u_ref, u_scr):
    m = mod_ref[0]
    h = _modulated_rms(x_ref[0], g_ref[...], m[0:1], m[1:2]).astype(BF16)
    _store_pair_tiles(_bdot(h, w_ref[...]), u_scr, u_ref)


def _ctxproj(ctx, mod3, ctx_row, g_mix, w_in_u):
    bsz, length, _ = ctx.shape
    return pl.pallas_call(
        _ctxproj_kernel,
        grid=(bsz,),
        in_specs=[pl.BlockSpec((1, length, D_MODEL), lambda b: (b, 0, 0)),
                  pl.BlockSpec((1, N_MOD, D_MODEL), lambda b: (ctx_row, 0, 0)),
                  _const_spec(g_mix.shape), _const_spec(w_in_u.shape)],
        out_specs=pl.BlockSpec((S5_PAIRS, length // S5_CHUNK, PAIR_W), lambda b: (0, b, 0)),
        out_shape=jax.ShapeDtypeStruct((S5_PAIRS, bsz * length // S5_CHUNK, PAIR_W), BF16),
        scratch_shapes=[pltpu.VMEM((S5_SLABS, length, LANES), F32)],
        compiler_params=_cparams(("parallel",)),
        name="ctxproj",
    )(ctx, mod3, g_mix, w_in_u)


def _cmul(ar, ai, br, bi):
    return ar * br - ai * bi, ar * bi + ai * br


def _s5_ops_kernel(a_re_ref, a_im_ref, ls_ref, bt_re_ref, bt_im_ref, c_re_ref, c_im_ref, d_ref,
                   wst_ref, wy_ref, coef_ref):
    hp = lax.Precision.HIGHEST
    n_tok = S5_CHUNK
    blk = n_tok * S5_GROUP
    tok_of_lane = lax.broadcasted_iota(jnp.int32, (1, blk), 1) // S5_GROUP
    row8 = lax.broadcasted_iota(jnp.int32, (SUBLANES, 1), 0)
    diag = (lax.broadcasted_iota(jnp.int32, (blk, blk), 0)
            == lax.broadcasted_iota(jnp.int32, (blk, blk), 1))
    contract_n = (((1,), (1,)), ((), ()))

    for gg in range(2):
        rows_g = slice(gg * blk, (gg + 1) * blk)
        cols_g = slice(gg * blk, (gg + 1) * blk)
        cols_other = slice((1 - gg) * blk, (2 - gg) * blk)
        w_intra = jnp.where(diag, d_ref[0, gg], 0.0)
        for d in range(2):
            forward = d == 0
            a_re = a_re_ref[d, 0, gg]
            a_im = a_im_ref[d, 0, gg]
            dt = jnp.exp(ls_ref[d, 0, gg])
            mag = jnp.exp(a_re * dt)
            lam = (mag * jnp.cos(a_im * dt), mag * jnp.sin(a_im * dt))
            den = a_re * a_re + a_im * a_im
            p, q = lam[0] - 1.0, lam[1]
            k = ((p * a_re + q * a_im) / den, (q * a_re - p * a_im) / den)
            bbar = _cmul(*k, bt_re_ref[d, 0, gg], bt_im_ref[d, 0, gg])
            c = (c_re_ref[d, 0, gg], c_im_ref[d, 0, gg])
            pw = [(jnp.ones_like(p), jnp.zeros_like(p))]
            for _ in range(n_tok):
                pw.append(_cmul(*pw[-1], *lam))

            expo = [n_tok - 1 - i if forward else i for i in range(n_tok)]
            cols = [_cmul(*pw[e], *bbar) for e in expo]
            m_re = jnp.concatenate([t[0] for t in cols], axis=0)
            m_im = jnp.concatenate([t[1] for t in cols], axis=0)
            for comp, m in ((2 * d, m_re), (2 * d + 1, m_im)):
                lo = comp * LANES + gg * S5_STATE
                other = comp * LANES + (1 - gg) * S5_STATE
                wst_ref[0, rows_g, lo:lo + S5_STATE] = m.astype(BF16)
                wst_ref[0, rows_g, other:other + S5_STATE] = jnp.zeros((blk, S5_STATE), BF16)

            c_rep = [jnp.concatenate([t] * n_tok, axis=0) for t in c]
            kr = (lax.dot_general(m_re, c_rep[0], contract_n, precision=hp,
                                  preferred_element_type=F32)
                  - lax.dot_general(m_im, c_rep[1], contract_n, precision=hp,
                                    preferred_element_type=F32))
            toep = jnp.zeros((blk, blk), F32)
            for j in range(n_tok):
                s = S5_GROUP * (n_tok - 1 - j if forward else j)
                if s == 0:
                    moved = kr
                elif forward:
                    moved = jnp.concatenate([kr[s:], jnp.zeros((s, blk), F32)], axis=0)
                else:
                    moved = jnp.concatenate([jnp.zeros((s, blk), F32), kr[:blk - s]], axis=0)
                toep = jnp.where(tok_of_lane == j, moved, toep)
            w_intra = w_intra + toep

            expo = [j + 1 if forward else n_tok - j for j in range(n_tok)]
            rows = [_cmul(*pw[e], *c) for e in expo]
            r_re = jnp.concatenate([t[0] for t in rows], axis=0).T
            r_im = -jnp.concatenate([t[1] for t in rows], axis=0).T
            for comp, r in ((2 * d, r_re), (2 * d + 1, r_im)):
                r0 = PAIR_W + comp * LANES + gg * S5_STATE
                wy_ref[0, r0:r0 + S5_STATE, cols_g] = r.astype(BF16)
                wy_ref[0, r0:r0 + S5_STATE, cols_other] = jnp.zeros((S5_STATE, blk), BF16)

            lp = [(jnp.ones_like(p), jnp.zeros_like(p)), pw[n_tok]]
            for _ in range(SUBLANES - 1):
                lp.append(_cmul(*lp[-1], *pw[n_tok]))
            lanes = slice(d * LANES + gg * S5_STATE, d * LANES + (gg + 1) * S5_STATE)
            for kk, s in enumerate((1, 2, 4)):
                valid = (row8 >= s) if forward else (row8 + s < SUBLANES)
                coef_ref[0, 2 * kk, :, lanes] = jnp.where(valid, lp[s][0], 0.0)
                coef_ref[0, 2 * kk + 1, :, lanes] = jnp.where(valid, lp[s][1], 0.0)
            q_re = jnp.zeros((SUBLANES, S5_STATE), F32)
            q_im = jnp.zeros((SUBLANES, S5_STATE), F32)
            for r in range(SUBLANES):
                e = r + 1 if forward else SUBLANES - r
                q_re = jnp.where(row8 == r, lp[e][0], q_re)
                q_im = jnp.where(row8 == r, lp[e][1], q_im)
            coef_ref[0, 6, :, lanes] = q_re
            coef_ref[0, 7, :, lanes] = q_im

        wy_ref[0, rows_g, cols_g] = w_intra.astype(BF16)
        wy_ref[0, rows_g, cols_other] = jnp.zeros((blk, blk), BF16)


def _s5_operators(a_re, a_im, log_step, b_re, b_im, c_re, c_im, d_skip):
    grp = lambda t, *tail: t.reshape(2, S5_PAIRS, 2, *tail)
    row = lambda t: grp(t, 1, t.shape[-1])
    bt_re = grp(jnp.swapaxes(b_re, -1, -2), S5_GROUP, S5_STATE)
    bt_im = grp(jnp.swapaxes(b_im, -1, -2), S5_GROUP, S5_STATE)
    d_tiled = jnp.tile(d_skip.reshape(S5_PAIRS, 2, 1, S5_GROUP), (1, 1, 1, S5_CHUNK))
    per_pair = lambda *tail: pl.BlockSpec((2, 1, 2) + tail, lambda p: (0, p, 0) + (0,) * len(tail))
    out = lambda *tail: pl.BlockSpec((1,) + tail, lambda p: (p,) + (0,) * len(tail))
    return pl.pallas_call(
        _s5_ops_kernel,
        grid=(S5_PAIRS,),
        in_specs=[per_pair(1, S5_STATE), per_pair(1, S5_STATE), per_pair(1, 1),
                  per_pair(S5_GROUP, S5_STATE), per_pair(S5_GROUP, S5_STATE),
                  per_pair(S5_GROUP, S5_STATE), per_pair(S5_GROUP, S5_STATE),
                  pl.BlockSpec((1, 2, 1, S5_CHUNK * S5_GROUP), lambda p: (p, 0, 0, 0))],
        out_specs=[out(PAIR_W, PAIR_W), out(2 * PAIR_W, PAIR_W), out(8, SUBLANES, 2 * LANES)],
        out_shape=[jax.ShapeDtypeStruct((S5_PAIRS, PAIR_W, PAIR_W), BF16),
                   jax.ShapeDtypeStruct((S5_PAIRS, 2 * PAIR_W, PAIR_W), BF16),
                   jax.ShapeDtypeStruct((S5_PAIRS, 8, SUBLANES, 2 * LANES), F32)],
        compiler_params=_cparams(("parallel",)),
        name="s5_ops",
    )(row(a_re), row(a_im), grp(log_step, 1, 1), bt_re, bt_im,
      grp(c_re, S5_GROUP, S5_STATE), grp(c_im, S5_GROUP, S5_STATE), d_tiled)


def _s5_kernel(u_ref, uc_ref, wst_ref, wy_ref, coef_ref, y_ref, s_a, sin_a, s_b, sin_b):
    n_main = u_ref.shape[1]
    n_ctx = uc_ref.shape[1]
    n_rows = n_ctx + n_main + n_ctx
    half = PAIRS_PER_SLAB // 2
    halves = [(list(range(0, half)), s_a, sin_a), (list(range(half, PAIRS_PER_SLAB)), s_b, sin_b)]

    for pairs, s_scr, _ in halves:
        for k, pp in enumerate(pairs):
            s_ctx = _bdot(uc_ref[pp], wst_ref[pp])
            s_scr[k, 0:n_ctx] = s_ctx
            s_scr[k, n_ctx:n_ctx + n_main] = _bdot(u_ref[pp], wst_ref[pp])
            s_scr[k, n_ctx + n_main:] = s_ctx
    y_intra = [_bdot(u_ref[pp], wy_ref[pp, 0:PAIR_W]) for pp in range(PAIRS_PER_SLAB)]

    row = lax.broadcasted_iota(jnp.int32, (SUBLANES, LANES), 0)
    n_groups = n_rows // SUBLANES
    n_iter = (n_ctx + n_main) // SUBLANES

    def group(s_scr, sin_scr, k, pp, base, c_re, c_im, lane0, forward):
        rows = pl.ds(base, SUBLANES)
        re_l = slice(lane0, lane0 + LANES)
        im_l = slice(lane0 + LANES, lane0 + 2 * LANES)
        cf = slice(0, LANES) if forward else slice(LANES, 2 * LANES)
        xr = s_scr[k, rows, re_l]
        xi = s_scr[k, rows, im_l]
        for step, sh in enumerate((1, 2, 4)):
            ar = coef_ref[pp, 2 * step, :, cf]
            ai = coef_ref[pp, 2 * step + 1, :, cf]
            amt = sh if forward else SUBLANES - sh
            sr = pltpu.roll(xr, amt, 0)
            si = pltpu.roll(xi, amt, 0)
            xr, xi = xr + (ar * sr - ai * si), xi + (ar * si + ai * sr)
        qr = coef_ref[pp, 6, :, cf]
        qi = coef_ref[pp, 7, :, cf]
        hr = xr + (qr * c_re - qi * c_im)
        hi = xi + (qr * c_im + qi * c_re)
        edge = 0 if forward else SUBLANES - 1
        amt = 1 if forward else SUBLANES - 1
        sin_scr[k, rows, re_l] = jnp.where(row == edge, c_re, pltpu.roll(hr, amt, 0))
        sin_scr[k, rows, im_l] = jnp.where(row == edge, c_im, pltpu.roll(hi, amt, 0))
        last = SUBLANES - 1 if forward else 0
        return (jnp.broadcast_to(hr[last:last + 1], (SUBLANES, LANES)),
                jnp.broadcast_to(hi[last:last + 1], (SUBLANES, LANES)))

    zero = jnp.zeros((SUBLANES, LANES), F32)
    main_rows = slice(n_ctx, n_ctx + n_main)
    for pairs, s_scr, sin_scr in halves:
        carry = [zero] * (4 * len(pairs))
        for g in range(n_iter):
            new = []
            for k, pp in enumerate(pairs):
                f_re, f_im, b_re, b_im = carry[4 * k:4 * k + 4]
                f_re, f_im = group(s_scr, sin_scr, k, pp, g * SUBLANES, f_re, f_im, 0, True)
                b_re, b_im = group(s_scr, sin_scr, k, pp, (n_groups - 1 - g) * SUBLANES,
                                   b_re, b_im, 2 * LANES, False)
                new += [f_re, f_im, b_re, b_im]
            carry = new
        for k, pp in enumerate(pairs):
            y_ref[pp] = y_intra[pp] + _bdot(sin_scr[k, main_rows, :].astype(BF16),
                                            wy_ref[pp, PAIR_W:2 * PAIR_W])


def _s5(u_pairs, uc_pairs, bsz, w_state, w_y, coef):
    n_main = u_pairs.shape[1] // bsz
    n_ctx = uc_pairs.shape[1] // bsz
    n_rows = n_main + 2 * n_ctx
    pps = PAIRS_PER_SLAB
    rows = lambda n: pl.BlockSpec((pps, n, PAIR_W), lambda j, b: (j, b, 0))
    return pl.pallas_call(
        _s5_kernel,
        grid=(S5_SLABS, bsz),
        in_specs=[rows(n_main), rows(n_ctx),
                  pl.BlockSpec((pps, PAIR_W, PAIR_W), lambda j, b: (j, 0, 0)),
                  pl.BlockSpec((pps, 2 * PAIR_W, PAIR_W), lambda j, b: (j, 0, 0)),
                  pl.BlockSpec((pps, 8, SUBLANES, 2 * LANES), lambda j, b: (j, 0, 0, 0))],
        out_specs=rows(n_main),
        out_shape=jax.ShapeDtypeStruct(u_pairs.shape, F32),
        scratch_shapes=[pltpu.VMEM((pps // 2, n_rows, PAIR_W), F32)] * 4,
        compiler_params=_cparams(("arbitrary", "arbitrary")),
        name="s5",
    )(u_pairs, uc_pairs, w_state, w_y, coef)


def _pair_tiles_to_tokens(y_ref, y_scr, slot):
    n_chunks = y_ref.shape[1]
    for j in range(S5_SLABS):
        slabs = []
        for a in range(S5_CHUNK):
            pp, blk = _pair_lane_block(a)
            slabs.append(y_ref[j * PAIRS_PER_SLAB + pp, :, blk * LANES:(blk + 1) * LANES])
        slabs = _swap_slab_and_piece(slabs)
        for t in range(S5_CHUNK):
            y_scr[slot, j, pl.ds(t, n_chunks, stride=S5_CHUNK), :] = slabs[t]


def _mix_kernel(y0_ref, ynext_ref, x_ref, ga_ref, mb_ref, mod_ref, wglu_ref, bglu_ref, wpa_ref,
                wout_ref, o_ref, y_scr):
    step = pl.program_id(0)
    cur = step % 2
    m = mod_ref[0]
    tt = x_ref.shape[1]

    @pl.when(step == 0)
    def _():
        _pair_tiles_to_tokens(y0_ref, y_scr, 0)

    groups = [slice(k * tt // ROW_GROUPS, (k + 1) * tt // ROW_GROUPS) for k in range(ROW_GROUPS)]
    ya = [_gelu_tanh(jnp.concatenate([y_scr[cur, j, r, :] for j in range(S5_SLABS)], axis=1))
          for r in groups]
    gate = [_bdot(t.astype(BF16), wglu_ref[...]) for t in ya]
    glu = [t * jax.nn.sigmoid(g + bglu_ref[...]) for t, g in zip(ya, gate)]
    pa = [_bdot(t.astype(BF16), wpa_ref[...]) for t in glu]
    merged = [(ga_ref[0, r, :].astype(F32) * p + mb_ref[0, r, :].astype(F32)).astype(BF16)
              for r, p in zip(groups, pa)]
    out = [_bdot(t, wout_ref[...]) for t in merged]
    for r, t in zip(groups, out):
        o_ref[0, r, :] = x_ref[0, r, :] + m[2:3] * t

    _pair_tiles_to_tokens(ynext_ref, y_scr, 1 - cur)


def _mix(y_pairs, x, ga, mb, mod3, w_glu, b_glu, w_proj_a, w_out):
    bsz, length, _ = x.shape
    tt = TOK_TILE
    nt = length // tt
    steps = bsz * nt
    tok = lambda w: pl.BlockSpec((1, tt, w), lambda s: (s // nt, s % nt, 0))
    y_tile = lambda fn: pl.BlockSpec((S5_PAIRS, tt // S5_CHUNK, PAIR_W), fn)
    return pl.pallas_call(
        _mix_kernel,
        grid=(steps,),
        in_specs=[y_tile(lambda s: (0, 0, 0)),
                  y_tile(lambda s: (0, jnp.minimum(s + 1, steps - 1), 0)),
                  tok(D_MODEL), tok(D_MODEL), tok(D_MODEL),
                  pl.BlockSpec((1, N_MOD, D_MODEL), lambda s: (s // nt, 0, 0)),
                  _const_spec(w_glu.shape), _const_spec(b_glu.shape),
                  _const_spec(w_proj_a.shape), _const_spec(w_out.shape)],
        out_specs=tok(D_MODEL),
        out_shape=jax.ShapeDtypeStruct(x.shape, F32),
        scratch_shapes=[pltpu.VMEM((2, S5_SLABS, tt, LANES), F32)],
        compiler_params=_cparams(("arbitrary",)),
        name="mix",
    )(y_pairs, y_pairs, x, ga, mb, mod3, w_glu, b_glu, w_proj_a, w_out)


def _conv_taps(cw_ref, cb_ref, cols):
    width = cols.stop - cols.start
    taps = [jnp.broadcast_to(cw_ref[k:k + 1, cols].astype(BF16), (GRID_W, width)) for k in range(9)]
    return taps, jnp.broadcast_to(cb_ref[:, cols].astype(BF16), (GRID_W, width))


def _token_neighbours(u):
    n, ch = u.shape
    shape4 = (n // GRID_W, GRID_W // SUBLANES, SUBLANES, ch)
    sub = lax.broadcasted_iota(jnp.int32, (1, 1, SUBLANES, 1), 2)
    zero = jnp.zeros((shape4[0], 1, SUBLANES, ch), F32)
    u4 = u.reshape(shape4)
    down = pltpu.roll(u4, 1, 2)
    left = jnp.where(sub == 0, jnp.concatenate([zero, down[:, :-1]], axis=1), down)
    up = pltpu.roll(u4, SUBLANES - 1, 2)
    right = jnp.where(sub == SUBLANES - 1, jnp.concatenate([up[:, 1:], zero], axis=1), up)
    return left.reshape(n, ch).astype(BF16), u.astype(BF16), right.reshape(n, ch).astype(BF16)


def _grid_conv_row(nbrs, taps, bias, rb):
    lo = rb * GRID_W
    part = []
    for dr in range(3):
        rows = slice(lo + dr * GRID_W, lo + (dr + 1) * GRID_W)
        left, here, right = (nbrs[dc][rows] * taps[3 * dr + dc] for dc in range(3))
        part.append((left + here) + right)
    return (part[0] + part[1]) + (part[2] + bias)


def _ffn_kernel(top_ref, x_ref, bot_ref, mod_ref, g_ref, wup_ref, cw_ref, cb_ref, wdn_ref, gf_ref,
                o_ref, h_scr, act_scr):
    i = pl.program_id(1)
    n_i = pl.num_programs(1)
    m = mod_ref[0]
    g = g_ref[...]
    tt = x_ref.shape[1]
    x = x_ref[0]

    def norm(t):
        return _modulated_rms(t, g, m[3:4], m[4:5])

    h_scr[0:GRID_W] = jnp.where(i > 0, norm(top_ref[0]), 0.0).astype(BF16)
    h_scr[GRID_W:GRID_W + tt] = norm(x).astype(BF16)
    h_scr[GRID_W + tt:] = jnp.where(i < n_i - 1, norm(bot_ref[0]), 0.0).astype(BF16)

    h = h_scr[...]

    fc = FFN_CHUNK
    for j in range(FFN_HIDDEN // fc):
        gs = slice(j * fc, (j + 1) * fc)
        vs = slice(FFN_HIDDEN + j * fc, FFN_HIDDEN + (j + 1) * fc)
        ug = _token_neighbours(_bdot(h, wup_ref[:, gs]))
        uv = _token_neighbours(_bdot(h, wup_ref[:, vs]))
        g_taps, g_bias = _conv_taps(cw_ref, cb_ref, gs)
        v_taps, v_bias = _conv_taps(cw_ref, cb_ref, vs)
        for rb in range(tt // GRID_W):
            cg = _grid_conv_row(ug, g_taps, g_bias, rb)
            cv = _grid_conv_row(uv, v_taps, v_bias, rb)
            act_scr[rb * GRID_W:(rb + 1) * GRID_W, gs] = (cg * jax.nn.sigmoid(cg)) * cv

    x2 = x + m[5:6] * _bdot(act_scr[...], wdn_ref[...])
    ms = jnp.mean(x2 * x2, axis=-1, keepdims=True)
    o_ref[0] = (x2 * lax.rsqrt(ms + EPS)) * gf_ref[...]


def _ffn(x1, mod3, g_ffn, w_up, conv_w9, conv_b, w_down, g_final):
    bsz, length, _ = x1.shape
    tt = FFN_TILE
    rows_per_tile = tt // GRID_W
    n_rows = length // GRID_W
    halo = lambda fn: pl.BlockSpec((1, GRID_W, D_MODEL), fn)
    return pl.pallas_call(
        _ffn_kernel,
        grid=(bsz, length // tt),
        in_specs=[halo(lambda b, i: (b, jnp.maximum(i * rows_per_tile - 1, 0), 0)),
                  pl.BlockSpec((1, tt, D_MODEL), lambda b, i: (b, i, 0)),
                  halo(lambda b, i: (b, jnp.minimum((i + 1) * rows_per_tile, n_rows - 1), 0)),
                  pl.BlockSpec((1, N_MOD, D_MODEL), lambda b, i: (b, 0, 0)),
                  _const_spec(g_ffn.shape), _const_spec(w_up.shape), _const_spec(conv_w9.shape),
                  _const_spec(conv_b.shape), _const_spec(w_down.shape), _const_spec(g_final.shape)],
        out_specs=pl.BlockSpec((1, tt, D_MODEL), lambda b, i: (b, i, 0)),
        out_shape=jax.ShapeDtypeStruct(x1.shape, F32),
        scratch_shapes=[pltpu.VMEM((tt + 2 * GRID_W, D_MODEL), BF16),
                        pltpu.VMEM((tt, FFN_HIDDEN), BF16)],
        compiler_params=_cparams(("parallel", "parallel")),
        name="ffn",
    )(x1, x1, x1, mod3, g_ffn, w_up, conv_w9, conv_b, w_down, g_final)


def kernel(x, c, ctx, c_ctx, w_ada, b_ada, g_mix, w_in, s5_a_re, s5_a_im, s5_log_step, s5_b_re, s5_b_im, s5_c_re, s5_c_im, s5_d, s5_w_glu, s5_b_glu, sgu_ln_g, sgu_ln_b, sgu_w, sgu_b, w_proj_a, w_proj_b, b_gate, w_out, g_ffn, w_up, conv_w, conv_b, w_down, g_final):
    bsz = x.shape[0]
    row = lambda t: t.reshape(1, -1)

    ctx_row = bsz
    c_rows = jnp.zeros((SUBLANES, D_MODEL), F32).at[:bsz].set(c).at[ctx_row].set(c_ctx)
    mod3 = _ada_rows(c_rows, w_ada[0], b_ada[0]).reshape(SUBLANES, N_MOD, D_MODEL)

    w_in_b = w_in[0].astype(BF16)
    wsp = sgu_w[0].astype(BF16)
    wsp_pair = jnp.concatenate([wsp[0::2], wsp[1::2]], axis=-1)
    bsp_rows = jnp.repeat(sgu_b[0].T, SGU_GROUP_DIM, axis=1)

    u_pairs, ga, mb = _inproj(x, mod3, row(g_mix[0]), w_in_b, row(b_gate[0]), row(sgu_ln_g[0]),
                              row(sgu_ln_b[0]), wsp_pair, bsp_rows, w_proj_b[0].astype(BF16))
    uc_pairs = _ctxproj(ctx, mod3, ctx_row, row(g_mix[0]), w_in_b[:, :S5_WIDTH])

    w_state, w_y, coef = _s5_operators(
        s5_a_re[0], s5_a_im[0], s5_log_step[0], s5_b_re[0], s5_b_im[0], s5_c_re[0], s5_c_im[0],
        s5_d[0])
    y_pairs = _s5(u_pairs, uc_pairs, bsz, w_state, w_y, coef)

    x1 = _mix(y_pairs, x, ga, mb, mod3, s5_w_glu[0].astype(BF16), row(s5_b_glu[0]),
              w_proj_a[0].astype(BF16), w_out[0].astype(BF16))

    return _ffn(x1, mod3, row(g_ffn[0]), w_up[0].astype(BF16), conv_w[0].reshape(9, -1),
                row(conv_b[0]), w_down[0].astype(BF16), row(g_final))
```

```python
import math

import jax
import jax.numpy as jnp
from jax import lax
from jax.experimental import pallas as pl
from jax.experimental.pallas import tpu as pltpu

F32 = jnp.float32
BF16 = jnp.bfloat16

D_MODEL = 1024
GRID_W = 64
S5_WIDTH = 512
S5_GROUP = 16
S5_GROUPS = S5_WIDTH // S5_GROUP
S5_STATE = 64
SGU_WIDTH = 512
SGU_GROUPS = 8
SGU_GROUP_DIM = SGU_WIDTH // SGU_GROUPS
CHUNK = 128
FFN_HIDDEN = 2816
N_MOD = 6
EPS = 1e-6

LANES = 128
SUBLANES = 8
V7X_VMEM_LIMIT = 56 * 1024 * 1024

S5_CHUNK = 16
S5_PAIRS = S5_GROUPS // 2
PAIR_W = 2 * S5_CHUNK * S5_GROUP
S5_SLABS = S5_WIDTH // LANES
PAIRS_PER_SLAB = S5_PAIRS // S5_SLABS

TOK_TILE = 1024
ROW_GROUPS = 4
FFN_TILE = 512
FFN_CHUNK = 256


def _cparams(sem):
    return pltpu.CompilerParams(dimension_semantics=sem, vmem_limit_bytes=V7X_VMEM_LIMIT)


def _const_spec(shape):
    zeros = (0,) * len(shape)
    return pl.BlockSpec(shape, lambda *_: zeros, pipeline_mode=pl.Buffered(1))


def _gelu_tanh(x):
    c = math.sqrt(2.0 / math.pi)
    return 0.5 * x * (1.0 + jnp.tanh(c * (x + 0.044715 * (x * x * x))))


def _modulated_rms(x, g, shift, scale):
    ms = jnp.mean(x * x, axis=-1, keepdims=True)
    return (x * lax.rsqrt(ms + EPS)) * (g * (1.0 + scale)) + shift


def _bdot(a, b):
    return jnp.dot(a, b, preferred_element_type=F32)


def _ada_kernel(c_ref, w_ref, b_ref, o_ref):
    c = c_ref[...]
    cs = c * jax.nn.sigmoid(c)
    o_ref[...] = jnp.dot(cs, w_ref[...], preferred_element_type=F32,
                         precision=lax.Precision.HIGHEST) + b_ref[...]


def _ada_rows(c_rows, w_ada, b_ada):
    n = w_ada.shape[1]
    bn = 1536
    return pl.pallas_call(
        _ada_kernel,
        grid=(n // bn,),
        in_specs=[_const_spec(c_rows.shape),
                  pl.BlockSpec((D_MODEL, bn), lambda j: (0, j)),
                  pl.BlockSpec((1, bn), lambda j: (0, j))],
        out_specs=pl.BlockSpec((c_rows.shape[0], bn), lambda j: (0, j)),
        out_shape=jax.ShapeDtypeStruct((c_rows.shape[0], n), F32),
        compiler_params=_cparams(("arbitrary",)),
        name="ada",
    )(c_rows, w_ada, b_ada.reshape(1, n))


def _swap_slab_and_piece(slabs):
    lane = lax.broadcasted_iota(jnp.int32, (1, LANES), 1)
    out = list(slabs)
    for k in range(3):
        width = S5_GROUP << k
        keep = ((lane >> (4 + k)) & 1) == 0
        nxt = list(out)
        for a0 in range(len(out)):
            if a0 & (1 << k):
                continue
            a1 = a0 | (1 << k)
            nxt[a0] = jnp.where(keep, out[a0], pltpu.roll(out[a1], width, 1))
            nxt[a1] = jnp.where(keep, pltpu.roll(out[a0], LANES - width, 1), out[a1])
        out = nxt
    return out


def _pair_lane_block(slab_idx):
    t_hi, g = divmod(slab_idx, SUBLANES)
    pp, gg = divmod(g, 2)
    return pp, gg * 2 + t_hi


def _store_pair_tiles(u, u_scr, up_ref, row0=0):
    n_chunks = u.shape[0] // S5_CHUNK
    chunks = slice(row0 // S5_CHUNK, row0 // S5_CHUNK + n_chunks)
    for j in range(S5_SLABS):
        u_scr[j, row0:row0 + u.shape[0], :] = u[:, j * LANES:(j + 1) * LANES]
    for j in range(S5_SLABS):
        slabs = _swap_slab_and_piece(
            [u_scr[j, pl.ds(row0 + t, n_chunks, stride=S5_CHUNK), :] for t in range(S5_CHUNK)])
        for a in range(S5_CHUNK):
            pp, blk = _pair_lane_block(a)
            up_ref[j * PAIRS_PER_SLAB + pp, chunks, blk * LANES:(blk + 1) * LANES] = (
                slabs[a].astype(BF16))


def _inproj_kernel(x_ref, mod_ref, g_ref, w_ref, bgate_ref, lng_ref, lnb_ref, wsp_ref, bsp_ref,
                   wpb_ref, u_ref, ga_ref, mb_ref, u_scr):
    m = mod_ref[0]
    tt = x_ref.shape[1]
    z0 = S5_WIDTH
    g0 = S5_WIDTH + 2 * SGU_WIDTH
    lane = lax.broadcasted_iota(jnp.int32, (CHUNK, LANES), 1)
    zero = jnp.zeros((CHUNK, LANES), BF16)

    groups = [slice(k * tt // ROW_GROUPS, (k + 1) * tt // ROW_GROUPS) for k in range(ROW_GROUPS)]
    proj = []
    for r in groups:
        h = _modulated_rms(x_ref[0, r, :], g_ref[...], m[0:1], m[1:2]).astype(BF16)
        proj.append(dict(
            u=_bdot(h, w_ref[:, 0:S5_WIDTH]),
            zv=_bdot(h, w_ref[:, z0 + SGU_WIDTH:z0 + 2 * SGU_WIDTH]),
            zu=_bdot(h, w_ref[:, z0:z0 + SGU_WIDTH]),
            ga=_bdot(h, w_ref[:, g0:g0 + D_MODEL]),
            gb=_bdot(h, w_ref[:, g0 + D_MODEL:g0 + 2 * D_MODEL])))

    for r, p in zip(groups, proj):
        _store_pair_tiles(p["u"], u_scr, u_ref, r.start)

        zv = _gelu_tanh(p["zv"])
        mu = jnp.mean(zv, axis=-1, keepdims=True)
        zc = zv - mu
        var = jnp.mean(zc * zc, axis=-1, keepdims=True)
        v = ((zc * lax.rsqrt(var + EPS)) * lng_ref[...] + lnb_ref[...]).astype(BF16)
        zu = _gelu_tanh(p["zu"])

        rows = []
        for c in range(zv.shape[0] // CHUNK):
            tiles = []
            for j in range(SGU_WIDTH // LANES):
                vj = v[c * CHUNK:(c + 1) * CHUNK, j * LANES:(j + 1) * LANES]
                rhs = jnp.concatenate([jnp.where(lane < SGU_GROUP_DIM, vj, zero),
                                       jnp.where(lane >= SGU_GROUP_DIM, vj, zero)], axis=0)
                tiles.append(_bdot(wsp_ref[j], rhs))
            rows.append(jnp.concatenate(tiles, axis=1) + bsp_ref[...])
        s = jnp.concatenate(rows, axis=0)
        pb = _bdot((zu * s).astype(BF16), wpb_ref[...])

        ga = jax.nn.sigmoid(p["ga"] + bgate_ref[:, 0:D_MODEL])
        gb = jax.nn.sigmoid(p["gb"] + bgate_ref[:, D_MODEL:2 * D_MODEL])
        ga_ref[0, r, :] = ga.astype(BF16)
        mb_ref[0, r, :] = (gb * pb).astype(BF16)


def _inproj(x, mod3, g_mix, w_in, b_gate, ln_g, ln_b, wsp_pair, bsp_rows, w_proj_b):
    bsz, length, _ = x.shape
    tt = TOK_TILE
    nt = length // tt
    tok = lambda w: pl.BlockSpec((1, tt, w), lambda b, i: (b, i, 0))
    return pl.pallas_call(
        _inproj_kernel,
        grid=(bsz, nt),
        in_specs=[tok(D_MODEL),
                  pl.BlockSpec((1, N_MOD, D_MODEL), lambda b, i: (b, 0, 0)),
                  _const_spec(g_mix.shape), _const_spec(w_in.shape), _const_spec(b_gate.shape),
                  _const_spec(ln_g.shape), _const_spec(ln_b.shape), _const_spec(wsp_pair.shape),
                  _const_spec(bsp_rows.shape), _const_spec(w_proj_b.shape)],
        out_specs=[pl.BlockSpec((S5_PAIRS, tt // S5_CHUNK, PAIR_W), lambda b, i: (0, b * nt + i, 0)),
                   tok(D_MODEL), tok(D_MODEL)],
        out_shape=[jax.ShapeDtypeStruct((S5_PAIRS, bsz * length // S5_CHUNK, PAIR_W), BF16),
                   jax.ShapeDtypeStruct((bsz, length, D_MODEL), BF16),
                   jax.ShapeDtypeStruct((bsz, length, D_MODEL), BF16)],
        scratch_shapes=[pltpu.VMEM((S5_SLABS, tt, LANES), F32)],
        compiler_params=_cparams(("parallel", "parallel")),
        name="inproj",
    )(x, mod3, g_mix, w_in, b_gate, ln_g, ln_b, wsp_pair, bsp_rows, w_proj_b)


def _ctxproj_kernel(x_ref, mod_ref, g_ref, w_ref, u_ref, u_scr):
    m = mod_ref[0]
    h = _modulated_rms(x_ref[0], g_ref[...], m[0:1], m[1:2]).astype(BF16)
    _store_pair_tiles(_bdot(h, w_ref[...]), u_scr, u_ref)


def _ctxproj(ctx, mod3, ctx_row, g_mix, w_in_u):
    bsz, length, _ = ctx.shape
    return pl.pallas_call(
        _ctxproj_kernel,
        grid=(bsz,),
        in_specs=[pl.BlockSpec((1, length, D_MODEL), lambda b: (b, 0, 0)),
                  pl.BlockSpec((1, N_MOD, D_MODEL), lambda b: (ctx_row, 0, 0)),
                  _const_spec(g_mix.shape), _const_spec(w_in_u.shape)],
        out_specs=pl.BlockSpec((S5_PAIRS, length // S5_CHUNK, PAIR_W), lambda b: (0, b, 0)),
        out_shape=jax.ShapeDtypeStruct((S5_PAIRS, bsz * length // S5_CHUNK, PAIR_W), BF16),
        scratch_shapes=[pltpu.VMEM((S5_SLABS, length, LANES), F32)],
        compiler_params=_cparams(("parallel",)),
        name="ctxproj",
    )(ctx, mod3, g_mix, w_in_u)


def _cmul(ar, ai, br, bi):
    return ar * br - ai * bi, ar * bi + ai * br


def _s5_ops_kernel(a_re_ref, a_im_ref, ls_ref, bt_re_ref, bt_im_ref, c_re_ref, c_im_ref, d_ref,
                   wst_ref, wy_ref, coef_ref):
    hp = lax.Precision.HIGHEST
    n_tok = S5_CHUNK
    blk = n_tok * S5_GROUP
    tok_of_lane = lax.broadcasted_iota(jnp.int32, (1, blk), 1) // S5_GROUP
    row8 = lax.broadcasted_iota(jnp.int32, (SUBLANES, 1), 0)
    diag = (lax.broadcasted_iota(jnp.int32, (blk, blk), 0)
            == lax.broadcasted_iota(jnp.int32, (blk, blk), 1))
    contract_n = (((1,), (1,)), ((), ()))

    for gg in range(2):
        rows_g = slice(gg * blk, (gg + 1) * blk)
        cols_g = slice(gg * blk, (gg + 1) * blk)
        cols_other = slice((1 - gg) * blk, (2 - gg) * blk)
        w_intra = jnp.where(diag, d_ref[0, gg], 0.0)
        for d in range(2):
            forward = d == 0
            a_re = a_re_ref[d, 0, gg]
            a_im = a_im_ref[d, 0, gg]
            dt = jnp.exp(ls_ref[d, 0, gg])
            mag = jnp.exp(a_re * dt)
            lam = (mag * jnp.cos(a_im * dt), mag * jnp.sin(a_im * dt))
            den = a_re * a_re + a_im * a_im
            p, q = lam[0] - 1.0, lam[1]
            k = ((p * a_re + q * a_im) / den, (q * a_re - p * a_im) / den)
            bbar = _cmul(*k, bt_re_ref[d, 0, gg], bt_im_ref[d, 0, gg])
            c = (c_re_ref[d, 0, gg], c_im_ref[d, 0, gg])
            pw = [(jnp.ones_like(p), jnp.zeros_like(p))]
            for _ in range(n_tok):
                pw.append(_cmul(*pw[-1], *lam))

            expo = [n_tok - 1 - i if forward else i for i in range(n_tok)]
            cols = [_cmul(*pw[e], *bbar) for e in expo]
            m_re = jnp.concatenate([t[0] for t in cols], axis=0)
            m_im = jnp.concatenate([t[1] for t in cols], axis=0)
            for comp, m in ((2 * d, m_re), (2 * d + 1, m_im)):
                lo = comp * LANES + gg * S5_STATE
                other = comp * LANES + (1 - gg) * S5_STATE
                wst_ref[0, rows_g, lo:lo + S5_STATE] = m.astype(BF16)
                wst_ref[0, rows_g, other:other + S5_STATE] = jnp.zeros((blk, S5_STATE), BF16)

            c_rep = jnp.concatenate([jnp.concatenate([c[0], -c[1]], axis=1)] * n_tok, axis=0)
            kr = lax.dot_general(jnp.concatenate([m_re, m_im], axis=1), c_rep, contract_n,
                                 precision=hp, preferred_element_type=F32)
            toep = jnp.zeros((blk, blk), F32)
            for j in range(n_tok):
                s = S5_GROUP * (n_tok - 1 - j if forward else j)
                if s == 0:
                    moved = kr
                elif forward:
                    moved = jnp.concatenate([kr[s:], jnp.zeros((s, blk), F32)], axis=0)
                else:
                    moved = jnp.concatenate([jnp.zeros((s, blk), F32), kr[:blk - s]], axis=0)
                toep = jnp.where(tok_of_lane == j, moved, toep)
            w_intra = w_intra + toep

            expo = [j + 1 if forward else n_tok - j for j in range(n_tok)]
            rows = [_cmul(*pw[e], *c) for e in expo]
            r_re = jnp.concatenate([t[0] for t in rows], axis=0).T
            r_im = -jnp.concatenate([t[1] for t in rows], axis=0).T
            for comp, r in ((2 * d, r_re), (2 * d + 1, r_im)):
                r0 = PAIR_W + comp * LANES + gg * S5_STATE
                wy_ref[0, r0:r0 + S5_STATE, cols_g] = r.astype(BF16)
                wy_ref[0, r0:r0 + S5_STATE, cols_other] = jnp.zeros((S5_STATE, blk), BF16)

            lp = [(jnp.ones_like(p), jnp.zeros_like(p)), pw[n_tok]]
            for _ in range(SUBLANES - 1):
                lp.append(_cmul(*lp[-1], *pw[n_tok]))
            lanes = slice(d * LANES + gg * S5_STATE, d * LANES + (gg + 1) * S5_STATE)
            for kk, s in enumerate((1, 2, 4)):
                valid = (row8 >= s) if forward else (row8 + s < SUBLANES)
                coef_ref[0, 2 * kk, :, lanes] = jnp.where(valid, lp[s][0], 0.0)
                coef_ref[0, 2 * kk + 1, :, lanes] = jnp.where(valid, lp[s][1], 0.0)
            q_re = jnp.zeros((SUBLANES, S5_STATE), F32)
            q_im = jnp.zeros((SUBLANES, S5_STATE), F32)
            for r in range(SUBLANES):
                e = r + 1 if forward else SUBLANES - r
                q_re = jnp.where(row8 == r, lp[e][0], q_re)
                q_im = jnp.where(row8 == r, lp[e][1], q_im)
            coef_ref[0, 6, :, lanes] = q_re
            coef_ref[0, 7, :, lanes] = q_im

        wy_ref[0, rows_g, cols_g] = w_intra.astype(BF16)
        wy_ref[0, rows_g, cols_other] = jnp.zeros((blk, blk), BF16)


def _s5_operators(a_re, a_im, log_step, b_re, b_im, c_re, c_im, d_skip):
    grp = lambda t, *tail: t.reshape(2, S5_PAIRS, 2, *tail)
    row = lambda t: grp(t, 1, t.shape[-1])
    bt_re = grp(jnp.swapaxes(b_re, -1, -2), S5_GROUP, S5_STATE)
    bt_im = grp(jnp.swapaxes(b_im, -1, -2), S5_GROUP, S5_STATE)
    d_tiled = jnp.tile(d_skip.reshape(S5_PAIRS, 2, 1, S5_GROUP), (1, 1, 1, S5_CHUNK))
    per_pair = lambda *tail: pl.BlockSpec((2, 1, 2) + tail, lambda p: (0, p, 0) + (0,) * len(tail))
    out = lambda *tail: pl.BlockSpec((1,) + tail, lambda p: (p,) + (0,) * len(tail))
    return pl.pallas_call(
        _s5_ops_kernel,
        grid=(S5_PAIRS,),
        in_specs=[per_pair(1, S5_STATE), per_pair(1, S5_STATE), per_pair(1, 1),
                  per_pair(S5_GROUP, S5_STATE), per_pair(S5_GROUP, S5_STATE),
                  per_pair(S5_GROUP, S5_STATE), per_pair(S5_GROUP, S5_STATE),
                  pl.BlockSpec((1, 2, 1, S5_CHUNK * S5_GROUP), lambda p: (p, 0, 0, 0))],
        out_specs=[out(PAIR_W, PAIR_W), out(2 * PAIR_W, PAIR_W), out(8, SUBLANES, 2 * LANES)],
        out_shape=[jax.ShapeDtypeStruct((S5_PAIRS, PAIR_W, PAIR_W), BF16),
                   jax.ShapeDtypeStruct((S5_PAIRS, 2 * PAIR_W, PAIR_W), BF16),
                   jax.ShapeDtypeStruct((S5_PAIRS, 8, SUBLANES, 2 * LANES), F32)],
        compiler_params=_cparams(("parallel",)),
        name="s5_ops",
    )(row(a_re), row(a_im), grp(log_step, 1, 1), bt_re, bt_im,
      grp(c_re, S5_GROUP, S5_STATE), grp(c_im, S5_GROUP, S5_STATE), d_tiled)


def _s5_kernel(u_ref, uc_ref, wst_ref, wy_ref, coef_ref, y_ref, s_a, sin_a, s_b, sin_b):
    n_main = u_ref.shape[1]
    n_ctx = uc_ref.shape[1]
    n_rows = n_ctx + n_main + n_ctx
    half = PAIRS_PER_SLAB // 2
    halves = [(list(range(0, half)), s_a, sin_a), (list(range(half, PAIRS_PER_SLAB)), s_b, sin_b)]

    for pairs, s_scr, _ in halves:
        for k, pp in enumerate(pairs):
            s_ctx = _bdot(uc_ref[pp], wst_ref[pp])
            s_scr[k, 0:n_ctx] = s_ctx
            s_scr[k, n_ctx:n_ctx + n_main] = _bdot(u_ref[pp], wst_ref[pp])
            s_scr[k, n_ctx + n_main:] = s_ctx
    y_intra = [_bdot(u_ref[pp], wy_ref[pp, 0:PAIR_W]) for pp in range(PAIRS_PER_SLAB)]

    row = lax.broadcasted_iota(jnp.int32, (SUBLANES, LANES), 0)
    n_groups = n_rows // SUBLANES
    n_iter = (n_ctx + n_main) // SUBLANES

    def group(s_scr, sin_scr, k, pp, base, c_re, c_im, lane0, forward):
        rows = pl.ds(base, SUBLANES)
        re_l = slice(lane0, lane0 + LANES)
        im_l = slice(lane0 + LANES, lane0 + 2 * LANES)
        cf = slice(0, LANES) if forward else slice(LANES, 2 * LANES)
        xr = s_scr[k, rows, re_l]
        xi = s_scr[k, rows, im_l]
        for step, sh in enumerate((1, 2, 4)):
            ar = coef_ref[pp, 2 * step, :, cf]
            ai = coef_ref[pp, 2 * step + 1, :, cf]
            amt = sh if forward else SUBLANES - sh
            sr = pltpu.roll(xr, amt, 0)
            si = pltpu.roll(xi, amt, 0)
            xr, xi = xr + (ar * sr - ai * si), xi + (ar * si + ai * sr)
        qr = coef_ref[pp, 6, :, cf]
        qi = coef_ref[pp, 7, :, cf]
        hr = xr + (qr * c_re - qi * c_im)
        hi = xi + (qr * c_im + qi * c_re)
        edge = 0 if forward else SUBLANES - 1
        amt = 1 if forward else SUBLANES - 1
        sin_scr[k, rows, re_l] = jnp.where(row == edge, c_re, pltpu.roll(hr, amt, 0))
        sin_scr[k, rows, im_l] = jnp.where(row == edge, c_im, pltpu.roll(hi, amt, 0))
        last = SUBLANES - 1 if forward else 0
        return (jnp.broadcast_to(hr[last:last + 1], (SUBLANES, LANES)),
                jnp.broadcast_to(hi[last:last + 1], (SUBLANES, LANES)))

    zero = jnp.zeros((SUBLANES, LANES), F32)
    main_rows = slice(n_ctx, n_ctx + n_main)
    for pairs, s_scr, sin_scr in halves:
        carry = [zero] * (4 * len(pairs))
        for g in range(n_iter):
            new = []
            for k, pp in enumerate(pairs):
                f_re, f_im, b_re, b_im = carry[4 * k:4 * k + 4]
                f_re, f_im = group(s_scr, sin_scr, k, pp, g * SUBLANES, f_re, f_im, 0, True)
                b_re, b_im = group(s_scr, sin_scr, k, pp, (n_groups - 1 - g) * SUBLANES,
                                   b_re, b_im, 2 * LANES, False)
                new += [f_re, f_im, b_re, b_im]
            carry = new
        for k, pp in enumerate(pairs):
            y_ref[pp] = y_intra[pp] + _bdot(sin_scr[k, main_rows, :].astype(BF16),
                                            wy_ref[pp, PAIR_W:2 * PAIR_W])


def _s5(u_pairs, uc_pairs, bsz, w_state, w_y, coef):
    n_main = u_pairs.shape[1] // bsz
    n_ctx = uc_pairs.shape[1] // bsz
    n_rows = n_main + 2 * n_ctx
    pps = PAIRS_PER_SLAB
    rows = lambda n: pl.BlockSpec((pps, n, PAIR_W), lambda j, b: (j, b, 0))
    return pl.pallas_call(
        _s5_kernel,
        grid=(S5_SLABS, bsz),
        in_specs=[rows(n_main), rows(n_ctx),
                  pl.BlockSpec((pps, PAIR_W, PAIR_W), lambda j, b: (j, 0, 0)),
                  pl.BlockSpec((pps, 2 * PAIR_W, PAIR_W), lambda j, b: (j, 0, 0)),
                  pl.BlockSpec((pps, 8, SUBLANES, 2 * LANES), lambda j, b: (j, 0, 0, 0))],
        out_specs=rows(n_main),
        out_shape=jax.ShapeDtypeStruct(u_pairs.shape, F32),
        scratch_shapes=[pltpu.VMEM((pps // 2, n_rows, PAIR_W), F32)] * 4,
        compiler_params=_cparams(("arbitrary", "arbitrary")),
        name="s5",
    )(u_pairs, uc_pairs, w_state, w_y, coef)


def _pair_tiles_to_tokens(y_ref, y_scr, slot):
    n_chunks = y_ref.shape[1]
    for j in range(S5_SLABS):
        slabs = []
        for a in range(S5_CHUNK):
            pp, blk = _pair_lane_block(a)
            slabs.append(y_ref[j * PAIRS_PER_SLAB + pp, :, blk * LANES:(blk + 1) * LANES])
        slabs = _swap_slab_and_piece(slabs)
        for t in range(S5_CHUNK):
            y_scr[slot, j, pl.ds(t, n_chunks, stride=S5_CHUNK), :] = slabs[t]


def _mix_kernel(y0_ref, ynext_ref, x_ref, ga_ref, mb_ref, mod_ref, wglu_ref, bglu_ref, wpa_ref,
                wout_ref, o_ref, y_scr):
    step = pl.program_id(0)
    cur = step % 2
    m = mod_ref[0]
    tt = x_ref.shape[1]

    @pl.when(step == 0)
    def _():
        _pair_tiles_to_tokens(y0_ref, y_scr, 0)

    groups = [slice(k * tt // ROW_GROUPS, (k + 1) * tt // ROW_GROUPS) for k in range(ROW_GROUPS)]
    ya = [_gelu_tanh(jnp.concatenate([y_scr[cur, j, r, :] for j in range(S5_SLABS)], axis=1))
          for r in groups]
    gate = [_bdot(t.astype(BF16), wglu_ref[...]) for t in ya]
    glu = [t * jax.nn.sigmoid(g + bglu_ref[...]) for t, g in zip(ya, gate)]
    pa = [_bdot(t.astype(BF16), wpa_ref[...]) for t in glu]
    merged = [(ga_ref[0, r, :].astype(F32) * p + mb_ref[0, r, :].astype(F32)).astype(BF16)
              for r, p in zip(groups, pa)]
    out = [_bdot(t, wout_ref[...]) for t in merged]
    for r, t in zip(groups, out):
        o_ref[0, r, :] = x_ref[0, r, :] + m[2:3] * t

    _pair_tiles_to_tokens(ynext_ref, y_scr, 1 - cur)


def _mix(y_pairs, x, ga, mb, mod3, w_glu, b_glu, w_proj_a, w_out):
    bsz, length, _ = x.shape
    tt = TOK_TILE
    nt = length // tt
    steps = bsz * nt
    tok = lambda w: pl.BlockSpec((1, tt, w), lambda s: (s // nt, s % nt, 0))
    y_tile = lambda fn: pl.BlockSpec((S5_PAIRS, tt // S5_CHUNK, PAIR_W), fn)
    return pl.pallas_call(
        _mix_kernel,
        grid=(steps,),
        in_specs=[y_tile(lambda s: (0, 0, 0)),
                  y_tile(lambda s: (0, jnp.minimum(s + 1, steps - 1), 0)),
                  tok(D_MODEL), tok(D_MODEL), tok(D_MODEL),
                  pl.BlockSpec((1, N_MOD, D_MODEL), lambda s: (s // nt, 0, 0)),
                  _const_spec(w_glu.shape), _const_spec(b_glu.shape),
                  _const_spec(w_proj_a.shape), _const_spec(w_out.shape)],
        out_specs=tok(D_MODEL),
        out_shape=jax.ShapeDtypeStruct(x.shape, F32),
        scratch_shapes=[pltpu.VMEM((2, S5_SLABS, tt, LANES), F32)],
        compiler_params=_cparams(("arbitrary",)),
        name="mix",
    )(y_pairs, y_pairs, x, ga, mb, mod3, w_glu, b_glu, w_proj_a, w_out)


def _conv_taps(cw_ref, cb_ref, cols):
    width = cols.stop - cols.start
    taps = [jnp.broadcast_to(cw_ref[k:k + 1, cols].astype(BF16), (GRID_W, width)) for k in range(9)]
    return taps, jnp.broadcast_to(cb_ref[:, cols].astype(BF16), (GRID_W, width))


def _token_neighbours(u):
    n, ch = u.shape
    shape4 = (n // GRID_W, GRID_W // SUBLANES, SUBLANES, ch)
    sub = lax.broadcasted_iota(jnp.int32, (1, 1, SUBLANES, 1), 2)
    zero = jnp.zeros((shape4[0], 1, SUBLANES, ch), F32)
    u4 = u.reshape(shape4)
    down = pltpu.roll(u4, 1, 2)
    left = jnp.where(sub == 0, jnp.concatenate([zero, down[:, :-1]], axis=1), down)
    up = pltpu.roll(u4, SUBLANES - 1, 2)
    right = jnp.where(sub == SUBLANES - 1, jnp.concatenate([up[:, 1:], zero], axis=1), up)
    return left.reshape(n, ch).astype(BF16), u.astype(BF16), right.reshape(n, ch).astype(BF16)


def _grid_conv_row(nbrs, taps, bias, rb):
    lo = rb * GRID_W
    part = []
    for dr in range(3):
        rows = slice(lo + dr * GRID_W, lo + (dr + 1) * GRID_W)
        left, here, right = (nbrs[dc][rows] * taps[3 * dr + dc] for dc in range(3))
        part.append((left + here) + right)
    return (part[0] + part[1]) + (part[2] + bias)


def _ffn_kernel(top_ref, x_ref, bot_ref, mod_ref, g_ref, wup_ref, cw_ref, cb_ref, wdn_ref, gf_ref,
                o_ref, h_scr, act_scr):
    i = pl.program_id(1)
    n_i = pl.num_programs(1)
    m = mod_ref[0]
    g = g_ref[...]
    tt = x_ref.shape[1]
    x = x_ref[0]

    def norm(t):
        return _modulated_rms(t, g, m[3:4], m[4:5])

    h_scr[0:GRID_W] = jnp.where(i > 0, norm(top_ref[0]), 0.0).astype(BF16)
    h_scr[GRID_W:GRID_W + tt] = norm(x).astype(BF16)
    h_scr[GRID_W + tt:] = jnp.where(i < n_i - 1, norm(bot_ref[0]), 0.0).astype(BF16)

    h = h_scr[...]

    fc = FFN_CHUNK
    for j in range(FFN_HIDDEN // fc):
        gs = slice(j * fc, (j + 1) * fc)
        vs = slice(FFN_HIDDEN + j * fc, FFN_HIDDEN + (j + 1) * fc)
        ug = _token_neighbours(_bdot(h, wup_ref[:, gs]))
        uv = _token_neighbours(_bdot(h, wup_ref[:, vs]))
        g_taps, g_bias = _conv_taps(cw_ref, cb_ref, gs)
        v_taps, v_bias = _conv_taps(cw_ref, cb_ref, vs)
        for rb in range(tt // GRID_W):
            cg = _grid_conv_row(ug, g_taps, g_bias, rb)
            cv = _grid_conv_row(uv, v_taps, v_bias, rb)
            act_scr[rb * GRID_W:(rb + 1) * GRID_W, gs] = (cg * jax.nn.sigmoid(cg)) * cv

    x2 = x + m[5:6] * _bdot(act_scr[...], wdn_ref[...])
    ms = jnp.mean(x2 * x2, axis=-1, keepdims=True)
    o_ref[0] = (x2 * lax.rsqrt(ms + EPS)) * gf_ref[...]


def _ffn(x1, mod3, g_ffn, w_up, conv_w9, conv_b, w_down, g_final):
    bsz, length, _ = x1.shape
    tt = FFN_TILE
    rows_per_tile = tt // GRID_W
    n_rows = length // GRID_W
    halo = lambda fn: pl.BlockSpec((1, GRID_W, D_MODEL), fn)
    return pl.pallas_call(
        _ffn_kernel,
        grid=(bsz, length // tt),
        in_specs=[halo(lambda b, i: (b, jnp.maximum(i * rows_per_tile - 1, 0), 0)),
                  pl.BlockSpec((1, tt, D_MODEL), lambda b, i: (b, i, 0)),
                  halo(lambda b, i: (b, jnp.minimum((i + 1) * rows_per_tile, n_rows - 1), 0)),
                  pl.BlockSpec((1, N_MOD, D_MODEL), lambda b, i: (b, 0, 0)),
                  _const_spec(g_ffn.shape), _const_spec(w_up.shape), _const_spec(conv_w9.shape),
                  _const_spec(conv_b.shape), _const_spec(w_down.shape), _const_spec(g_final.shape)],
        out_specs=pl.BlockSpec((1, tt, D_MODEL), lambda b, i: (b, i, 0)),
        out_shape=jax.ShapeDtypeStruct(x1.shape, F32),
        scratch_shapes=[pltpu.VMEM((tt + 2 * GRID_W, D_MODEL), BF16),
                        pltpu.VMEM((tt, FFN_HIDDEN), BF16)],
        compiler_params=_cparams(("parallel", "parallel")),
        name="ffn",
    )(x1, x1, x1, mod3, g_ffn, w_up, conv_w9, conv_b, w_down, g_final)


def kernel(x, c, ctx, c_ctx, w_ada, b_ada, g_mix, w_in, s5_a_re, s5_a_im, s5_log_step, s5_b_re, s5_b_im, s5_c_re, s5_c_im, s5_d, s5_w_glu, s5_b_glu, sgu_ln_g, sgu_ln_b, sgu_w, sgu_b, w_proj_a, w_proj_b, b_gate, w_out, g_ffn, w_up, conv_w, conv_b, w_down, g_final):
    bsz = x.shape[0]
    row = lambda t: t.reshape(1, -1)

    ctx_row = bsz
    c_rows = jnp.zeros((SUBLANES, D_MODEL), F32).at[:bsz].set(c).at[ctx_row].set(c_ctx)
    mod3 = _ada_rows(c_rows, w_ada[0], b_ada[0]).reshape(SUBLANES, N_MOD, D_MODEL)

    w_in_b = w_in[0].astype(BF16)
    wsp = sgu_w[0].astype(BF16)
    wsp_pair = jnp.concatenate([wsp[0::2], wsp[1::2]], axis=-1)
    bsp_rows = jnp.repeat(sgu_b[0].T, SGU_GROUP_DIM, axis=1)

    u_pairs, ga, mb = _inproj(x, mod3, row(g_mix[0]), w_in_b, row(b_gate[0]), row(sgu_ln_g[0]),
                              row(sgu_ln_b[0]), wsp_pair, bsp_rows, w_proj_b[0].astype(BF16))
    uc_pairs = _ctxproj(ctx, mod3, ctx_row, row(g_mix[0]), w_in_b[:, :S5_WIDTH])

    w_state, w_y, coef = _s5_operators(
        s5_a_re[0], s5_a_im[0], s5_log_step[0], s5_b_re[0], s5_b_im[0], s5_c_re[0], s5_c_im[0],
        s5_d[0])
    y_pairs = _s5(u_pairs, uc_pairs, bsz, w_state, w_y, coef)

    x1 = _mix(y_pairs, x, ga, mb, mod3, s5_w_glu[0].astype(BF16), row(s5_b_glu[0]),
              w_proj_a[0].astype(BF16), w_out[0].astype(BF16))

    return _ffn(x1, mod3, row(g_ffn[0]), w_up[0].astype(BF16), conv_w[0].reshape(9, -1),
                row(conv_b[0]), w_down[0].astype(BF16), row(g_final))
```

```python
import math

import jax
import jax.numpy as jnp
from jax import lax
from jax.experimental import pallas as pl
from jax.experimental.pallas import tpu as pltpu

F32 = jnp.float32
BF16 = jnp.bfloat16

D_MODEL = 1024
GRID_W = 64
S5_WIDTH = 512
S5_GROUP = 16
S5_GROUPS = S5_WIDTH // S5_GROUP
S5_STATE = 64
SGU_WIDTH = 512
SGU_GROUPS = 8
SGU_GROUP_DIM = SGU_WIDTH // SGU_GROUPS
CHUNK = 128
FFN_HIDDEN = 2816
N_MOD = 6
EPS = 1e-6

LANES = 128
SUBLANES = 8
V7X_VMEM_LIMIT = 56 * 1024 * 1024

S5_CHUNK = 16
S5_PAIRS = S5_GROUPS // 2
PAIR_W = 2 * S5_CHUNK * S5_GROUP
S5_SLABS = S5_WIDTH // LANES
PAIRS_PER_SLAB = S5_PAIRS // S5_SLABS

TOK_TILE = 1024
ROW_GROUPS = 4
MIX_ROW_GROUPS = 2
FFN_TILE = 512
FFN_CHUNK = 256


def _cparams(sem):
    return pltpu.CompilerParams(dimension_semantics=sem, vmem_limit_bytes=V7X_VMEM_LIMIT)


def _const_spec(shape):
    zeros = (0,) * len(shape)
    return pl.BlockSpec(shape, lambda *_: zeros, pipeline_mode=pl.Buffered(1))


def _gelu_tanh(x):
    c = math.sqrt(2.0 / math.pi)
    return 0.5 * x * (1.0 + jnp.tanh(c * (x + 0.044715 * (x * x * x))))


def _modulated_rms(x, g, shift, scale):
    ms = jnp.mean(x * x, axis=-1, keepdims=True)
    return (x * lax.rsqrt(ms + EPS)) * (g * (1.0 + scale)) + shift


def _bdot(a, b):
    return jnp.dot(a, b, preferred_element_type=F32)


def _ada_kernel(c_ref, w_ref, b_ref, o_ref):
    c = c_ref[...]
    cs = c * jax.nn.sigmoid(c)
    o_ref[...] = jnp.dot(cs, w_ref[...], preferred_element_type=F32,
                         precision=lax.Precision.HIGHEST) + b_ref[...]


def _ada_rows(c_rows, w_ada, b_ada):
    n = w_ada.shape[1]
    bn = 1536
    return pl.pallas_call(
        _ada_kernel,
        grid=(n // bn,),
        in_specs=[_const_spec(c_rows.shape),
                  pl.BlockSpec((D_MODEL, bn), lambda j: (0, j)),
                  pl.BlockSpec((1, bn), lambda j: (0, j))],
        out_specs=pl.BlockSpec((c_rows.shape[0], bn), lambda j: (0, j)),
        out_shape=jax.ShapeDtypeStruct((c_rows.shape[0], n), F32),
        compiler_params=_cparams(("arbitrary",)),
        name="ada",
    )(c_rows, w_ada, b_ada.reshape(1, n))


def _swap_slab_and_piece(slabs):
    lane = lax.broadcasted_iota(jnp.int32, (1, LANES), 1)
    out = list(slabs)
    for k in range(3):
        width = S5_GROUP << k
        keep = ((lane >> (4 + k)) & 1) == 0
        nxt = list(out)
        for a0 in range(len(out)):
            if a0 & (1 << k):
                continue
            a1 = a0 | (1 << k)
            nxt[a0] = jnp.where(keep, out[a0], pltpu.roll(out[a1], width, 1))
            nxt[a1] = jnp.where(keep, pltpu.roll(out[a0], LANES - width, 1), out[a1])
        out = nxt
    return out


def _pair_lane_block(slab_idx):
    t_hi, g = divmod(slab_idx, SUBLANES)
    pp, gg = divmod(g, 2)
    return pp, gg * 2 + t_hi


def _store_pair_tiles(u, u_scr, up_ref, row0=0):
    n_chunks = u.shape[0] // S5_CHUNK
    chunks = slice(row0 // S5_CHUNK, row0 // S5_CHUNK + n_chunks)
    for j in range(S5_SLABS):
        u_scr[j, row0:row0 + u.shape[0], :] = u[:, j * LANES:(j + 1) * LANES]
    for j in range(S5_SLABS):
        slabs = _swap_slab_and_piece(
            [u_scr[j, pl.ds(row0 + t, n_chunks, stride=S5_CHUNK), :] for t in range(S5_CHUNK)])
        for a in range(S5_CHUNK):
            pp, blk = _pair_lane_block(a)
            up_ref[j * PAIRS_PER_SLAB + pp, chunks, blk * LANES:(blk + 1) * LANES] = (
                slabs[a].astype(BF16))


def _inproj_kernel(x_ref, mod_ref, g_ref, w_ref, bgate_ref, lng_ref, lnb_ref, wsp_ref, bsp_ref,
                   wpb_ref, u_ref, ga_ref, mb_ref, u_scr):
    m = mod_ref[0]
    tt = x_ref.shape[1]
    z0 = S5_WIDTH
    g0 = S5_WIDTH + 2 * SGU_WIDTH
    lane = lax.broadcasted_iota(jnp.int32, (CHUNK, LANES), 1)
    zero = jnp.zeros((CHUNK, LANES), BF16)

    groups = [slice(k * tt // ROW_GROUPS, (k + 1) * tt // ROW_GROUPS) for k in range(ROW_GROUPS)]
    proj = []
    for r in groups:
        h = _modulated_rms(x_ref[0, r, :], g_ref[...], m[0:1], m[1:2]).astype(BF16)
        proj.append(dict(
            u=_bdot(h, w_ref[:, 0:S5_WIDTH]),
            zv=_bdot(h, w_ref[:, z0 + SGU_WIDTH:z0 + 2 * SGU_WIDTH]),
            zu=_bdot(h, w_ref[:, z0:z0 + SGU_WIDTH]),
            ga=_bdot(h, w_ref[:, g0:g0 + D_MODEL]),
            gb=_bdot(h, w_ref[:, g0 + D_MODEL:g0 + 2 * D_MODEL])))

    for r, p in zip(groups, proj):
        _store_pair_tiles(p["u"], u_scr, u_ref, r.start)

        zv = _gelu_tanh(p["zv"])
        mu = jnp.mean(zv, axis=-1, keepdims=True)
        zc = zv - mu
        var = jnp.mean(zc * zc, axis=-1, keepdims=True)
        v = ((zc * lax.rsqrt(var + EPS)) * lng_ref[...] + lnb_ref[...]).astype(BF16)
        zu = _gelu_tanh(p["zu"])

        rows = []
        for c in range(zv.shape[0] // CHUNK):
            tiles = []
            for j in range(SGU_WIDTH // LANES):
                vj = v[c * CHUNK:(c + 1) * CHUNK, j * LANES:(j + 1) * LANES]
                rhs = jnp.concatenate([jnp.where(lane < SGU_GROUP_DIM, vj, zero),
                                       jnp.where(lane >= SGU_GROUP_DIM, vj, zero)], axis=0)
                tiles.append(_bdot(wsp_ref[j], rhs))
            rows.append(jnp.concatenate(tiles, axis=1) + bsp_ref[...])
        s = jnp.concatenate(rows, axis=0)
        pb = _bdot((zu * s).astype(BF16), wpb_ref[...])

        ga = jax.nn.sigmoid(p["ga"] + bgate_ref[:, 0:D_MODEL])
        gb = jax.nn.sigmoid(p["gb"] + bgate_ref[:, D_MODEL:2 * D_MODEL])
        ga_ref[0, r, :] = ga.astype(BF16)
        mb_ref[0, r, :] = (gb * pb).astype(BF16)


def _inproj(x, mod3, g_mix, w_in, b_gate, ln_g, ln_b, wsp_pair, bsp_rows, w_proj_b):
    bsz, length, _ = x.shape
    tt = TOK_TILE
    nt = length // tt
    tok = lambda w: pl.BlockSpec((1, tt, w), lambda b, i: (b, i, 0))
    return pl.pallas_call(
        _inproj_kernel,
        grid=(bsz, nt),
        in_specs=[tok(D_MODEL),
                  pl.BlockSpec((1, N_MOD, D_MODEL), lambda b, i: (b, 0, 0)),
                  _const_spec(g_mix.shape), _const_spec(w_in.shape), _const_spec(b_gate.shape),
                  _const_spec(ln_g.shape), _const_spec(ln_b.shape), _const_spec(wsp_pair.shape),
                  _const_spec(bsp_rows.shape), _const_spec(w_proj_b.shape)],
        out_specs=[pl.BlockSpec((S5_PAIRS, tt // S5_CHUNK, PAIR_W), lambda b, i: (0, b * nt + i, 0)),
                   tok(D_MODEL), tok(D_MODEL)],
        out_shape=[jax.ShapeDtypeStruct((S5_PAIRS, bsz * length // S5_CHUNK, PAIR_W), BF16),
                   jax.ShapeDtypeStruct((bsz, length, D_MODEL), BF16),
                   jax.ShapeDtypeStruct((bsz, length, D_MODEL), BF16)],
        scratch_shapes=[pltpu.VMEM((S5_SLABS, tt, LANES), F32)],
        compiler_params=_cparams(("parallel", "parallel")),
        name="inproj",
    )(x, mod3, g_mix, w_in, b_gate, ln_g, ln_b, wsp_pair, bsp_rows, w_proj_b)


def _ctxproj_kernel(x_ref, mod_ref, g_ref, w_ref, u_ref, u_scr):
    m = mod_ref[0]
    h = _modulated_rms(x_ref[0], g_ref[...], m[0:1], m[1:2]).astype(BF16)
    _store_pair_tiles(_bdot(h, w_ref[...]), u_scr, u_ref)


def _ctxproj(ctx, mod3, ctx_row, g_mix, w_in_u):
    bsz, length, _ = ctx.shape
    return pl.pallas_call(
        _ctxproj_kernel,
        grid=(bsz,),
        in_specs=[pl.BlockSpec((1, length, D_MODEL), lambda b: (b, 0, 0)),
                  pl.BlockSpec((1, N_MOD, D_MODEL), lambda b: (ctx_row, 0, 0)),
                  _const_spec(g_mix.shape), _const_spec(w_in_u.shape)],
        out_specs=pl.BlockSpec((S5_PAIRS, length // S5_CHUNK, PAIR_W), lambda b: (0, b, 0)),
        out_shape=jax.ShapeDtypeStruct((S5_PAIRS, bsz * length // S5_CHUNK, PAIR_W), BF16),
        scratch_shapes=[pltpu.VMEM((S5_SLABS, length, LANES), F32)],
        compiler_params=_cparams(("parallel",)),
        name="ctxproj",
    )(ctx, mod3, g_mix, w_in_u)


def _cmul(ar, ai, br, bi):
    return ar * br - ai * bi, ar * bi + ai * br


def _s5_ops_kernel(a_re_ref, a_im_ref, ls_ref, bt_re_ref, bt_im_ref, c_re_ref, c_im_ref, d_ref,
                   wst_ref, wy_ref, coef_ref):
    hp = lax.Precision.HIGHEST
    n_tok = S5_CHUNK
    blk = n_tok * S5_GROUP
    tok_of_lane = lax.broadcasted_iota(jnp.int32, (1, blk), 1) // S5_GROUP
    row8 = lax.broadcasted_iota(jnp.int32, (SUBLANES, 1), 0)
    diag = (lax.broadcasted_iota(jnp.int32, (blk, blk), 0)
            == lax.broadcasted_iota(jnp.int32, (blk, blk), 1))
    contract_n = (((1,), (1,)), ((), ()))

    for gg in range(2):
        rows_g = slice(gg * blk, (gg + 1) * blk)
        cols_g = slice(gg * blk, (gg + 1) * blk)
        cols_other = slice((1 - gg) * blk, (2 - gg) * blk)
        w_intra = jnp.where(diag, d_ref[0, gg], 0.0)
        for d in range(2):
            forward = d == 0
            a_re = a_re_ref[d, 0, gg]
            a_im = a_im_ref[d, 0, gg]
            dt = jnp.exp(ls_ref[d, 0, gg])
            mag = jnp.exp(a_re * dt)
            lam = (mag * jnp.cos(a_im * dt), mag * jnp.sin(a_im * dt))
            den = a_re * a_re + a_im * a_im
            p, q = lam[0] - 1.0, lam[1]
            k = ((p * a_re + q * a_im) / den, (q * a_re - p * a_im) / den)
            bbar = _cmul(*k, bt_re_ref[d, 0, gg], bt_im_ref[d, 0, gg])
            c = (c_re_ref[d, 0, gg], c_im_ref[d, 0, gg])
            pw = [(jnp.ones_like(p), jnp.zeros_like(p))]
            for _ in range(n_tok):
                pw.append(_cmul(*pw[-1], *lam))

            expo = [n_tok - 1 - i if forward else i for i in range(n_tok)]
            cols = [_cmul(*pw[e], *bbar) for e in expo]
            m_re = jnp.concatenate([t[0] for t in cols], axis=0)
            m_im = jnp.concatenate([t[1] for t in cols], axis=0)
            for comp, m in ((2 * d, m_re), (2 * d + 1, m_im)):
                lo = comp * LANES + gg * S5_STATE
                other = comp * LANES + (1 - gg) * S5_STATE
                wst_ref[0, rows_g, lo:lo + S5_STATE] = m.astype(BF16)
                wst_ref[0, rows_g, other:other + S5_STATE] = jnp.zeros((blk, S5_STATE), BF16)

            c_rep = jnp.concatenate([jnp.concatenate([c[0], -c[1]], axis=1)] * n_tok, axis=0)
            kr = lax.dot_general(jnp.concatenate([m_re, m_im], axis=1), c_rep, contract_n,
                                 precision=hp, preferred_element_type=F32)
            toep = jnp.zeros((blk, blk), F32)
            for j in range(n_tok):
                s = S5_GROUP * (n_tok - 1 - j if forward else j)
                if s == 0:
                    moved = kr
                elif forward:
                    moved = jnp.concatenate([kr[s:], jnp.zeros((s, blk), F32)], axis=0)
                else:
                    moved = jnp.concatenate([jnp.zeros((s, blk), F32), kr[:blk - s]], axis=0)
                toep = jnp.where(tok_of_lane == j, moved, toep)
            w_intra = w_intra + toep

            expo = [j + 1 if forward else n_tok - j for j in range(n_tok)]
            rows = [_cmul(*pw[e], *c) for e in expo]
            r_re = jnp.concatenate([t[0] for t in rows], axis=0).T
            r_im = -jnp.concatenate([t[1] for t in rows], axis=0).T
            for comp, r in ((2 * d, r_re), (2 * d + 1, r_im)):
                r0 = PAIR_W + comp * LANES + gg * S5_STATE
                wy_ref[0, r0:r0 + S5_STATE, cols_g] = r.astype(BF16)
                wy_ref[0, r0:r0 + S5_STATE, cols_other] = jnp.zeros((S5_STATE, blk), BF16)

            lp = [(jnp.ones_like(p), jnp.zeros_like(p)), pw[n_tok]]
            for _ in range(SUBLANES - 1):
                lp.append(_cmul(*lp[-1], *pw[n_tok]))
            lanes = slice(d * LANES + gg * S5_STATE, d * LANES + (gg + 1) * S5_STATE)
            for kk, s in enumerate((1, 2, 4)):
                valid = (row8 >= s) if forward else (row8 + s < SUBLANES)
                coef_ref[0, 2 * kk, :, lanes] = jnp.where(valid, lp[s][0], 0.0)
                coef_ref[0, 2 * kk + 1, :, lanes] = jnp.where(valid, lp[s][1], 0.0)
            q_re = jnp.zeros((SUBLANES, S5_STATE), F32)
            q_im = jnp.zeros((SUBLANES, S5_STATE), F32)
            for r in range(SUBLANES):
                e = r + 1 if forward else SUBLANES - r
                q_re = jnp.where(row8 == r, lp[e][0], q_re)
                q_im = jnp.where(row8 == r, lp[e][1], q_im)
            coef_ref[0, 6, :, lanes] = q_re
            coef_ref[0, 7, :, lanes] = q_im

        wy_ref[0, rows_g, cols_g] = w_intra.astype(BF16)
        wy_ref[0, rows_g, cols_other] = jnp.zeros((blk, blk), BF16)


def _s5_operators(a_re, a_im, log_step, b_re, b_im, c_re, c_im, d_skip):
    grp = lambda t, *tail: t.reshape(2, S5_PAIRS, 2, *tail)
    row = lambda t: grp(t, 1, t.shape[-1])
    bt_re = grp(jnp.swapaxes(b_re, -1, -2), S5_GROUP, S5_STATE)
    bt_im = grp(jnp.swapaxes(b_im, -1, -2), S5_GROUP, S5_STATE)
    d_tiled = jnp.tile(d_skip.reshape(S5_PAIRS, 2, 1, S5_GROUP), (1, 1, 1, S5_CHUNK))
    per_pair = lambda *tail: pl.BlockSpec((2, 1, 2) + tail, lambda p: (0, p, 0) + (0,) * len(tail))
    out = lambda *tail: pl.BlockSpec((1,) + tail, lambda p: (p,) + (0,) * len(tail))
    return pl.pallas_call(
        _s5_ops_kernel,
        grid=(S5_PAIRS,),
        in_specs=[per_pair(1, S5_STATE), per_pair(1, S5_STATE), per_pair(1, 1),
                  per_pair(S5_GROUP, S5_STATE), per_pair(S5_GROUP, S5_STATE),
                  per_pair(S5_GROUP, S5_STATE), per_pair(S5_GROUP, S5_STATE),
                  pl.BlockSpec((1, 2, 1, S5_CHUNK * S5_GROUP), lambda p: (p, 0, 0, 0))],
        out_specs=[out(PAIR_W, PAIR_W), out(2 * PAIR_W, PAIR_W), out(8, SUBLANES, 2 * LANES)],
        out_shape=[jax.ShapeDtypeStruct((S5_PAIRS, PAIR_W, PAIR_W), BF16),
                   jax.ShapeDtypeStruct((S5_PAIRS, 2 * PAIR_W, PAIR_W), BF16),
                   jax.ShapeDtypeStruct((S5_PAIRS, 8, SUBLANES, 2 * LANES), F32)],
        compiler_params=_cparams(("parallel",)),
        name="s5_ops",
    )(row(a_re), row(a_im), grp(log_step, 1, 1), bt_re, bt_im,
      grp(c_re, S5_GROUP, S5_STATE), grp(c_im, S5_GROUP, S5_STATE), d_tiled)


def _s5_kernel(u_ref, uc_ref, wst_ref, wy_ref, coef_ref, y_ref, s_a, sin_a, s_b, sin_b):
    n_main = u_ref.shape[1]
    n_ctx = uc_ref.shape[1]
    n_rows = n_ctx + n_main + n_ctx
    half = PAIRS_PER_SLAB // 2
    halves = [(list(range(0, half)), s_a, sin_a), (list(range(half, PAIRS_PER_SLAB)), s_b, sin_b)]

    for pairs, s_scr, _ in halves:
        for k, pp in enumerate(pairs):
            s_ctx = _bdot(uc_ref[pp], wst_ref[pp])
            s_scr[k, 0:n_ctx] = s_ctx
            s_scr[k, n_ctx:n_ctx + n_main] = _bdot(u_ref[pp], wst_ref[pp])
            s_scr[k, n_ctx + n_main:] = s_ctx
    y_intra = [_bdot(u_ref[pp], wy_ref[pp, 0:PAIR_W]) for pp in range(PAIRS_PER_SLAB)]

    row = lax.broadcasted_iota(jnp.int32, (SUBLANES, LANES), 0)
    n_groups = n_rows // SUBLANES
    n_iter = (n_ctx + n_main) // SUBLANES

    def group(s_scr, sin_scr, k, pp, base, c_re, c_im, lane0, forward):
        rows = pl.ds(base, SUBLANES)
        re_l = slice(lane0, lane0 + LANES)
        im_l = slice(lane0 + LANES, lane0 + 2 * LANES)
        cf = slice(0, LANES) if forward else slice(LANES, 2 * LANES)
        xr = s_scr[k, rows, re_l]
        xi = s_scr[k, rows, im_l]
        for step, sh in enumerate((1, 2, 4)):
            ar = coef_ref[pp, 2 * step, :, cf]
            ai = coef_ref[pp, 2 * step + 1, :, cf]
            amt = sh if forward else SUBLANES - sh
            sr = pltpu.roll(xr, amt, 0)
            si = pltpu.roll(xi, amt, 0)
            xr, xi = xr + (ar * sr - ai * si), xi + (ar * si + ai * sr)
        qr = coef_ref[pp, 6, :, cf]
        qi = coef_ref[pp, 7, :, cf]
        hr = xr + (qr * c_re - qi * c_im)
        hi = xi + (qr * c_im + qi * c_re)
        edge = 0 if forward else SUBLANES - 1
        amt = 1 if forward else SUBLANES - 1
        sin_scr[k, rows, re_l] = jnp.where(row == edge, c_re, pltpu.roll(hr, amt, 0))
        sin_scr[k, rows, im_l] = jnp.where(row == edge, c_im, pltpu.roll(hi, amt, 0))
        last = SUBLANES - 1 if forward else 0
        return (jnp.broadcast_to(hr[last:last + 1], (SUBLANES, LANES)),
                jnp.broadcast_to(hi[last:last + 1], (SUBLANES, LANES)))

    zero = jnp.zeros((SUBLANES, LANES), F32)
    main_rows = slice(n_ctx, n_ctx + n_main)
    for pairs, s_scr, sin_scr in halves:
        carry = [zero] * (4 * len(pairs))
        for g in range(n_iter):
            new = []
            for k, pp in enumerate(pairs):
                f_re, f_im, b_re, b_im = carry[4 * k:4 * k + 4]
                f_re, f_im = group(s_scr, sin_scr, k, pp, g * SUBLANES, f_re, f_im, 0, True)
                b_re, b_im = group(s_scr, sin_scr, k, pp, (n_groups - 1 - g) * SUBLANES,
                                   b_re, b_im, 2 * LANES, False)
                new += [f_re, f_im, b_re, b_im]
            carry = new
        for k, pp in enumerate(pairs):
            y_ref[pp] = y_intra[pp] + _bdot(sin_scr[k, main_rows, :].astype(BF16),
                                            wy_ref[pp, PAIR_W:2 * PAIR_W])


def _s5(u_pairs, uc_pairs, bsz, w_state, w_y, coef):
    n_main = u_pairs.shape[1] // bsz
    n_ctx = uc_pairs.shape[1] // bsz
    n_rows = n_main + 2 * n_ctx
    pps = PAIRS_PER_SLAB
    rows = lambda n: pl.BlockSpec((pps, n, PAIR_W), lambda j, b: (j, b, 0))
    return pl.pallas_call(
        _s5_kernel,
        grid=(S5_SLABS, bsz),
        in_specs=[rows(n_main), rows(n_ctx),
                  pl.BlockSpec((pps, PAIR_W, PAIR_W), lambda j, b: (j, 0, 0)),
                  pl.BlockSpec((pps, 2 * PAIR_W, PAIR_W), lambda j, b: (j, 0, 0)),
                  pl.BlockSpec((pps, 8, SUBLANES, 2 * LANES), lambda j, b: (j, 0, 0, 0))],
        out_specs=rows(n_main),
        out_shape=jax.ShapeDtypeStruct(u_pairs.shape, F32),
        scratch_shapes=[pltpu.VMEM((pps // 2, n_rows, PAIR_W), F32)] * 4,
        compiler_params=_cparams(("arbitrary", "arbitrary")),
        name="s5",
    )(u_pairs, uc_pairs, w_state, w_y, coef)


def _pair_tiles_to_tokens(y_ref, y_scr, slot):
    n_chunks = y_ref.shape[1]
    for j in range(S5_SLABS):
        slabs = []
        for a in range(S5_CHUNK):
            pp, blk = _pair_lane_block(a)
            slabs.append(y_ref[j * PAIRS_PER_SLAB + pp, :, blk * LANES:(blk + 1) * LANES])
        slabs = _swap_slab_and_piece(slabs)
        for t in range(S5_CHUNK):
            y_scr[slot, j, pl.ds(t, n_chunks, stride=S5_CHUNK), :] = slabs[t]


def _mix_kernel(y0_ref, ynext_ref, x_ref, ga_ref, mb_ref, mod_ref, wglu_ref, bglu_ref, wpa_ref,
                wout_ref, o_ref, y_scr):
    step = pl.program_id(0)
    cur = step % 2
    m = mod_ref[0]
    tt = x_ref.shape[1]

    @pl.when(step == 0)
    def _():
        _pair_tiles_to_tokens(y0_ref, y_scr, 0)

    rows = tt // MIX_ROW_GROUPS
    groups = [slice(k * rows, (k + 1) * rows) for k in range(MIX_ROW_GROUPS)]
    ya = [_gelu_tanh(jnp.concatenate([y_scr[cur, j, r, :] for j in range(S5_SLABS)], axis=1))
          for r in groups]
    gate = [_bdot(t.astype(BF16), wglu_ref[...]) for t in ya]
    glu = [t * jax.nn.sigmoid(g + bglu_ref[...]) for t, g in zip(ya, gate)]
    pa = [_bdot(t.astype(BF16), wpa_ref[...]) for t in glu]
    merged = [(ga_ref[0, r, :].astype(F32) * p + mb_ref[0, r, :].astype(F32)).astype(BF16)
              for r, p in zip(groups, pa)]
    out = [_bdot(t, wout_ref[...]) for t in merged]
    for r, t in zip(groups, out):
        o_ref[0, r, :] = x_ref[0, r, :] + m[2:3] * t

    _pair_tiles_to_tokens(ynext_ref, y_scr, 1 - cur)


def _mix(y_pairs, x, ga, mb, mod3, w_glu, b_glu, w_proj_a, w_out):
    bsz, length, _ = x.shape
    tt = TOK_TILE
    nt = length // tt
    steps = bsz * nt
    tok = lambda w: pl.BlockSpec((1, tt, w), lambda s: (s // nt, s % nt, 0))
    y_tile = lambda fn: pl.BlockSpec((S5_PAIRS, tt // S5_CHUNK, PAIR_W), fn)
    return pl.pallas_call(
        _mix_kernel,
        grid=(steps,),
        in_specs=[y_tile(lambda s: (0, 0, 0)),
                  y_tile(lambda s: (0, jnp.minimum(s + 1, steps - 1), 0)),
                  tok(D_MODEL), tok(D_MODEL), tok(D_MODEL),
                  pl.BlockSpec((1, N_MOD, D_MODEL), lambda s: (s // nt, 0, 0)),
                  _const_spec(w_glu.shape), _const_spec(b_glu.shape),
                  _const_spec(w_proj_a.shape), _const_spec(w_out.shape)],
        out_specs=tok(D_MODEL),
        out_shape=jax.ShapeDtypeStruct(x.shape, F32),
        scratch_shapes=[pltpu.VMEM((2, S5_SLABS, tt, LANES), F32)],
        compiler_params=_cparams(("arbitrary",)),
        name="mix",
    )(y_pairs, y_pairs, x, ga, mb, mod3, w_glu, b_glu, w_proj_a, w_out)


def _conv_taps(cw_ref, cb_ref, cols):
    width = cols.stop - cols.start
    taps = [jnp.broadcast_to(cw_ref[k:k + 1, cols].astype(BF16), (GRID_W, width)) for k in range(9)]
    return taps, jnp.broadcast_to(cb_ref[:, cols].astype(BF16), (GRID_W, width))


def _token_neighbours(u):
    n, ch = u.shape
    shape4 = (n // GRID_W, GRID_W // SUBLANES, SUBLANES, ch)
    sub = lax.broadcasted_iota(jnp.int32, (1, 1, SUBLANES, 1), 2)
    zero = jnp.zeros((shape4[0], 1, SUBLANES, ch), F32)
    u4 = u.reshape(shape4)
    down = pltpu.roll(u4, 1, 2)
    left = jnp.where(sub == 0, jnp.concatenate([zero, down[:, :-1]], axis=1), down)
    up = pltpu.roll(u4, SUBLANES - 1, 2)
    right = jnp.where(sub == SUBLANES - 1, jnp.concatenate([up[:, 1:], zero], axis=1), up)
    return left.reshape(n, ch).astype(BF16), u.astype(BF16), right.reshape(n, ch).astype(BF16)


def _grid_conv_row(nbrs, taps, bias, rb):
    lo = rb * GRID_W
    part = []
    for dr in range(3):
        rows = slice(lo + dr * GRID_W, lo + (dr + 1) * GRID_W)
        left, here, right = (nbrs[dc][rows] * taps[3 * dr + dc] for dc in range(3))
        part.append((left + here) + right)
    return (part[0] + part[1]) + (part[2] + bias)


def _ffn_kernel(top_ref, x_ref, bot_ref, mod_ref, g_ref, wup_ref, cw_ref, cb_ref, wdn_ref, gf_ref,
                o_ref, h_scr, act_scr):
    i = pl.program_id(1)
    n_i = pl.num_programs(1)
    m = mod_ref[0]
    g = g_ref[...]
    tt = x_ref.shape[1]
    x = x_ref[0]

    def norm(t):
        return _modulated_rms(t, g, m[3:4], m[4:5])

    h_scr[0:GRID_W] = jnp.where(i > 0, norm(top_ref[0]), 0.0).astype(BF16)
    h_scr[GRID_W:GRID_W + tt] = norm(x).astype(BF16)
    h_scr[GRID_W + tt:] = jnp.where(i < n_i - 1, norm(bot_ref[0]), 0.0).astype(BF16)

    h = h_scr[...]

    fc = FFN_CHUNK
    for j in range(FFN_HIDDEN // fc):
        gs = slice(j * fc, (j + 1) * fc)
        vs = slice(FFN_HIDDEN + j * fc, FFN_HIDDEN + (j + 1) * fc)
        ug = _token_neighbours(_bdot(h, wup_ref[:, gs]))
        uv = _token_neighbours(_bdot(h, wup_ref[:, vs]))
        g_taps, g_bias = _conv_taps(cw_ref, cb_ref, gs)
        v_taps, v_bias = _conv_taps(cw_ref, cb_ref, vs)
        for rb in range(tt // GRID_W):
            cg = _grid_conv_row(ug, g_taps, g_bias, rb)
            cv = _grid_conv_row(uv, v_taps, v_bias, rb)
            act_scr[rb * GRID_W:(rb + 1) * GRID_W, gs] = (cg * jax.nn.sigmoid(cg)) * cv

    x2 = x + m[5:6] * _bdot(act_scr[...], wdn_ref[...])
    ms = jnp.mean(x2 * x2, axis=-1, keepdims=True)
    o_ref[0] = (x2 * lax.rsqrt(ms + EPS)) * gf_ref[...]


def _ffn(x1, mod3, g_ffn, w_up, conv_w9, conv_b, w_down, g_final):
    bsz, length, _ = x1.shape
    tt = FFN_TILE
    rows_per_tile = tt // GRID_W
    n_rows = length // GRID_W
    halo = lambda fn: pl.BlockSpec((1, GRID_W, D_MODEL), fn)
    return pl.pallas_call(
        _ffn_kernel,
        grid=(bsz, length // tt),
        in_specs=[halo(lambda b, i: (b, jnp.maximum(i * rows_per_tile - 1, 0), 0)),
                  pl.BlockSpec((1, tt, D_MODEL), lambda b, i: (b, i, 0)),
                  halo(lambda b, i: (b, jnp.minimum((i + 1) * rows_per_tile, n_rows - 1), 0)),
                  pl.BlockSpec((1, N_MOD, D_MODEL), lambda b, i: (b, 0, 0)),
                  _const_spec(g_ffn.shape), _const_spec(w_up.shape), _const_spec(conv_w9.shape),
                  _const_spec(conv_b.shape), _const_spec(w_down.shape), _const_spec(g_final.shape)],
        out_specs=pl.BlockSpec((1, tt, D_MODEL), lambda b, i: (b, i, 0)),
        out_shape=jax.ShapeDtypeStruct(x1.shape, F32),
        scratch_shapes=[pltpu.VMEM((tt + 2 * GRID_W, D_MODEL), BF16),
                        pltpu.VMEM((tt, FFN_HIDDEN), BF16)],
        compiler_params=_cparams(("parallel", "parallel")),
        name="ffn",
    )(x1, x1, x1, mod3, g_ffn, w_up, conv_w9, conv_b, w_down, g_final)


def kernel(x, c, ctx, c_ctx, w_ada, b_ada, g_mix, w_in, s5_a_re, s5_a_im, s5_log_step, s5_b_re, s5_b_im, s5_c_re, s5_c_im, s5_d, s5_w_glu, s5_b_glu, sgu_ln_g, sgu_ln_b, sgu_w, sgu_b, w_proj_a, w_proj_b, b_gate, w_out, g_ffn, w_up, conv_w, conv_b, w_down, g_final):
    bsz = x.shape[0]
    row = lambda t: t.reshape(1, -1)

    ctx_row = bsz
    c_rows = jnp.zeros((SUBLANES, D_MODEL), F32).at[:bsz].set(c).at[ctx_row].set(c_ctx)
    mod3 = _ada_rows(c_rows, w_ada[0], b_ada[0]).reshape(SUBLANES, N_MOD, D_MODEL)

    w_in_b = w_in[0].astype(BF16)
    wsp = sgu_w[0].astype(BF16)
    wsp_pair = jnp.concatenate([wsp[0::2], wsp[1::2]], axis=-1)
    bsp_rows = jnp.repeat(sgu_b[0].T, SGU_GROUP_DIM, axis=1)

    u_pairs, ga, mb = _inproj(x, mod3, row(g_mix[0]), w_in_b, row(b_gate[0]), row(sgu_ln_g[0]),
                              row(sgu_ln_b[0]), wsp_pair, bsp_rows, w_proj_b[0].astype(BF16))
    uc_pairs = _ctxproj(ctx, mod3, ctx_row, row(g_mix[0]), w_in_b[:, :S5_WIDTH])

    w_state, w_y, coef = _s5_operators(
        s5_a_re[0], s5_a_im[0], s5_log_step[0], s5_b_re[0], s5_b_im[0], s5_c_re[0], s5_c_im[0],
        s5_d[0])
    y_pairs = _s5(u_pairs, uc_pairs, bsz, w_state, w_y, coef)

    x1 = _mix(y_pairs, x, ga, mb, mod3, s5_w_glu[0].astype(BF16), row(s5_b_glu[0]),
              w_proj_a[0].astype(BF16), w_out[0].astype(BF16))

    return _ffn(x1, mod3, row(g_ffn[0]), w_up[0].astype(BF16), conv_w[0].reshape(9, -1),
                row(conv_b[0]), w_down[0].astype(BF16), row(g_final))
```

```python
import math

import jax
import jax.numpy as jnp
from jax import lax
from jax.experimental import pallas as pl
from jax.experimental.pallas import tpu as pltpu

F32 = jnp.float32
BF16 = jnp.bfloat16

D_MODEL = 1024
GRID_W = 64
S5_WIDTH = 512
S5_GROUP = 16
S5_GROUPS = S5_WIDTH // S5_GROUP
S5_STATE = 64
SGU_WIDTH = 512
SGU_GROUPS = 8
SGU_GROUP_DIM = SGU_WIDTH // SGU_GROUPS
CHUNK = 128
FFN_HIDDEN = 2816
N_MOD = 6
EPS = 1e-6

LANES = 128
SUBLANES = 8
V7X_VMEM_LIMIT = 56 * 1024 * 1024

S5_CHUNK = 16
S5_PAIRS = S5_GROUPS // 2
PAIR_W = 2 * S5_CHUNK * S5_GROUP
S5_SLABS = S5_WIDTH // LANES
PAIRS_PER_SLAB = S5_PAIRS // S5_SLABS

TOK_TILE = 1024
ROW_GROUPS = 4
FFN_TILE = 512
FFN_CHUNK = 256


def _cparams(sem):
    return pltpu.CompilerParams(dimension_semantics=sem, vmem_limit_bytes=V7X_VMEM_LIMIT)


def _const_spec(shape):
    zeros = (0,) * len(shape)
    return pl.BlockSpec(shape, lambda *_: zeros, pipeline_mode=pl.Buffered(1))


def _gelu_tanh(x):
    c = math.sqrt(2.0 / math.pi)
    return 0.5 * x * (1.0 + jnp.tanh(c * (x + 0.044715 * (x * x * x))))


def _modulated_rms(x, g, shift, scale):
    ms = jnp.mean(x * x, axis=-1, keepdims=True)
    return (x * lax.rsqrt(ms + EPS)) * (g * (1.0 + scale)) + shift


def _bdot(a, b):
    return jnp.dot(a, b, preferred_element_type=F32)


def _ada_kernel(c_ref, w_ref, b_ref, o_ref):
    c = c_ref[...]
    cs = c * jax.nn.sigmoid(c)
    o_ref[...] = jnp.dot(cs, w_ref[...], preferred_element_type=F32,
                         precision=lax.Precision.HIGHEST) + b_ref[...]


def _ada_rows(c_rows, w_ada, b_ada):
    n = w_ada.shape[1]
    bn = 1536
    return pl.pallas_call(
        _ada_kernel,
        grid=(n // bn,),
        in_specs=[_const_spec(c_rows.shape),
                  pl.BlockSpec((D_MODEL, bn), lambda j: (0, j)),
                  pl.BlockSpec((1, bn), lambda j: (0, j))],
        out_specs=pl.BlockSpec((c_rows.shape[0], bn), lambda j: (0, j)),
        out_shape=jax.ShapeDtypeStruct((c_rows.shape[0], n), F32),
        compiler_params=_cparams(("arbitrary",)),
        name="ada",
    )(c_rows, w_ada, b_ada.reshape(1, n))


def _swap_slab_and_piece(slabs):
    lane = lax.broadcasted_iota(jnp.int32, (1, LANES), 1)
    out = list(slabs)
    for k in range(3):
        width = S5_GROUP << k
        keep = ((lane >> (4 + k)) & 1) == 0
        nxt = list(out)
        for a0 in range(len(out)):
            if a0 & (1 << k):
                continue
            a1 = a0 | (1 << k)
            nxt[a0] = jnp.where(keep, out[a0], pltpu.roll(out[a1], width, 1))
            nxt[a1] = jnp.where(keep, pltpu.roll(out[a0], LANES - width, 1), out[a1])
        out = nxt
    return out


def _pair_lane_block(slab_idx):
    t_hi, g = divmod(slab_idx, SUBLANES)
    pp, gg = divmod(g, 2)
    return pp, gg * 2 + t_hi


def _store_pair_tiles(u, u_scr, up_ref, row0=0):
    n_chunks = u.shape[0] // S5_CHUNK
    chunks = slice(row0 // S5_CHUNK, row0 // S5_CHUNK + n_chunks)
    for j in range(S5_SLABS):
        u_scr[j, row0:row0 + u.shape[0], :] = u[:, j * LANES:(j + 1) * LANES]
    for j in range(S5_SLABS):
        slabs = _swap_slab_and_piece(
            [u_scr[j, pl.ds(row0 + t, n_chunks, stride=S5_CHUNK), :] for t in range(S5_CHUNK)])
        for a in range(S5_CHUNK):
            pp, blk = _pair_lane_block(a)
            up_ref[j * PAIRS_PER_SLAB + pp, chunks, blk * LANES:(blk + 1) * LANES] = (
                slabs[a].astype(BF16))


def _inproj_kernel(x_ref, mod_ref, g_ref, w_ref, bgate_ref, lng_ref, lnb_ref, wsp_ref, bsp_ref,
                   wpb_ref, u_ref, ga_ref, mb_ref, u_scr):
    m = mod_ref[0]
    tt = x_ref.shape[1]
    z0 = S5_WIDTH
    g0 = S5_WIDTH + 2 * SGU_WIDTH
    lane = lax.broadcasted_iota(jnp.int32, (CHUNK, LANES), 1)
    zero = jnp.zeros((CHUNK, LANES), BF16)

    groups = [slice(k * tt // ROW_GROUPS, (k + 1) * tt // ROW_GROUPS) for k in range(ROW_GROUPS)]
    proj = []
    for r in groups:
        h = _modulated_rms(x_ref[0, r, :], g_ref[...], m[0:1], m[1:2]).astype(BF16)
        proj.append(dict(
            u=_bdot(h, w_ref[:, 0:S5_WIDTH]),
            zv=_bdot(h, w_ref[:, z0 + SGU_WIDTH:z0 + 2 * SGU_WIDTH]),
            zu=_bdot(h, w_ref[:, z0:z0 + SGU_WIDTH]),
            ga=_bdot(h, w_ref[:, g0:g0 + D_MODEL]),
            gb=_bdot(h, w_ref[:, g0 + D_MODEL:g0 + 2 * D_MODEL])))

    for r, p in zip(groups, proj):
        _store_pair_tiles(p["u"], u_scr, u_ref, r.start)

        zv = _gelu_tanh(p["zv"])
        mu = jnp.mean(zv, axis=-1, keepdims=True)
        zc = zv - mu
        var = jnp.mean(zc * zc, axis=-1, keepdims=True)
        v = ((zc * lax.rsqrt(var + EPS)) * lng_ref[...] + lnb_ref[...]).astype(BF16)
        zu = _gelu_tanh(p["zu"])

        rows = []
        for c in range(zv.shape[0] // CHUNK):
            tiles = []
            for j in range(SGU_WIDTH // LANES):
                vj = v[c * CHUNK:(c + 1) * CHUNK, j * LANES:(j + 1) * LANES]
                rhs = jnp.concatenate([jnp.where(lane < SGU_GROUP_DIM, vj, zero),
                                       jnp.where(lane >= SGU_GROUP_DIM, vj, zero)], axis=0)
                tiles.append(_bdot(wsp_ref[j], rhs))
            rows.append(jnp.concatenate(tiles, axis=1) + bsp_ref[...])
        s = jnp.concatenate(rows, axis=0)
        pb = _bdot((zu * s).astype(BF16), wpb_ref[...])

        ga = jax.nn.sigmoid(p["ga"] + bgate_ref[:, 0:D_MODEL])
        gb = jax.nn.sigmoid(p["gb"] + bgate_ref[:, D_MODEL:2 * D_MODEL])
        ga_ref[0, r, :] = ga.astype(BF16)
        mb_ref[0, r, :] = (gb * pb).astype(BF16)


def _inproj(x, mod3, g_mix, w_in, b_gate, ln_g, ln_b, wsp_pair, bsp_rows, w_proj_b):
    bsz, length, _ = x.shape
    tt = TOK_TILE
    nt = length // tt
    tok = lambda w: pl.BlockSpec((1, tt, w), lambda b, i: (b, i, 0))
    return pl.pallas_call(
        _inproj_kernel,
        grid=(bsz, nt),
        in_specs=[tok(D_MODEL),
                  pl.BlockSpec((1, N_MOD, D_MODEL), lambda b, i: (b, 0, 0)),
                  _const_spec(g_mix.shape), _const_spec(w_in.shape), _const_spec(b_gate.shape),
                  _const_spec(ln_g.shape), _const_spec(ln_b.shape), _const_spec(wsp_pair.shape),
                  _const_spec(bsp_rows.shape), _const_spec(w_proj_b.shape)],
        out_specs=[pl.BlockSpec((S5_PAIRS, tt // S5_CHUNK, PAIR_W), lambda b, i: (0, b * nt + i, 0)),
                   tok(D_MODEL), tok(D_MODEL)],
        out_shape=[jax.ShapeDtypeStruct((S5_PAIRS, bsz * length // S5_CHUNK, PAIR_W), BF16),
                   jax.ShapeDtypeStruct((bsz, length, D_MODEL), BF16),
                   jax.ShapeDtypeStruct((bsz, length, D_MODEL), BF16)],
        scratch_shapes=[pltpu.VMEM((S5_SLABS, tt, LANES), F32)],
        compiler_params=_cparams(("parallel", "parallel")),
        name="inproj",
    )(x, mod3, g_mix, w_in, b_gate, ln_g, ln_b, wsp_pair, bsp_rows, w_proj_b)


def _ctxproj_kernel(x_ref, mod_ref, g_ref, w_ref, u_ref, u_scr):
    m = mod_ref[0]
    h = _modulated_rms(x_ref[0], g_ref[...], m[0:1], m[1:2]).astype(BF16)
    _store_pair_tiles(_bdot(h, w_ref[...]), u_scr, u_ref)


def _ctxproj(ctx, mod3, ctx_row, g_mix, w_in_u):
    bsz, length, _ = ctx.shape
    return pl.pallas_call(
        _ctxproj_kernel,
        grid=(bsz,),
        in_specs=[pl.BlockSpec((1, length, D_MODEL), lambda b: (b, 0, 0)),
                  pl.BlockSpec((1, N_MOD, D_MODEL), lambda b: (ctx_row, 0, 0)),
                  _const_spec(g_mix.shape), _const_spec(w_in_u.shape)],
        out_specs=pl.BlockSpec((S5_PAIRS, length // S5_CHUNK, PAIR_W), lambda b: (0, b, 0)),
        out_shape=jax.ShapeDtypeStruct((S5_PAIRS, bsz * length // S5_CHUNK, PAIR_W), BF16),
        scratch_shapes=[pltpu.VMEM((S5_SLABS, length, LANES), F32)],
        compiler_params=_cparams(("parallel",)),
        name="ctxproj",
    )(ctx, mod3, g_mix, w_in_u)


def _cmul(ar, ai, br, bi):
    return ar * br - ai * bi, ar * bi + ai * br


def _s5_ops_kernel(a_re_ref, a_im_ref, ls_ref, bt_re_ref, bt_im_ref, c_re_ref, c_im_ref, d_ref,
                   wst_ref, wy_ref, coef_ref):
    hp = lax.Precision.HIGHEST
    n_tok = S5_CHUNK
    blk = n_tok * S5_GROUP
    tok_of_lane = lax.broadcasted_iota(jnp.int32, (1, blk), 1) // S5_GROUP
    row8 = lax.broadcasted_iota(jnp.int32, (SUBLANES, 1), 0)
    diag = (lax.broadcasted_iota(jnp.int32, (blk, blk), 0)
            == lax.broadcasted_iota(jnp.int32, (blk, blk), 1))
    contract_n = (((1,), (1,)), ((), ()))

    for gg in range(2):
        rows_g = slice(gg * blk, (gg + 1) * blk)
        cols_g = slice(gg * blk, (gg + 1) * blk)
        cols_other = slice((1 - gg) * blk, (2 - gg) * blk)
        w_intra = jnp.where(diag, d_ref[0, gg], 0.0)
        for d in range(2):
            forward = d == 0
            a_re = a_re_ref[d, 0, gg]
            a_im = a_im_ref[d, 0, gg]
            dt = jnp.exp(ls_ref[d, 0, gg])
            mag = jnp.exp(a_re * dt)
            lam = (mag * jnp.cos(a_im * dt), mag * jnp.sin(a_im * dt))
            den = a_re * a_re + a_im * a_im
            p, q = lam[0] - 1.0, lam[1]
            k = ((p * a_re + q * a_im) / den, (q * a_re - p * a_im) / den)
            bbar = _cmul(*k, bt_re_ref[d, 0, gg], bt_im_ref[d, 0, gg])
            c = (c_re_ref[d, 0, gg], c_im_ref[d, 0, gg])
            pw = [(jnp.ones_like(p), jnp.zeros_like(p))]
            for _ in range(n_tok):
                pw.append(_cmul(*pw[-1], *lam))

            expo = [n_tok - 1 - i if forward else i for i in range(n_tok)]
            cols = [_cmul(*pw[e], *bbar) for e in expo]
            m_re = jnp.concatenate([t[0] for t in cols], axis=0)
            m_im = jnp.concatenate([t[1] for t in cols], axis=0)
            for comp, m in ((2 * d, m_re), (2 * d + 1, m_im)):
                lo = comp * LANES + gg * S5_STATE
                other = comp * LANES + (1 - gg) * S5_STATE
                wst_ref[0, rows_g, lo:lo + S5_STATE] = m.astype(BF16)
                wst_ref[0, rows_g, other:other + S5_STATE] = jnp.zeros((blk, S5_STATE), BF16)

            c_rep = jnp.concatenate([jnp.concatenate([c[0], -c[1]], axis=1)] * n_tok, axis=0)
            kr = lax.dot_general(jnp.concatenate([m_re, m_im], axis=1), c_rep, contract_n,
                                 precision=hp, preferred_element_type=F32)
            toep = jnp.zeros((blk, blk), F32)
            for j in range(n_tok):
                s = S5_GROUP * (n_tok - 1 - j if forward else j)
                if s == 0:
                    moved = kr
                elif forward:
                    moved = jnp.concatenate([kr[s:], jnp.zeros((s, blk), F32)], axis=0)
                else:
                    moved = jnp.concatenate([jnp.zeros((s, blk), F32), kr[:blk - s]], axis=0)
                toep = jnp.where(tok_of_lane == j, moved, toep)
            w_intra = w_intra + toep

            expo = [j + 1 if forward else n_tok - j for j in range(n_tok)]
            rows = [_cmul(*pw[e], *c) for e in expo]
            r_re = jnp.concatenate([t[0] for t in rows], axis=0).T
            r_im = -jnp.concatenate([t[1] for t in rows], axis=0).T
            for comp, r in ((2 * d, r_re), (2 * d + 1, r_im)):
                r0 = PAIR_W + comp * LANES + gg * S5_STATE
                wy_ref[0, r0:r0 + S5_STATE, cols_g] = r.astype(BF16)
                wy_ref[0, r0:r0 + S5_STATE, cols_other] = jnp.zeros((S5_STATE, blk), BF16)

            lp = [(jnp.ones_like(p), jnp.zeros_like(p)), pw[n_tok]]
            for _ in range(SUBLANES - 1):
                lp.append(_cmul(*lp[-1], *pw[n_tok]))
            lanes = slice(d * LANES + gg * S5_STATE, d * LANES + (gg + 1) * S5_STATE)
            for kk, s in enumerate((1, 2, 4)):
                valid = (row8 >= s) if forward else (row8 + s < SUBLANES)
                coef_ref[0, 2 * kk, :, lanes] = jnp.where(valid, lp[s][0], 0.0)
                coef_ref[0, 2 * kk + 1, :, lanes] = jnp.where(valid, lp[s][1], 0.0)
            q_re = jnp.zeros((SUBLANES, S5_STATE), F32)
            q_im = jnp.zeros((SUBLANES, S5_STATE), F32)
            for r in range(SUBLANES):
                e = r + 1 if forward else SUBLANES - r
                q_re = jnp.where(row8 == r, lp[e][0], q_re)
                q_im = jnp.where(row8 == r, lp[e][1], q_im)
            coef_ref[0, 6, :, lanes] = q_re
            coef_ref[0, 7, :, lanes] = q_im

        wy_ref[0, rows_g, cols_g] = w_intra.astype(BF16)
        wy_ref[0, rows_g, cols_other] = jnp.zeros((blk, blk), BF16)


def _s5_operators(a_re, a_im, log_step, b_re, b_im, c_re, c_im, d_skip):
    grp = lambda t, *tail: t.reshape(2, S5_PAIRS, 2, *tail)
    row = lambda t: grp(t, 1, t.shape[-1])
    bt_re = grp(jnp.swapaxes(b_re, -1, -2), S5_GROUP, S5_STATE)
    bt_im = grp(jnp.swapaxes(b_im, -1, -2), S5_GROUP, S5_STATE)
    d_tiled = jnp.tile(d_skip.reshape(S5_PAIRS, 2, 1, S5_GROUP), (1, 1, 1, S5_CHUNK))
    per_pair = lambda *tail: pl.BlockSpec((2, 1, 2) + tail, lambda p: (0, p, 0) + (0,) * len(tail))
    out = lambda *tail: pl.BlockSpec((1,) + tail, lambda p: (p,) + (0,) * len(tail))
    return pl.pallas_call(
        _s5_ops_kernel,
        grid=(S5_PAIRS,),
        in_specs=[per_pair(1, S5_STATE), per_pair(1, S5_STATE), per_pair(1, 1),
                  per_pair(S5_GROUP, S5_STATE), per_pair(S5_GROUP, S5_STATE),
                  per_pair(S5_GROUP, S5_STATE), per_pair(S5_GROUP, S5_STATE),
                  pl.BlockSpec((1, 2, 1, S5_CHUNK * S5_GROUP), lambda p: (p, 0, 0, 0))],
        out_specs=[out(PAIR_W, PAIR_W), out(2 * PAIR_W, PAIR_W), out(8, SUBLANES, 2 * LANES)],
        out_shape=[jax.ShapeDtypeStruct((S5_PAIRS, PAIR_W, PAIR_W), BF16),
                   jax.ShapeDtypeStruct((S5_PAIRS, 2 * PAIR_W, PAIR_W), BF16),
                   jax.ShapeDtypeStruct((S5_PAIRS, 8, SUBLANES, 2 * LANES), F32)],
        compiler_params=_cparams(("parallel",)),
        name="s5_ops",
    )(row(a_re), row(a_im), grp(log_step, 1, 1), bt_re, bt_im,
      grp(c_re, S5_GROUP, S5_STATE), grp(c_im, S5_GROUP, S5_STATE), d_tiled)


def _s5_kernel(u_ref, uc_ref, wst_ref, wy_ref, coef_ref, y_ref, s_a, sin_a, s_b, sin_b):
    n_main = u_ref.shape[1]
    n_ctx = uc_ref.shape[1]
    n_rows = n_ctx + n_main + n_ctx
    half = PAIRS_PER_SLAB // 2
    halves = [(list(range(0, half)), s_a, sin_a), (list(range(half, PAIRS_PER_SLAB)), s_b, sin_b)]

    for pairs, s_scr, _ in halves:
        for k, pp in enumerate(pairs):
            s_ctx = _bdot(uc_ref[pp], wst_ref[pp])
            s_scr[k, 0:n_ctx] = s_ctx
            s_scr[k, n_ctx:n_ctx + n_main] = _bdot(u_ref[pp], wst_ref[pp])
            s_scr[k, n_ctx + n_main:] = s_ctx
    y_intra = [_bdot(u_ref[pp], wy_ref[pp, 0:PAIR_W]) for pp in range(PAIRS_PER_SLAB)]

    row = lax.broadcasted_iota(jnp.int32, (SUBLANES, LANES), 0)
    n_groups = n_rows // SUBLANES
    n_iter = (n_ctx + n_main) // SUBLANES

    def group(s_scr, sin_scr, k, pp, base, c_re, c_im, lane0, forward):
        rows = pl.ds(base, SUBLANES)
        re_l = slice(lane0, lane0 + LANES)
        im_l = slice(lane0 + LANES, lane0 + 2 * LANES)
        cf = slice(0, LANES) if forward else slice(LANES, 2 * LANES)
        xr = s_scr[k, rows, re_l]
        xi = s_scr[k, rows, im_l]
        for step, sh in enumerate((1, 2, 4)):
            ar = coef_ref[pp, 2 * step, :, cf]
            ai = coef_ref[pp, 2 * step + 1, :, cf]
            amt = sh if forward else SUBLANES - sh
            sr = pltpu.roll(xr, amt, 0)
            si = pltpu.roll(xi, amt, 0)
            xr, xi = xr + (ar * sr - ai * si), xi + (ar * si + ai * sr)
        qr = coef_ref[pp, 6, :, cf]
        qi = coef_ref[pp, 7, :, cf]
        hr = xr + (qr * c_re - qi * c_im)
        hi = xi + (qr * c_im + qi * c_re)
        edge = 0 if forward else SUBLANES - 1
        amt = 1 if forward else SUBLANES - 1
        sin_scr[k, rows, re_l] = jnp.where(row == edge, c_re, pltpu.roll(hr, amt, 0))
        sin_scr[k, rows, im_l] = jnp.where(row == edge, c_im, pltpu.roll(hi, amt, 0))
        last = SUBLANES - 1 if forward else 0
        return (jnp.broadcast_to(hr[last:last + 1], (SUBLANES, LANES)),
                jnp.broadcast_to(hi[last:last + 1], (SUBLANES, LANES)))

    zero = jnp.zeros((SUBLANES, LANES), F32)
    main_rows = slice(n_ctx, n_ctx + n_main)
    for pairs, s_scr, sin_scr in halves:
        carry = [zero] * (4 * len(pairs))
        for g in range(n_iter):
            new = []
            for k, pp in enumerate(pairs):
                f_re, f_im, b_re, b_im = carry[4 * k:4 * k + 4]
                f_re, f_im = group(s_scr, sin_scr, k, pp, g * SUBLANES, f_re, f_im, 0, True)
                b_re, b_im = group(s_scr, sin_scr, k, pp, (n_groups - 1 - g) * SUBLANES,
                                   b_re, b_im, 2 * LANES, False)
                new += [f_re, f_im, b_re, b_im]
            carry = new
        for k, pp in enumerate(pairs):
            y_ref[pp] = y_intra[pp] + _bdot(sin_scr[k, main_rows, :].astype(BF16),
                                            wy_ref[pp, PAIR_W:2 * PAIR_W])


def _s5(u_pairs, uc_pairs, bsz, w_state, w_y, coef):
    n_main = u_pairs.shape[1] // bsz
    n_ctx = uc_pairs.shape[1] // bsz
    n_rows = n_main + 2 * n_ctx
    pps = PAIRS_PER_SLAB
    rows = lambda n: pl.BlockSpec((pps, n, PAIR_W), lambda j, b: (j, b, 0))
    return pl.pallas_call(
        _s5_kernel,
        grid=(S5_SLABS, bsz),
        in_specs=[rows(n_main), rows(n_ctx),
                  pl.BlockSpec((pps, PAIR_W, PAIR_W), lambda j, b: (j, 0, 0)),
                  pl.BlockSpec((pps, 2 * PAIR_W, PAIR_W), lambda j, b: (j, 0, 0)),
                  pl.BlockSpec((pps, 8, SUBLANES, 2 * LANES), lambda j, b: (j, 0, 0, 0))],
        out_specs=rows(n_main),
        out_shape=jax.ShapeDtypeStruct(u_pairs.shape, F32),
        scratch_shapes=[pltpu.VMEM((pps // 2, n_rows, PAIR_W), F32)] * 4,
        compiler_params=_cparams(("arbitrary", "arbitrary")),
        name="s5",
    )(u_pairs, uc_pairs, w_state, w_y, coef)


def _pair_tiles_to_tokens(y_ref, y_scr, slot):
    n_chunks = y_ref.shape[1]
    for j in range(S5_SLABS):
        slabs = []
        for a in range(S5_CHUNK):
            pp, blk = _pair_lane_block(a)
            slabs.append(y_ref[j * PAIRS_PER_SLAB + pp, :, blk * LANES:(blk + 1) * LANES])
        slabs = _swap_slab_and_piece(slabs)
        for t in range(S5_CHUNK):
            y_scr[slot, j, pl.ds(t, n_chunks, stride=S5_CHUNK), :] = slabs[t]


def _mix_kernel(y0_ref, ynext_ref, x_ref, ga_ref, mb_ref, mod_ref, wglu_ref, bglu_ref, wpa_ref,
                wout_ref, o_ref, y_scr):
    step = pl.program_id(0)
    cur = step % 2
    m = mod_ref[0]
    tt = x_ref.shape[1]

    @pl.when(step == 0)
    def _():
        _pair_tiles_to_tokens(y0_ref, y_scr, 0)

    groups = [slice(k * tt // ROW_GROUPS, (k + 1) * tt // ROW_GROUPS) for k in range(ROW_GROUPS)]
    ya = [_gelu_tanh(jnp.concatenate([y_scr[cur, j, r, :] for j in range(S5_SLABS)], axis=1))
          for r in groups]
    gate = [_bdot(t.astype(BF16), wglu_ref[...]) for t in ya]
    glu = [t * jax.nn.sigmoid(g + bglu_ref[...]) for t, g in zip(ya, gate)]
    pa = [_bdot(t.astype(BF16), wpa_ref[...]) for t in glu]
    merged = [(ga_ref[0, r, :].astype(F32) * p + mb_ref[0, r, :].astype(F32)).astype(BF16)
              for r, p in zip(groups, pa)]
    out = [_bdot(t, wout_ref[...]) for t in merged]
    for r, t in zip(groups, out):
        o_ref[0, r, :] = x_ref[0, r, :] + m[2:3] * t

    _pair_tiles_to_tokens(ynext_ref, y_scr, 1 - cur)


def _mix(y_pairs, x, ga, mb, mod3, w_glu, b_glu, w_proj_a, w_out):
    bsz, length, _ = x.shape
    tt = TOK_TILE
    nt = length // tt
    steps = bsz * nt
    tok = lambda w: pl.BlockSpec((1, tt, w), lambda s: (s // nt, s % nt, 0))
    y_tile = lambda fn: pl.BlockSpec((S5_PAIRS, tt // S5_CHUNK, PAIR_W), fn)
    return pl.pallas_call(
        _mix_kernel,
        grid=(steps,),
        in_specs=[y_tile(lambda s: (0, 0, 0)),
                  y_tile(lambda s: (0, jnp.minimum(s + 1, steps - 1), 0)),
                  tok(D_MODEL), tok(D_MODEL), tok(D_MODEL),
                  pl.BlockSpec((1, N_MOD, D_MODEL), lambda s: (s // nt, 0, 0)),
                  _const_spec(w_glu.shape), _const_spec(b_glu.shape),
                  _const_spec(w_proj_a.shape), _const_spec(w_out.shape)],
        out_specs=tok(D_MODEL),
        out_shape=jax.ShapeDtypeStruct(x.shape, F32),
        scratch_shapes=[pltpu.VMEM((2, S5_SLABS, tt, LANES), F32)],
        compiler_params=_cparams(("arbitrary",)),
        name="mix",
    )(y_pairs, y_pairs, x, ga, mb, mod3, w_glu, b_glu, w_proj_a, w_out)


def _conv_taps(cw_ref, cb_ref, cols):
    width = cols.stop - cols.start

    def block(row):
        tile = jnp.broadcast_to(row, (2 * SUBLANES, width)).astype(BF16)
        return jnp.concatenate([tile] * (GRID_W // (2 * SUBLANES)), axis=0)

    return [block(cw_ref[k:k + 1, cols]) for k in range(9)], block(cb_ref[:, cols])


def _token_neighbours(u):
    n, ch = u.shape
    shape4 = (n // GRID_W, GRID_W // SUBLANES, SUBLANES, ch)
    sub = lax.broadcasted_iota(jnp.int32, (1, 1, SUBLANES, 1), 2)
    zero = jnp.zeros((shape4[0], 1, SUBLANES, ch), F32)
    u4 = u.reshape(shape4)
    down = pltpu.roll(u4, 1, 2)
    left = jnp.where(sub == 0, jnp.concatenate([zero, down[:, :-1]], axis=1), down)
    up = pltpu.roll(u4, SUBLANES - 1, 2)
    right = jnp.where(sub == SUBLANES - 1, jnp.concatenate([up[:, 1:], zero], axis=1), up)
    return left.reshape(n, ch).astype(BF16), u.astype(BF16), right.reshape(n, ch).astype(BF16)


def _grid_conv_row(nbrs, taps, bias, rb):
    lo = rb * GRID_W
    part = []
    for dr in range(3):
        rows = slice(lo + dr * GRID_W, lo + (dr + 1) * GRID_W)
        left, here, right = (nbrs[dc][rows] * taps[3 * dr + dc] for dc in range(3))
        part.append((left + here) + right)
    return (part[0] + part[1]) + (part[2] + bias)


def _ffn_kernel(top_ref, x_ref, bot_ref, mod_ref, g_ref, wup_ref, cw_ref, cb_ref, wdn_ref, gf_ref,
                o_ref, h_scr, act_scr):
    i = pl.program_id(1)
    n_i = pl.num_programs(1)
    m = mod_ref[0]
    g = g_ref[...]
    tt = x_ref.shape[1]
    x = x_ref[0]

    def norm(t):
        return _modulated_rms(t, g, m[3:4], m[4:5])

    h_scr[0:GRID_W] = jnp.where(i > 0, norm(top_ref[0]), 0.0).astype(BF16)
    h_scr[GRID_W:GRID_W + tt] = norm(x).astype(BF16)
    h_scr[GRID_W + tt:] = jnp.where(i < n_i - 1, norm(bot_ref[0]), 0.0).astype(BF16)

    h = h_scr[...]

    fc = FFN_CHUNK
    for j in range(FFN_HIDDEN // fc):
        gs = slice(j * fc, (j + 1) * fc)
        vs = slice(FFN_HIDDEN + j * fc, FFN_HIDDEN + (j + 1) * fc)
        ug = _token_neighbours(_bdot(h, wup_ref[:, gs]))
        uv = _token_neighbours(_bdot(h, wup_ref[:, vs]))
        g_taps, g_bias = _conv_taps(cw_ref, cb_ref, gs)
        v_taps, v_bias = _conv_taps(cw_ref, cb_ref, vs)
        for rb in range(tt // GRID_W):
            cg = _grid_conv_row(ug, g_taps, g_bias, rb)
            cv = _grid_conv_row(uv, v_taps, v_bias, rb)
            act_scr[rb * GRID_W:(rb + 1) * GRID_W, gs] = (cg * jax.nn.sigmoid(cg)) * cv

    x2 = x + m[5:6] * _bdot(act_scr[...], wdn_ref[...])
    ms = jnp.mean(x2 * x2, axis=-1, keepdims=True)
    o_ref[0] = (x2 * lax.rsqrt(ms + EPS)) * gf_ref[...]


def _ffn(x1, mod3, g_ffn, w_up, conv_w9, conv_b, w_down, g_final):
    bsz, length, _ = x1.shape
    tt = FFN_TILE
    rows_per_tile = tt // GRID_W
    n_rows = length // GRID_W
    halo = lambda fn: pl.BlockSpec((1, GRID_W, D_MODEL), fn)
    return pl.pallas_call(
        _ffn_kernel,
        grid=(bsz, length // tt),
        in_specs=[halo(lambda b, i: (b, jnp.maximum(i * rows_per_tile - 1, 0), 0)),
                  pl.BlockSpec((1, tt, D_MODEL), lambda b, i: (b, i, 0)),
                  halo(lambda b, i: (b, jnp.minimum((i + 1) * rows_per_tile, n_rows - 1), 0)),
                  pl.BlockSpec((1, N_MOD, D_MODEL), lambda b, i: (b, 0, 0)),
                  _const_spec(g_ffn.shape), _const_spec(w_up.shape), _const_spec(conv_w9.shape),
                  _const_spec(conv_b.shape), _const_spec(w_down.shape), _const_spec(g_final.shape)],
        out_specs=pl.BlockSpec((1, tt, D_MODEL), lambda b, i: (b, i, 0)),
        out_shape=jax.ShapeDtypeStruct(x1.shape, F32),
        scratch_shapes=[pltpu.VMEM((tt + 2 * GRID_W, D_MODEL), BF16),
                        pltpu.VMEM((tt, FFN_HIDDEN), BF16)],
        compiler_params=_cparams(("parallel", "parallel")),
        name="ffn",
    )(x1, x1, x1, mod3, g_ffn, w_up, conv_w9, conv_b, w_down, g_final)


def kernel(x, c, ctx, c_ctx, w_ada, b_ada, g_mix, w_in, s5_a_re, s5_a_im, s5_log_step, s5_b_re, s5_b_im, s5_c_re, s5_c_im, s5_d, s5_w_glu, s5_b_glu, sgu_ln_g, sgu_ln_b, sgu_w, sgu_b, w_proj_a, w_proj_b, b_gate, w_out, g_ffn, w_up, conv_w, conv_b, w_down, g_final):
    bsz = x.shape[0]
    row = lambda t: t.reshape(1, -1)

    ctx_row = bsz
    c_rows = jnp.zeros((SUBLANES, D_MODEL), F32).at[:bsz].set(c).at[ctx_row].set(c_ctx)
    mod3 = _ada_rows(c_rows, w_ada[0], b_ada[0]).reshape(SUBLANES, N_MOD, D_MODEL)

    w_in_b = w_in[0].astype(BF16)
    wsp = sgu_w[0].astype(BF16)
    wsp_pair = jnp.concatenate([wsp[0::2], wsp[1::2]], axis=-1)
    bsp_rows = jnp.repeat(sgu_b[0].T, SGU_GROUP_DIM, axis=1)

    u_pairs, ga, mb = _inproj(x, mod3, row(g_mix[0]), w_in_b, row(b_gate[0]), row(sgu_ln_g[0]),
                              row(sgu_ln_b[0]), wsp_pair, bsp_rows, w_proj_b[0].astype(BF16))
    uc_pairs = _ctxproj(ctx, mod3, ctx_row, row(g_mix[0]), w_in_b[:, :S5_WIDTH])

    w_state, w_y, coef = _s5_operators(
        s5_a_re[0], s5_a_im[0], s5_log_step[0], s5_b_re[0], s5_b_im[0], s5_c_re[0], s5_c_im[0],
        s5_d[0])
    y_pairs = _s5(u_pairs, uc_pairs, bsz, w_state, w_y, coef)

    x1 = _mix(y_pairs, x, ga, mb, mod3, s5_w_glu[0].astype(BF16), row(s5_b_glu[0]),
              w_proj_a[0].astype(BF16), w_out[0].astype(BF16))

    return _ffn(x1, mod3, row(g_ffn[0]), w_up[0].astype(BF16), conv_w[0].reshape(9, -1),
                row(conv_b[0]), w_down[0].astype(BF16), row(g_final))
```
